```python
import jax, jax.numpy as jnp
from jax import lax
import numpy as np

D_MODEL = 1024
BATCH = 8
SEQ = 8192
DEPTH = 4

CHUNK = 64
N_MIXERS = 2
LRU_WIDTH = D_MODEL
LRU_BLOCKS = 8
LRU_BLOCK = LRU_WIDTH // LRU_BLOCKS
CONV_WIDTH = 4
LRU_C = 8.0
FOX_HEADS = 16
FOX_HEAD_DIM = D_MODEL // FOX_HEADS
Q_BLOCK = 2 * CHUNK
D_FF = 7 * D_MODEL // 2
N_EXPERTS = 8
TOP_K = 2
RMS_EPS = 1e-6
N_LRU = (DEPTH + 1) // 2
N_FOX = DEPTH // 2
N_DENSE = (DEPTH + 1) // 2
N_MOE = DEPTH // 2

kernel_name = "hybrid_rglru_fox_moe_trunk"


def rms_norm(x, g):
    xf = x.astype(jnp.float32)
    y = xf * lax.rsqrt(jnp.mean(xf * xf, axis=-1, keepdims=True) + RMS_EPS)
    return (y * g.astype(jnp.float32)).astype(x.dtype)


def _lin_combine(c1, c2):
    a1, b1 = c1
    a2, b2 = c2
    return a1 * a2, a2 * b1 + b2


def rg_lru_mixer(h, w_in, conv_w, conv_b, w_a, b_a, w_x, b_x, log_lambda, w_out):
    B, S, _ = h.shape
    proj = h @ w_in
    y_branch, x_branch = jnp.split(proj, 2, axis=-1)
    y_branch = jax.nn.gelu(y_branch)
    xc = lax.conv_general_dilated(
        x_branch, conv_w[:, None, :].astype(x_branch.dtype), window_strides=(1,),
        padding=[(CONV_WIDTH - 1, 0)], dimension_numbers=('NWC', 'WIO', 'NWC'),
        feature_group_count=LRU_WIDTH) + conv_b
    xb = xc.reshape(B, S, LRU_BLOCKS, LRU_BLOCK)
    r = jax.nn.sigmoid(jnp.einsum('bshi,hij->bshj', xb, w_a).reshape(B, S, LRU_WIDTH) + b_a)
    gi = jax.nn.sigmoid(jnp.einsum('bshi,hij->bshj', xb, w_x).reshape(B, S, LRU_WIDTH) + b_x)
    log_a = -LRU_C * r.astype(jnp.float32) * jax.nn.softplus(-log_lambda.astype(jnp.float32))
    a = jnp.exp(log_a)
    mult = jnp.sqrt(-jnp.expm1(2.0 * log_a))
    b = mult * (gi * xc).astype(jnp.float32)
    _, hseq = lax.associative_scan(_lin_combine, (a, b), axis=1)
    return (hseq.astype(h.dtype) * y_branch) @ w_out


def fox_mixer(h, w_in, b_f, q_gain, k_gain, w_out):
    B, S, D = h.shape
    proj = h @ w_in
    q, k, v, g, f = jnp.split(proj, [D, 2 * D, 3 * D, 4 * D], axis=-1)
    q = rms_norm(q.reshape(B, S, FOX_HEADS, FOX_HEAD_DIM), q_gain)
    k = rms_norm(k.reshape(B, S, FOX_HEADS, FOX_HEAD_DIM), k_gain)
    v = v.reshape(B, S, FOX_HEADS, FOX_HEAD_DIM)
    log_f = jax.nn.log_sigmoid(f.astype(jnp.float32) + b_f.astype(jnp.float32))
    F = jnp.cumsum(log_f, axis=1).transpose(0, 2, 1)
    qh = q.transpose(0, 2, 1, 3)
    kh = k.transpose(0, 2, 1, 3)
    vh = v.transpose(0, 2, 1, 3)
    nb = S // Q_BLOCK
    q_blocks = qh.reshape(B, FOX_HEADS, nb, Q_BLOCK, FOX_HEAD_DIM).transpose(2, 0, 1, 3, 4)
    F_blocks = F.reshape(B, FOX_HEADS, nb, Q_BLOCK).transpose(2, 0, 1, 3)
    key_pos = jnp.arange(S)
    scale = FOX_HEAD_DIM ** -0.5

    def attend(args):
        qb, Fq, blk = args
        s = jnp.einsum('bhqd,bhkd->bhqk', qb, kh, preferred_element_type=jnp.float32) * scale
        s = s + Fq[..., None] - F[:, :, None, :]
        q_pos = blk * Q_BLOCK + jnp.arange(Q_BLOCK)
        s = jnp.where(key_pos[None, :] <= q_pos[:, None], s, -jnp.inf)
        p = jax.nn.softmax(s, axis=-1)
        return jnp.einsum('bhqk,bhkd->bhqd', p.astype(vh.dtype), vh)

    o = lax.map(attend, (q_blocks, F_blocks, jnp.arange(nb)))
    o = o.transpose(1, 0, 3, 2, 4).reshape(B, S, D)
    o = o * jax.nn.sigmoid(g)
    return o @ w_out


def swiglu(h, w_gate, w_up, w_down):
    return (jax.nn.silu(h @ w_gate) * (h @ w_up)) @ w_down


def moe_swiglu(h, router, w_gate, w_up, w_down):
    B, S, D = h.shape
    t = h.reshape(B * S, D)
    logits = (t @ router).astype(jnp.float32)
    top_v, top_i = lax.top_k(logits, TOP_K)
    w = jax.nn.softmax(top_v, axis=-1)
    combine = jnp.sum(jax.nn.one_hot(top_i, N_EXPERTS, dtype=jnp.float32) * w[..., None], axis=1)
    out = jnp.zeros_like(t)
    for e in range(N_EXPERTS):
        out = out + combine[:, e:e + 1].astype(t.dtype) * swiglu(t, w_gate[e], w_up[e], w_down[e])
    return out.reshape(B, S, D)


def setup_inputs(seed: int = 0) -> dict:
    key = jax.random.key(seed)
    ks = jax.random.split(key, 24)
    f32 = jnp.float32
    nrm = lambda k, shape, s: jax.random.normal(k, shape, f32) * s
    D = D_MODEL
    u = jax.random.uniform(ks[8], (N_LRU, LRU_WIDTH), f32, minval=0.9, maxval=0.999)
    a0 = u ** (1.0 / LRU_C)
    log_lambda = jnp.log(a0) - jnp.log1p(-a0)
    return {
        "x": nrm(ks[0], (BATCH, SEQ, D), 1.0),
        "norm_mix": 1.0 + nrm(ks[1], (DEPTH, D), 0.01),
        "norm_ffn": 1.0 + nrm(ks[2], (DEPTH, D), 0.01),
        "lru_w_in": nrm(ks[3], (N_LRU, D, 2 * LRU_WIDTH), D ** -0.5),
        "lru_conv_w": nrm(ks[4], (N_LRU, CONV_WIDTH, LRU_WIDTH), CONV_WIDTH ** -0.5),
        "lru_conv_b": nrm(ks[5], (N_LRU, LRU_WIDTH), 0.01),
        "lru_w_a": nrm(ks[6], (N_LRU, LRU_BLOCKS, LRU_BLOCK, LRU_BLOCK), LRU_BLOCK ** -0.5),
        "lru_b_a": nrm(ks[7], (N_LRU, LRU_WIDTH), 0.01),
        "lru_w_x": nrm(ks[9], (N_LRU, LRU_BLOCKS, LRU_BLOCK, LRU_BLOCK), LRU_BLOCK ** -0.5),
        "lru_b_x": nrm(ks[10], (N_LRU, LRU_WIDTH), 0.01),
        "lru_log_lambda": log_lambda,
        "lru_w_out": nrm(ks[11], (N_LRU, LRU_WIDTH, D), LRU_WIDTH ** -0.5),
        "fox_w_in": nrm(ks[12], (N_FOX, D, 4 * D + FOX_HEADS), D ** -0.5),
        "fox_b_f": jax.random.uniform(ks[13], (N_FOX, FOX_HEADS), f32, minval=1.0, maxval=6.0),
        "fox_q_gain": 1.0 + nrm(ks[14], (N_FOX, FOX_HEAD_DIM), 0.01),
        "fox_k_gain": 1.0 + nrm(ks[15], (N_FOX, FOX_HEAD_DIM), 0.01),
        "fox_w_out": nrm(ks[16], (N_FOX, D, D), D ** -0.5),
        "ffn_w_gate": nrm(ks[17], (N_DENSE, D, D_FF), D ** -0.5),
        "ffn_w_up": nrm(ks[18], (N_DENSE, D, D_FF), D ** -0.5),
        "ffn_w_down": nrm(ks[19], (N_DENSE, D_FF, D), D_FF ** -0.5),
        "moe_router": nrm(ks[20], (N_MOE, D, N_EXPERTS), D ** -0.5),
        "moe_w_gate": nrm(ks[21], (N_MOE, N_EXPERTS, D, D_FF), D ** -0.5),
        "moe_w_up": nrm(ks[22], (N_MOE, N_EXPERTS, D, D_FF), D ** -0.5),
        "moe_w_down": nrm(ks[23], (N_MOE, N_EXPERTS, D_FF, D), D_FF ** -0.5),
    }


def reference(x, norm_mix, norm_ffn, lru_w_in, lru_conv_w, lru_conv_b, lru_w_a, lru_b_a,
              lru_w_x, lru_b_x, lru_log_lambda, lru_w_out, fox_w_in, fox_b_f, fox_q_gain,
              fox_k_gain, fox_w_out, ffn_w_gate, ffn_w_up, ffn_w_down, moe_router,
              moe_w_gate, moe_w_up, moe_w_down):
    for i in range(DEPTH):
        j = i // 2
        hn = rms_norm(x, norm_mix[i])
        if i % N_MIXERS == 0:
            x = x + rg_lru_mixer(hn, lru_w_in[j], lru_conv_w[j], lru_conv_b[j], lru_w_a[j],
                                 lru_b_a[j], lru_w_x[j], lru_b_x[j], lru_log_lambda[j], lru_w_out[j])
        else:
            x = x + fox_mixer(hn, fox_w_in[j], fox_b_f[j], fox_q_gain[j], fox_k_gain[j], fox_w_out[j])
        hn = rms_norm(x, norm_ffn[i])
        if i % 2 == 0:
            x = x + swiglu(hn, ffn_w_gate[j], ffn_w_up[j], ffn_w_down[j])
        else:
            x = x + moe_swiglu(hn, moe_router[j], moe_w_gate[j], moe_w_up[j], moe_w_down[j])
    return x
```

```python
import functools

import jax
import jax.numpy as jnp
from jax import lax
from jax.experimental import pallas as pl
from jax.experimental.pallas import tpu as pltpu

RMS_EPS = 1e-6
LRU_C = 8.0
LRU_BLOCKS = 8
CONV_WIDTH = 4
FOX_HEADS = 16
N_EXPERTS = 8
LANES = 128
SUBLANES = 8
NEG_BIG = -1e30

F32 = jnp.float32
BF16 = jnp.bfloat16


def _dot(a, b):
    return jnp.dot(a, b, preferred_element_type=F32)


def _rms_rows(x, g):
    ms = jnp.mean(x * x, axis=-1, keepdims=True)
    return x * lax.rsqrt(ms + RMS_EPS) * g


def _sigmoid(x):
    return 1.0 / (1.0 + jnp.exp(-x))


def _softplus(x):
    return jnp.maximum(x, 0.0) + jnp.log1p(jnp.exp(-jnp.abs(x)))


def _gelu_tanh(x):
    return 0.5 * x * (1.0 + jnp.tanh(0.7978845608028654 * (x + 0.044715 * (x * x * x))))


def _split3(x):
    hi = x.astype(BF16)
    r1 = x - hi.astype(F32)
    mid = r1.astype(BF16)
    lo = (r1 - mid.astype(F32)).astype(BF16)
    return hi, mid, lo


def _lru_kernel(x_ref, g_ref, win_ref, cw_ref, cb_ref, wgate_ref, ba_ref, bx_ref, lam_ref, wout_ref,
                o_ref, xb_scr, hy_scr, carry_scr):
    t = pl.program_id(1)
    ts, d = x_ref.shape
    w = d

    @pl.when(t == 0)
    def _():
        xb_scr[0:SUBLANES, :] = jnp.zeros((SUBLANES, w), F32)
        carry_scr[...] = jnp.zeros_like(carry_scr)

    x = x_ref[...]
    hn = _rms_rows(x, g_ref[...]).astype(BF16)
    proj = _dot(hn, win_ref[...])
    xb = proj[:, w:]
    xb_scr[SUBLANES:SUBLANES + ts, :] = xb
    xc = cw_ref[3:4, :] * xb + cb_ref[...]
    for k in range(1, CONV_WIDTH):
        xc = xc + cw_ref[3 - k:4 - k, :] * xb_scr[SUBLANES - k:SUBLANES - k + ts, :]
    xb_scr[0:SUBLANES, :] = xb[ts - SUBLANES:, :]
    xcb = xc.astype(BF16)

    sp = _softplus(-lam_ref[...])
    rowmod = lax.broadcasted_iota(jnp.int32, (ts, LANES), 0) % SUBLANES
    for h in range(LRU_BLOCKS):
        sl = slice(h * LANES, (h + 1) * LANES)
        z = _dot(xcb[:, sl], wgate_ref[h])
        r = _sigmoid(z[:, :LANES] + ba_ref[:, sl])
        gi = _sigmoid(z[:, LANES:] + bx_ref[:, sl])
        log_a = (-LRU_C) * r * sp[:, sl]
        a = jnp.exp(log_a)
        b = jnp.sqrt(-jnp.tanh(log_a) * (a * a + 1.0)) * (gi * xc[:, sl])
        for sh in (1, 2, 4):
            keep = rowmod >= sh
            a_s = jnp.where(keep, pltpu.roll(a, sh, 0), 1.0)
            b_s = jnp.where(keep, pltpu.roll(b, sh, 0), 0.0)
            b = a * b_s + b
            a = a * a_s
        carry = carry_scr[h:h + 1, :]
        outs = []
        for gidx in range(ts // SUBLANES):
            rows = slice(gidx * SUBLANES, (gidx + 1) * SUBLANES)
            hg = a[rows, :] * carry + b[rows, :]
            outs.append(hg)
            carry = hg[SUBLANES - 1:SUBLANES, :]
        carry_scr[h:h + 1, :] = carry
        hseq = jnp.concatenate(outs, axis=0)
        yb = _gelu_tanh(proj[:, sl])
        hy_scr[:, sl] = (hseq * yb).astype(BF16)

    o_ref[...] = x + _dot(hy_scr[...], wout_ref[...])


def _lru_layer(x, g, w_in, conv_w, conv_b, w_a, b_a, w_x, b_x, log_lambda, w_out, *, ts=256):
    bsz, s, d = x.shape
    w = d
    wgate = jnp.concatenate([w_a, w_x], axis=-1).astype(BF16)
    row = lambda v: v.reshape(1, -1).astype(F32)
    const2 = lambda shape: pl.BlockSpec(shape, lambda b, t: (0, 0))
    return pl.pallas_call(
        _lru_kernel,
        grid=(bsz, s // ts),
        in_specs=[
            pl.BlockSpec((None, ts, d), lambda b, t: (b, t, 0)),
            const2((1, d)),
            const2((d, 2 * w)),
            const2((CONV_WIDTH, w)),
            const2((1, w)),
            pl.BlockSpec((LRU_BLOCKS, LANES, 2 * LANES), lambda b, t: (0, 0, 0)),
            const2((1, w)),
            const2((1, w)),
            const2((1, w)),
            const2((w, d)),
        ],
        out_specs=pl.BlockSpec((None, ts, d), lambda b, t: (b, t, 0)),
        out_shape=jax.ShapeDtypeStruct(x.shape, F32),
        scratch_shapes=[
            pltpu.VMEM((ts + SUBLANES, w), F32),
            pltpu.VMEM((ts, w), BF16),
            pltpu.VMEM((LRU_BLOCKS, LANES), F32),
        ],
        compiler_params=pltpu.CompilerParams(
            dimension_semantics=("arbitrary", "arbitrary"), vmem_limit_bytes=48 * 1024 * 1024),
        name="lru_mixer",
    )(x, row(g), w_in.astype(BF16), conv_w.astype(F32), row(conv_b), wgate, row(b_a), row(b_x),
      row(log_lambda), w_out.astype(BF16))


def _fox_proj_kernel(x_ref, g_ref, w_ref, wf_ref, bf_ref, qg_ref, kg_ref, hm_ref,
                     q_ref, k_ref, v_ref, gate_ref, ft_ref, carry_scr):
    t = pl.program_id(1)
    ts, d = x_ref.shape

    @pl.when(t == 0)
    def _():
        carry_scr[...] = jnp.zeros_like(carry_scr)

    hn = _rms_rows(x_ref[...], g_ref[...]).astype(BF16)
    proj = _dot(hn, w_ref[...])
    hm = hm_ref[...]

    def head_norm(z, gain):
        ms = _dot((z * z).astype(BF16), hm)
        return z * lax.rsqrt(ms + RMS_EPS) * gain

    q_ref[...] = head_norm(proj[:, :d], qg_ref[...]).astype(BF16)
    k_ref[...] = head_norm(proj[:, d:2 * d], kg_ref[...]).astype(BF16)
    v_ref[...] = proj[:, 2 * d:3 * d].astype(BF16)
    gate_ref[...] = _sigmoid(proj[:, 3 * d:]).astype(BF16)

    f = _dot(hn, wf_ref[...]) + bf_ref[...]
    lf = -_softplus(-f)
    r_i = lax.broadcasted_iota(jnp.int32, (ts, ts), 0)
    c_i = lax.broadcasted_iota(jnp.int32, (ts, ts), 1)
    ltri = jnp.where(c_i <= r_i, 1.0, 0.0).astype(BF16)
    hi, mid, lo = _split3(lf)
    fcum = (_dot(ltri, hi) + _dot(ltri, mid)) + _dot(ltri, lo) + carry_scr[...]
    carry_scr[...] = fcum[ts - 1:ts, :]
    ft_ref[...] = fcum.T[:FOX_HEADS, :]


def _fox_proj(x, g, w_in, b_f, q_gain, k_gain, *, ts=256):
    bsz, s, d = x.shape
    hd = d // FOX_HEADS
    w_main = w_in[:, :4 * d].astype(BF16)
    w_f = jnp.pad(w_in[:, 4 * d:], ((0, 0), (0, LANES - FOX_HEADS))).astype(BF16)
    bf_row = jnp.pad(b_f.astype(F32), (0, LANES - FOX_HEADS)).reshape(1, LANES)
    scale = hd ** -0.5
    qg_row = (jnp.tile(q_gain.astype(F32), FOX_HEADS) * scale).reshape(1, d)
    kg_row = jnp.tile(k_gain.astype(F32), FOX_HEADS).reshape(1, d)
    head_id = jnp.arange(d) // hd
    hm = jnp.where(head_id[:, None] == head_id[None, :], 1.0 / hd, 0.0).astype(BF16)
    const2 = lambda shape: pl.BlockSpec(shape, lambda b, t: (0, 0))
    tok = pl.BlockSpec((None, ts, d), lambda b, t: (b, t, 0))
    act = jax.ShapeDtypeStruct((bsz, s, d), BF16)
    return pl.pallas_call(
        _fox_proj_kernel,
        grid=(bsz, s // ts),
        in_specs=[tok, const2((1, d)), const2((d, 4 * d)), const2((d, LANES)), const2((1, LANES)),
                  const2((1, d)), const2((1, d)), const2((d, d))],
        out_specs=[tok, tok, tok, tok, pl.BlockSpec((None, FOX_HEADS, ts), lambda b, t: (b, 0, t))],
        out_shape=[act, act, act, act, jax.ShapeDtypeStruct((bsz, FOX_HEADS, s), F32)],
        scratch_shapes=[pltpu.VMEM((1, LANES), F32)],
        compiler_params=pltpu.CompilerParams(
            dimension_semantics=("arbitrary", "arbitrary"), vmem_limit_bytes=48 * 1024 * 1024),
        name="fox_proj",
    )(x, g.reshape(1, d).astype(F32), w_main, w_f, bf_row, qg_row, kg_row, hm)


def _fox_attn_kernel(q_ref, k_ref, v_ref, ft_ref, o_ref, ka_scr, kb_scr, m_scr, l_scr, acc_scr):
    qi = pl.program_id(2)
    tq = q_ref.shape[0]
    tk = tq
    half = LANES // 2
    lane = lax.broadcasted_iota(jnp.int32, (1, LANES), 1)

    @pl.when(qi == 0)
    def _():
        kk = k_ref[...]
        lane_k = lax.broadcasted_iota(jnp.int32, kk.shape, 1)
        zero = jnp.zeros_like(kk)
        ka_scr[...] = jnp.where(lane_k < half, kk, zero)
        kb_scr[...] = jnp.where(lane_k >= half, kk, zero)

    m_scr[...] = jnp.full_like(m_scr, NEG_BIG)
    l_scr[...] = jnp.zeros_like(l_scr)
    acc_scr[...] = jnp.zeros_like(acc_scr)
    q = q_ref[...]

    def attend(j, masked):
        start = pl.multiple_of(j * tk, tk)
        v = v_ref[pl.ds(start, tk), :]
        for hd, k_scr in ((0, ka_scr), (1, kb_scr)):
            kb = k_scr[pl.ds(start, tk), :]
            s = lax.dot_general(q, kb, (((1,), (1,)), ((), ())), preferred_element_type=F32)
            s = s - ft_ref[hd, pl.ds(j, 1), :]
            if masked:
                r_i = lax.broadcasted_iota(jnp.int32, (tq, tk), 0)
                c_i = lax.broadcasted_iota(jnp.int32, (tq, tk), 1)
                s = jnp.where(c_i <= r_i, s, NEG_BIG)
            m_prev = m_scr[hd]
            m_new = jnp.maximum(m_prev, jnp.max(s, axis=1, keepdims=True))
            p = jnp.exp(s - m_new[:, :1])
            alpha = jnp.exp(m_prev - m_new)
            l_scr[hd] = alpha * l_scr[hd] + jnp.sum(p, axis=1, keepdims=True)
            acc_scr[hd] = alpha * acc_scr[hd] + _dot(p.astype(BF16), v)
            m_scr[hd] = m_new

    def body(j, carry):
        attend(j, False)
        return carry

    lax.fori_loop(0, qi, body, 0)
    attend(qi, True)

    oa = acc_scr[0] / l_scr[0]
    ob = acc_scr[1] / l_scr[1]
    o_ref[...] = jnp.where(lane < half, oa, ob).astype(o_ref.dtype)


def _fox_attn(q, k, v, ft, *, tq=512):
    bsz, s, d = q.shape
    pairs = d // LANES
    nq = s // tq
    ft5 = ft.reshape(bsz, pairs, 2, nq, tq)
    return pl.pallas_call(
        _fox_attn_kernel,
        grid=(bsz, pairs, nq),
        in_specs=[
            pl.BlockSpec((None, tq, LANES), lambda b, p, i: (b, i, p)),
            pl.BlockSpec((None, s, LANES), lambda b, p, i: (b, 0, p)),
            pl.BlockSpec((None, s, LANES), lambda b, p, i: (b, 0, p)),
            pl.BlockSpec((None, None, 2, nq, tq), lambda b, p, i: (b, p, 0, 0, 0)),
        ],
        out_specs=pl.BlockSpec((None, tq, LANES), lambda b, p, i: (b, i, p)),
        out_shape=jax.ShapeDtypeStruct((bsz, s, d), BF16),
        scratch_shapes=[
            pltpu.VMEM((s, LANES), BF16),
            pltpu.VMEM((s, LANES), BF16),
            pltpu.VMEM((2, tq, LANES), F32),
            pltpu.VMEM((2, tq, LANES), F32),
            pltpu.VMEM((2, tq, LANES), F32),
        ],
        compiler_params=pltpu.CompilerParams(
            dimension_semantics=("arbitrary", "arbitrary", "arbitrary"),
            vmem_limit_bytes=48 * 1024 * 1024),
        name="fox_attn",
    )(q, k, v, ft5)


def _fox_out_kernel(x_ref, o_ref, gate_ref, w_ref, out_ref):
    og = (o_ref[...].astype(F32) * gate_ref[...].astype(F32)).astype(BF16)
    out_ref[...] = x_ref[...] + _dot(og, w_ref[...])


def _fox_out(x2, o2, gate2, w_out, *, tm=512):
    t, d = x2.shape
    tok = pl.BlockSpec((tm, d), lambda i: (i, 0))
    return pl.pallas_call(
        _fox_out_kernel,
        grid=(t // tm,),
        in_specs=[tok, tok, tok, pl.BlockSpec((d, d), lambda i: (0, 0))],
        out_specs=tok,
        out_shape=jax.ShapeDtypeStruct((t, d), F32),
        compiler_params=pltpu.CompilerParams(dimension_semantics=("arbitrary",)),
        name="fox_out",
    )(x2, o2, gate2, w_out.astype(BF16))


def _ffn_kernel(*refs, routed):
    if routed:
        (x_ref, g_ref, rhi_ref, rlo_ref, wg_ref, wu_ref, wd_ref, o_ref, hn_scr, acc_scr, comb_scr) = refs
    else:
        (x_ref, g_ref, wg_ref, wu_ref, wd_ref, o_ref, hn_scr, acc_scr) = refs
    e = pl.program_id(1)
    f = pl.program_id(2)
    last = (e == pl.num_programs(1) - 1) & (f == pl.num_programs(2) - 1)

    @pl.when((e == 0) & (f == 0))
    def _():
        x = x_ref[...]
        hn = _rms_rows(x, g_ref[...])
        hn_scr[...] = hn.astype(BF16)
        acc_scr[...] = x
        if routed:
            tm = x.shape[0]
            h_hi = hn.astype(BF16)
            h_lo = (hn - h_hi.astype(F32)).astype(BF16)
            logits = (_dot(h_hi, rhi_ref[...]) + _dot(h_lo, rhi_ref[...])) + _dot(h_hi, rlo_ref[...])
            lane = lax.broadcasted_iota(jnp.int32, (tm, LANES), 1)
            logits = jnp.where(lane < N_EXPERTS, logits, NEG_BIG)
            m1 = jnp.max(logits, axis=1, keepdims=True)
            i1 = jnp.min(jnp.where(logits == m1, lane, LANES), axis=1, keepdims=True)
            rest = jnp.where(lane == i1, NEG_BIG, logits)
            m2 = jnp.max(rest, axis=1, keepdims=True)
            i2 = jnp.min(jnp.where(rest == m2, lane, LANES), axis=1, keepdims=True)
            e2 = jnp.exp(m2 - m1)
            w1 = 1.0 / (1.0 + e2)
            w2 = e2 / (1.0 + e2)
            zeros = jnp.zeros((tm, LANES), F32)
            for ex in range(N_EXPERTS):
                comb_scr[ex] = jnp.where(i1 == ex, w1, zeros) + jnp.where(i2 == ex, w2, zeros)

    hn = hn_scr[...]
    gg = _dot(hn, wg_ref[...])
    uu = _dot(hn, wu_ref[...])
    act = (gg * _sigmoid(gg)) * uu
    if routed:
        tf = act.shape[1]
        act = act * jnp.tile(comb_scr[e], (1, tf // LANES))
    acc_scr[...] += _dot(act.astype(BF16), wd_ref[...])

    @pl.when(last)
    def _():
        o_ref[...] = acc_scr[...]


def _ffn_layer(x2, g, w_gate, w_up, w_down, router=None, *, tm=512, tf=896):
    t, d = x2.shape
    n_e, _, dff = w_gate.shape
    routed = router is not None
    tok = pl.BlockSpec((tm, d), lambda i, e, f: (i, 0))
    const = lambda shape: pl.BlockSpec(shape, lambda i, e, f: (0, 0))
    in_specs = [tok, const((1, d))]
    args = [x2, g.reshape(1, d).astype(F32)]
    scratch = [pltpu.VMEM((tm, d), BF16), pltpu.VMEM((tm, d), F32)]
    if routed:
        r_pad = jnp.pad(router.astype(F32), ((0, 0), (0, LANES - n_e)))
        r_hi = r_pad.astype(BF16)
        r_lo = (r_pad - r_hi.astype(F32)).astype(BF16)
        in_specs += [const((d, LANES)), const((d, LANES))]
        args += [r_hi, r_lo]
        scratch.append(pltpu.VMEM((n_e, tm, LANES), F32))
    in_specs += [
        pl.BlockSpec((None, d, tf), lambda i, e, f: (e, 0, f)),
        pl.BlockSpec((None, d, tf), lambda i, e, f: (e, 0, f)),
        pl.BlockSpec((None, tf, d), lambda i, e, f: (e, f, 0)),
    ]
    args += [w_gate.astype(BF16), w_up.astype(BF16), w_down.astype(BF16)]
    return pl.pallas_call(
        functools.partial(_ffn_kernel, routed=routed),
        grid=(t // tm, n_e, dff // tf),
        in_specs=in_specs,
        out_specs=tok,
        out_shape=jax.ShapeDtypeStruct((t, d), F32),
        scratch_shapes=scratch,
        compiler_params=pltpu.CompilerParams(
            dimension_semantics=("arbitrary", "arbitrary", "arbitrary"),
            vmem_limit_bytes=48 * 1024 * 1024),
        name="moe_ffn" if routed else "dense_ffn",
    )(*args)


def kernel(x, norm_mix, norm_ffn, lru_w_in, lru_conv_w, lru_conv_b, lru_w_a, lru_b_a, lru_w_x, lru_b_x,
           lru_log_lambda, lru_w_out, fox_w_in, fox_b_f, fox_q_gain, fox_k_gain, fox_w_out, ffn_w_gate,
           ffn_w_up, ffn_w_down, moe_router, moe_w_gate, moe_w_up, moe_w_down):
    bsz, s, d = x.shape
    depth = norm_mix.shape[0]
    for i in range(depth):
        j = i // 2
        if i % 2 == 0:
            x = _lru_layer(x, norm_mix[i], lru_w_in[j], lru_conv_w[j], lru_conv_b[j], lru_w_a[j],
                           lru_b_a[j], lru_w_x[j], lru_b_x[j], lru_log_lambda[j], lru_w_out[j])
            x2 = _ffn_layer(x.reshape(bsz * s, d), norm_ffn[i], ffn_w_gate[j][None], ffn_w_up[j][None],
                            ffn_w_down[j][None])
        else:
            q, k, v, gate, ft = _fox_proj(x, norm_mix[i], fox_w_in[j], fox_b_f[j], fox_q_gain[j],
                                          fox_k_gain[j])
            o = _fox_attn(q, k, v, ft)
            x2 = _fox_out(x.reshape(bsz * s, d), o.reshape(bsz * s, d), gate.reshape(bsz * s, d),
                          fox_w_out[j])
            x2 = _ffn_layer(x2, norm_ffn[i], moe_w_gate[j], moe_w_up[j], moe_w_down[j], moe_router[j])
        x = x2.reshape(bsz, s, d)
    return x
```

```python
import functools

import jax
import jax.numpy as jnp
from jax import lax
from jax.experimental import pallas as pl
from jax.experimental.pallas import tpu as pltpu

RMS_EPS = 1e-6
LRU_C = 8.0
LRU_BLOCKS = 8
CONV_WIDTH = 4
FOX_HEADS = 16
N_EXPERTS = 8
LANES = 128
SUBLANES = 8
NEG_BIG = -1e30
LOG2E = 1.4426950408889634
ATTN_SHIFT_MAX = 40.0

F32 = jnp.float32
BF16 = jnp.bfloat16


def _dot(a, b):
    return jnp.dot(a, b, preferred_element_type=F32)


def _rms_rows(x, g):
    ms = jnp.mean(x * x, axis=-1, keepdims=True)
    return x * lax.rsqrt(ms + RMS_EPS) * g


def _sigmoid(x):
    return 1.0 / (1.0 + jnp.exp(-x))


def _softplus(x):
    return jnp.maximum(x, 0.0) + jnp.log1p(jnp.exp(-jnp.abs(x)))


def _gelu_tanh(x):
    return 0.5 * x * (1.0 + jnp.tanh(0.7978845608028654 * (x + 0.044715 * (x * x * x))))


def _split3(x):
    hi = x.astype(BF16)
    r1 = x - hi.astype(F32)
    mid = r1.astype(BF16)
    lo = (r1 - mid.astype(F32)).astype(BF16)
    return hi, mid, lo


def _lru_kernel(x_ref, g_ref, win_ref, cw_ref, cb_ref, wgate_ref, ba_ref, bx_ref, lam_ref, wout_ref,
                o_ref, xb_scr, hy_scr, carry_scr):
    t = pl.program_id(1)
    ts, d = x_ref.shape
    w = d

    @pl.when(t == 0)
    def _():
        xb_scr[0:SUBLANES, :] = jnp.zeros((SUBLANES, w), F32)
        carry_scr[...] = jnp.zeros_like(carry_scr)

    x = x_ref[...]
    hn = _rms_rows(x, g_ref[...]).astype(BF16)
    proj = _dot(hn, win_ref[...])
    xb = proj[:, w:]
    xb_scr[SUBLANES:SUBLANES + ts, :] = xb
    xc = cw_ref[3:4, :] * xb + cb_ref[...]
    for k in range(1, CONV_WIDTH):
        xc = xc + cw_ref[3 - k:4 - k, :] * xb_scr[SUBLANES - k:SUBLANES - k + ts, :]
    xb_scr[0:SUBLANES, :] = xb[ts - SUBLANES:, :]
    xcb = xc.astype(BF16)

    sp = _softplus(-lam_ref[...])
    rowmod = lax.broadcasted_iota(jnp.int32, (ts, LANES), 0) % SUBLANES
    for h in range(LRU_BLOCKS):
        sl = slice(h * LANES, (h + 1) * LANES)
        z = _dot(xcb[:, sl], wgate_ref[h])
        r = _sigmoid(z[:, :LANES] + ba_ref[:, sl])
        gi = _sigmoid(z[:, LANES:] + bx_ref[:, sl])
        log_a = (-LRU_C) * r * sp[:, sl]
        a = jnp.exp(log_a)
        b = jnp.sqrt(-jnp.tanh(log_a) * (a * a + 1.0)) * (gi * xc[:, sl])
        for sh in (1, 2, 4):
            keep = rowmod >= sh
            a_s = jnp.where(keep, pltpu.roll(a, sh, 0), 1.0)
            b_s = jnp.where(keep, pltpu.roll(b, sh, 0), 0.0)
            b = a * b_s + b
            a = a * a_s
        carry = carry_scr[h:h + 1, :]
        outs = []
        for gidx in range(ts // SUBLANES):
            rows = slice(gidx * SUBLANES, (gidx + 1) * SUBLANES)
            hg = a[rows, :] * carry + b[rows, :]
            outs.append(hg)
            carry = hg[SUBLANES - 1:SUBLANES, :]
        carry_scr[h:h + 1, :] = carry
        hseq = jnp.concatenate(outs, axis=0)
        yb = _gelu_tanh(proj[:, sl])
        hy_scr[:, sl] = (hseq * yb).astype(BF16)

    o_ref[...] = x + _dot(hy_scr[...], wout_ref[...])


def _lru_layer(x, g, w_in, conv_w, conv_b, w_a, b_a, w_x, b_x, log_lambda, w_out, *, ts=256):
    bsz, s, d = x.shape
    w = d
    wgate = jnp.concatenate([w_a, w_x], axis=-1).astype(BF16)
    row = lambda v: v.reshape(1, -1).astype(F32)
    const2 = lambda shape: pl.BlockSpec(shape, lambda b, t: (0, 0))
    return pl.pallas_call(
        _lru_kernel,
        grid=(bsz, s // ts),
        in_specs=[
            pl.BlockSpec((None, ts, d), lambda b, t: (b, t, 0)),
            const2((1, d)),
            const2((d, 2 * w)),
            const2((CONV_WIDTH, w)),
            const2((1, w)),
            pl.BlockSpec((LRU_BLOCKS, LANES, 2 * LANES), lambda b, t: (0, 0, 0)),
            const2((1, w)),
            const2((1, w)),
            const2((1, w)),
            const2((w, d)),
        ],
        out_specs=pl.BlockSpec((None, ts, d), lambda b, t: (b, t, 0)),
        out_shape=jax.ShapeDtypeStruct(x.shape, F32),
        scratch_shapes=[
            pltpu.VMEM((ts + SUBLANES, w), F32),
            pltpu.VMEM((ts, w), BF16),
            pltpu.VMEM((LRU_BLOCKS, LANES), F32),
        ],
        compiler_params=pltpu.CompilerParams(
            dimension_semantics=("arbitrary", "arbitrary"), vmem_limit_bytes=48 * 1024 * 1024),
        name="lru_mixer",
    )(x, row(g), w_in.astype(BF16), conv_w.astype(F32), row(conv_b), wgate, row(b_a), row(b_x),
      row(log_lambda), w_out.astype(BF16))


def _fox_proj_kernel(x_ref, g_ref, w_ref, wf_ref, bf_ref, qg_ref, kg_ref, hm_ref,
                     q_ref, k_ref, v_ref, gate_ref, f_ref, carry_scr):
    t = pl.program_id(1)
    ts, d = x_ref.shape

    @pl.when(t == 0)
    def _():
        carry_scr[...] = jnp.zeros_like(carry_scr)

    hn = _rms_rows(x_ref[...], g_ref[...]).astype(BF16)
    proj = _dot(hn, w_ref[...])
    hm = hm_ref[...]

    def head_norm(z, gain):
        ms = _dot((z * z).astype(BF16), hm)
        return z * lax.rsqrt(ms + RMS_EPS) * gain

    q_ref[...] = head_norm(proj[:, :d], qg_ref[...]).astype(BF16)
    k_ref[...] = head_norm(proj[:, d:2 * d], kg_ref[...]).astype(BF16)
    v_ref[...] = proj[:, 2 * d:3 * d].astype(BF16)
    gate_ref[...] = _sigmoid(proj[:, 3 * d:]).astype(BF16)

    f = _dot(hn, wf_ref[...]) + bf_ref[...]
    lf = -_softplus(-f)
    r_i = lax.broadcasted_iota(jnp.int32, (ts, ts), 0)
    c_i = lax.broadcasted_iota(jnp.int32, (ts, ts), 1)
    ltri = jnp.where(c_i <= r_i, 1.0, 0.0).astype(BF16)
    hi, mid, lo = _split3(lf)
    fcum = (_dot(ltri, hi) + _dot(ltri, mid)) + _dot(ltri, lo) + carry_scr[...]
    carry_scr[...] = fcum[ts - 1:ts, :]
    f_ref[...] = fcum * LOG2E


def _fox_proj(x, g, w_in, b_f, q_gain, k_gain, *, ts=256):
    bsz, s, d = x.shape
    hd = d // FOX_HEADS
    w_main = w_in[:, :4 * d].astype(BF16)
    w_f = jnp.pad(w_in[:, 4 * d:], ((0, 0), (0, LANES - FOX_HEADS))).astype(BF16)
    bf_row = jnp.pad(b_f.astype(F32), (0, LANES - FOX_HEADS)).reshape(1, LANES)
    scale = hd ** -0.5
    qg_row = (jnp.tile(q_gain.astype(F32), FOX_HEADS) * (scale * LOG2E)).reshape(1, d)
    kg_row = jnp.tile(k_gain.astype(F32), FOX_HEADS).reshape(1, d)
    head_id = jnp.arange(d) // hd
    hm = jnp.where(head_id[:, None] == head_id[None, :], 1.0 / hd, 0.0).astype(BF16)
    const2 = lambda shape: pl.BlockSpec(shape, lambda b, t: (0, 0))
    tok = pl.BlockSpec((None, ts, d), lambda b, t: (b, t, 0))
    act = jax.ShapeDtypeStruct((bsz, s, d), BF16)
    return pl.pallas_call(
        _fox_proj_kernel,
        grid=(bsz, s // ts),
        in_specs=[tok, const2((1, d)), const2((d, 4 * d)), const2((d, LANES)), const2((1, LANES)),
                  const2((1, d)), const2((1, d)), const2((d, d))],
        out_specs=[tok, tok, tok, tok, pl.BlockSpec((None, ts, LANES), lambda b, t: (b, t, 0))],
        out_shape=[act, act, act, act, jax.ShapeDtypeStruct((bsz, s, LANES), F32)],
        scratch_shapes=[pltpu.VMEM((1, LANES), F32)],
        compiler_params=pltpu.CompilerParams(
            dimension_semantics=("arbitrary", "arbitrary"), vmem_limit_bytes=48 * 1024 * 1024),
        name="fox_proj",
    )(x, g.reshape(1, d).astype(F32), w_main, w_f, bf_row, qg_row, kg_row, hm)


def _aug_lanes(hd):
    base = SUBLANES * hd
    return base, base + 3


def _attn_placement(pairs):
    pq = jnp.zeros((pairs, 3, LANES, LANES), F32)
    pk = jnp.zeros((pairs, 2, 3, LANES, LANES), F32)
    for p in range(pairs):
        for hd in range(2):
            q_lo, k_lo = _aug_lanes(hd)
            for i in range(3):
                pq = pq.at[p, i, 2 * p + hd, q_lo + i].set(1.0)
                pk = pk.at[p, hd, i, 2 * p + hd, k_lo + i].set(1.0)
    return pq.astype(BF16), pk.astype(BF16)


def _place(x, p_ref, idx):
    hi, mid, lo = _split3(x)
    return (_dot(hi, p_ref[idx + (0,)]) + _dot(mid, p_ref[idx + (1,)])) + _dot(lo, p_ref[idx + (2,)])


def _fox_attn_fast_kernel(c_ref, q_ref, k_ref, v_ref, f_ref, pq_ref, pk_ref, o_ref,
                          ka_scr, kb_scr, va_scr, vb_scr, acc_scr, *, chunk):
    qi = pl.program_id(2)
    tq = q_ref.shape[0]
    tk = tq
    s_len = k_ref.shape[0]
    half = LANES // 2
    k_scrs = (ka_scr, kb_scr)
    v_scrs = (va_scr, vb_scr)
    sum_lane = (half, half - 1)

    @pl.when(qi == 0)
    def _():
        def build(c, carry):
            rows = pl.ds(pl.multiple_of(c * chunk, chunk), chunk)
            kk = k_ref[rows, :]
            vv = v_ref[rows, :]
            lane = lax.broadcasted_iota(jnp.int32, (chunk, LANES), 1)
            nf = -f_ref[rows, :]
            for hd in range(2):
                ones_lo, _ = _aug_lanes(hd)
                aug = _place(nf, pk_ref, (hd,))
                aug = aug + jnp.where((lane >= ones_lo) & (lane < ones_lo + 3), 1.0, 0.0)
                keep = (lane < half) if hd == 0 else (lane >= half)
                k_scrs[hd][rows, 0:LANES] = jnp.where(keep, kk, jnp.zeros_like(kk))
                k_scrs[hd][rows, LANES:2 * LANES] = aug.astype(BF16)
                v_scrs[hd][rows, :] = jnp.where(lane == sum_lane[hd], jnp.ones_like(vv), vv)
            return carry
        lax.fori_loop(0, s_len // chunk, build, 0)

    lane_q = lax.broadcasted_iota(jnp.int32, (tq, LANES), 1)
    fq = f_ref[pl.ds(pl.multiple_of(qi * tq, tq), tq), :] - c_ref[...]
    augq = _place(fq, pq_ref, ())
    q_ones = ((lane_q >= 3) & (lane_q < 6)) | ((lane_q >= SUBLANES + 3) & (lane_q < SUBLANES + 6))
    augq = augq + jnp.where(q_ones, 1.0, 0.0)
    q_aug = jnp.concatenate([q_ref[...], augq.astype(BF16)], axis=1)
    acc_scr[...] = jnp.zeros_like(acc_scr)

    def attend(j, masked):
        rows = pl.ds(pl.multiple_of(j * tk, tk), tk)
        for hd in range(2):
            s = lax.dot_general(q_aug, k_scrs[hd][rows, :], (((1,), (1,)), ((), ())),
                                preferred_element_type=F32)
            p = jnp.exp2(s)
            if masked:
                r_i = lax.broadcasted_iota(jnp.int32, (tq, tk), 0)
                c_i = lax.broadcasted_iota(jnp.int32, (tq, tk), 1)
                p = jnp.where(c_i <= r_i, p, 0.0)
            acc_scr[hd] += _dot(p.astype(BF16), v_scrs[hd][rows, :])

    def body(j, carry):
        attend(j, False)
        return carry

    lax.fori_loop(0, qi, body, 0)
    attend(qi, True)

    acc_a = acc_scr[0]
    acc_b = acc_scr[1]
    oa = acc_a / acc_a[:, sum_lane[0]:sum_lane[0] + 1]
    ob = acc_b / acc_b[:, sum_lane[1]:sum_lane[1] + 1]
    o_ref[...] = jnp.where(lane_q < half, oa, ob).astype(o_ref.dtype)


def _fox_attn_fast(q, k, v, f_tok, shift, *, tq=512, chunk=1024):
    bsz, s, d = q.shape
    pairs = d // LANES
    pq, pk = _attn_placement(pairs)
    c_row = jnp.full((1, LANES), shift * LOG2E, F32)
    return pl.pallas_call(
        functools.partial(_fox_attn_fast_kernel, chunk=chunk),
        grid=(bsz, pairs, s // tq),
        in_specs=[
            pl.BlockSpec((1, LANES), lambda b, p, i: (0, 0)),
            pl.BlockSpec((None, tq, LANES), lambda b, p, i: (b, i, p)),
            pl.BlockSpec((None, s, LANES), lambda b, p, i: (b, 0, p)),
            pl.BlockSpec((None, s, LANES), lambda b, p, i: (b, 0, p)),
            pl.BlockSpec((None, s, LANES), lambda b, p, i: (b, 0, 0)),
            pl.BlockSpec((None, 3, LANES, LANES), lambda b, p, i: (p, 0, 0, 0)),
            pl.BlockSpec((None, 2, 3, LANES, LANES), lambda b, p, i: (p, 0, 0, 0, 0)),
        ],
        out_specs=pl.BlockSpec((None, tq, LANES), lambda b, p, i: (b, i, p)),
        out_shape=jax.ShapeDtypeStruct((bsz, s, d), BF16),
        scratch_shapes=[
            pltpu.VMEM((s, 2 * LANES), BF16),
            pltpu.VMEM((s, 2 * LANES), BF16),
            pltpu.VMEM((s, LANES), BF16),
            pltpu.VMEM((s, LANES), BF16),
            pltpu.VMEM((2, tq, LANES), F32),
        ],
        compiler_params=pltpu.CompilerParams(
            dimension_semantics=("arbitrary", "arbitrary", "arbitrary"),
            vmem_limit_bytes=48 * 1024 * 1024),
        name="fox_attn_fast",
    )(c_row, q, k, v, f_tok, pq, pk)


def _fox_attn_online_kernel(q_ref, k_ref, v_ref, ft_ref, o_ref, ka_scr, kb_scr, m_scr, l_scr, acc_scr):
    qi = pl.program_id(2)
    tq = q_ref.shape[0]
    tk = tq
    half = LANES // 2
    lane = lax.broadcasted_iota(jnp.int32, (1, LANES), 1)

    @pl.when(qi == 0)
    def _():
        kk = k_ref[...]
        lane_k = lax.broadcasted_iota(jnp.int32, kk.shape, 1)
        zero = jnp.zeros_like(kk)
        ka_scr[...] = jnp.where(lane_k < half, kk, zero)
        kb_scr[...] = jnp.where(lane_k >= half, kk, zero)

    m_scr[...] = jnp.full_like(m_scr, NEG_BIG)
    l_scr[...] = jnp.zeros_like(l_scr)
    acc_scr[...] = jnp.zeros_like(acc_scr)
    q = q_ref[...]

    def attend(j, masked):
        start = pl.multiple_of(j * tk, tk)
        v = v_ref[pl.ds(start, tk), :]
        for hd, k_scr in ((0, ka_scr), (1, kb_scr)):
            kb = k_scr[pl.ds(start, tk), :]
            s = lax.dot_general(q, kb, (((1,), (1,)), ((), ())), preferred_element_type=F32)
            s = s - ft_ref[hd, pl.ds(j, 1), :]
            if masked:
                r_i = lax.broadcasted_iota(jnp.int32, (tq, tk), 0)
                c_i = lax.broadcasted_iota(jnp.int32, (tq, tk), 1)
                s = jnp.where(c_i <= r_i, s, NEG_BIG)
            m_prev = m_scr[hd]
            m_new = jnp.maximum(m_prev, jnp.max(s, axis=1, keepdims=True))
            p = jnp.exp2(s - m_new[:, :1])
            alpha = jnp.exp2(m_prev - m_new)
            l_scr[hd] = alpha * l_scr[hd] + jnp.sum(p, axis=1, keepdims=True)
            acc_scr[hd] = alpha * acc_scr[hd] + _dot(p.astype(BF16), v)
            m_scr[hd] = m_new

    def body(j, carry):
        attend(j, False)
        return carry

    lax.fori_loop(0, qi, body, 0)
    attend(qi, True)

    oa = acc_scr[0] / l_scr[0]
    ob = acc_scr[1] / l_scr[1]
    o_ref[...] = jnp.where(lane < half, oa, ob).astype(o_ref.dtype)


def _fox_attn_online(q, k, v, f_tok, *, tq=512):
    bsz, s, d = q.shape
    pairs = d // LANES
    nq = s // tq
    ft5 = jnp.swapaxes(f_tok[:, :, :FOX_HEADS], 1, 2).reshape(bsz, pairs, 2, nq, tq)
    return pl.pallas_call(
        _fox_attn_online_kernel,
        grid=(bsz, pairs, nq),
        in_specs=[
            pl.BlockSpec((None, tq, LANES), lambda b, p, i: (b, i, p)),
            pl.BlockSpec((None, s, LANES), lambda b, p, i: (b, 0, p)),
            pl.BlockSpec((None, s, LANES), lambda b, p, i: (b, 0, p)),
            pl.BlockSpec((None, None, 2, nq, tq), lambda b, p, i: (b, p, 0, 0, 0)),
        ],
        out_specs=pl.BlockSpec((None, tq, LANES), lambda b, p, i: (b, i, p)),
        out_shape=jax.ShapeDtypeStruct((bsz, s, d), BF16),
        scratch_shapes=[
            pltpu.VMEM((s, LANES), BF16),
            pltpu.VMEM((s, LANES), BF16),
            pltpu.VMEM((2, tq, LANES), F32),
            pltpu.VMEM((2, tq, LANES), F32),
            pltpu.VMEM((2, tq, LANES), F32),
        ],
        compiler_params=pltpu.CompilerParams(
            dimension_semantics=("arbitrary", "arbitrary", "arbitrary"),
            vmem_limit_bytes=48 * 1024 * 1024),
        name="fox_attn_online",
    )(q, k, v, ft5)


def _fox_attn(q, k, v, f_tok, q_gain, k_gain):
    hd = q.shape[-1] // FOX_HEADS
    shift = (hd ** 0.5) * jnp.max(jnp.abs(q_gain)).astype(F32) * jnp.max(jnp.abs(k_gain)).astype(F32)
    return lax.cond(shift < ATTN_SHIFT_MAX,
                    lambda: _fox_attn_fast(q, k, v, f_tok, shift),
                    lambda: _fox_attn_online(q, k, v, f_tok))


def _fox_out_kernel(x_ref, o_ref, gate_ref, w_ref, out_ref):
    og = (o_ref[...].astype(F32) * gate_ref[...].astype(F32)).astype(BF16)
    out_ref[...] = x_ref[...] + _dot(og, w_ref[...])


def _fox_out(x2, o2, gate2, w_out, *, tm=512):
    t, d = x2.shape
    tok = pl.BlockSpec((tm, d), lambda i: (i, 0))
    return pl.pallas_call(
        _fox_out_kernel,
        grid=(t // tm,),
        in_specs=[tok, tok, tok, pl.BlockSpec((d, d), lambda i: (0, 0))],
        out_specs=tok,
        out_shape=jax.ShapeDtypeStruct((t, d), F32),
        compiler_params=pltpu.CompilerParams(dimension_semantics=("arbitrary",)),
        name="fox_out",
    )(x2, o2, gate2, w_out.astype(BF16))


def _ffn_kernel(*refs, routed):
    if routed:
        (x_ref, g_ref, rhi_ref, rlo_ref, wg_ref, wu_ref, wd_ref, o_ref, hn_scr, acc_scr, comb_scr) = refs
    else:
        (x_ref, g_ref, wg_ref, wu_ref, wd_ref, o_ref, hn_scr, acc_scr) = refs
    e = pl.program_id(1)
    f = pl.program_id(2)
    last = (e == pl.num_programs(1) - 1) & (f == pl.num_programs(2) - 1)

    @pl.when((e == 0) & (f == 0))
    def _():
        x = x_ref[...]
        hn = _rms_rows(x, g_ref[...])
        hn_scr[...] = hn.astype(BF16)
        acc_scr[...] = x
        if routed:
            tm = x.shape[0]
            h_hi = hn.astype(BF16)
            h_lo = (hn - h_hi.astype(F32)).astype(BF16)
            logits = (_dot(h_hi, rhi_ref[...]) + _dot(h_lo, rhi_ref[...])) + _dot(h_hi, rlo_ref[...])
            lane = lax.broadcasted_iota(jnp.int32, (tm, LANES), 1)
            logits = jnp.where(lane < N_EXPERTS, logits, NEG_BIG)
            m1 = jnp.max(logits, axis=1, keepdims=True)
            i1 = jnp.min(jnp.where(logits == m1, lane, LANES), axis=1, keepdims=True)
            rest = jnp.where(lane == i1, NEG_BIG, logits)
            m2 = jnp.max(rest, axis=1, keepdims=True)
            i2 = jnp.min(jnp.where(rest == m2, lane, LANES), axis=1, keepdims=True)
            e2 = jnp.exp(m2 - m1)
            w1 = 1.0 / (1.0 + e2)
            w2 = e2 / (1.0 + e2)
            zeros = jnp.zeros((tm, LANES), F32)
            for ex in range(N_EXPERTS):
                comb_scr[ex] = jnp.where(i1 == ex, w1, zeros) + jnp.where(i2 == ex, w2, zeros)

    hn = hn_scr[...]
    gg = _dot(hn, wg_ref[...])
    uu = _dot(hn, wu_ref[...])
    act = (gg * _sigmoid(gg)) * uu
    if routed:
        tf = act.shape[1]
        act = act * jnp.tile(comb_scr[e], (1, tf // LANES))
    acc_scr[...] += _dot(act.astype(BF16), wd_ref[...])

    @pl.when(last)
    def _():
        o_ref[...] = acc_scr[...]


def _ffn_layer(x2, g, w_gate, w_up, w_down, router=None, *, tm=512, tf=896):
    t, d = x2.shape
    n_e, _, dff = w_gate.shape
    routed = router is not None
    tok = pl.BlockSpec((tm, d), lambda i, e, f: (i, 0))
    const = lambda shape: pl.BlockSpec(shape, lambda i, e, f: (0, 0))
    in_specs = [tok, const((1, d))]
    args = [x2, g.reshape(1, d).astype(F32)]
    scratch = [pltpu.VMEM((tm, d), BF16), pltpu.VMEM((tm, d), F32)]
    if routed:
        r_pad = jnp.pad(router.astype(F32), ((0, 0), (0, LANES - n_e)))
        r_hi = r_pad.astype(BF16)
        r_lo = (r_pad - r_hi.astype(F32)).astype(BF16)
        in_specs += [const((d, LANES)), const((d, LANES))]
        args += [r_hi, r_lo]
        scratch.append(pltpu.VMEM((n_e, tm, LANES), F32))
    in_specs += [
        pl.BlockSpec((None, d, tf), lambda i, e, f: (e, 0, f)),
        pl.BlockSpec((None, d, tf), lambda i, e, f: (e, 0, f)),
        pl.BlockSpec((None, tf, d), lambda i, e, f: (e, f, 0)),
    ]
    args += [w_gate.astype(BF16), w_up.astype(BF16), w_down.astype(BF16)]
    return pl.pallas_call(
        functools.partial(_ffn_kernel, routed=routed),
        grid=(t // tm, n_e, dff // tf),
        in_specs=in_specs,
        out_specs=tok,
        out_shape=jax.ShapeDtypeStruct((t, d), F32),
        scratch_shapes=scratch,
        compiler_params=pltpu.CompilerParams(
            dimension_semantics=("arbitrary", "arbitrary", "arbitrary"),
            vmem_limit_bytes=48 * 1024 * 1024),
        name="moe_ffn" if routed else "dense_ffn",
    )(*args)


def kernel(x, norm_mix, norm_ffn, lru_w_in, lru_conv_w, lru_conv_b, lru_w_a, lru_b_a, lru_w_x, lru_b_x,
           lru_log_lambda, lru_w_out, fox_w_in, fox_b_f, fox_q_gain, fox_k_gain, fox_w_out, ffn_w_gate,
           ffn_w_up, ffn_w_down, moe_router, moe_w_gate, moe_w_up, moe_w_down):
    bsz, s, d = x.shape
    depth = norm_mix.shape[0]
    for i in range(depth):
        j = i // 2
        if i % 2 == 0:
            x = _lru_layer(x, norm_mix[i], lru_w_in[j], lru_conv_w[j], lru_conv_b[j], lru_w_a[j],
                           lru_b_a[j], lru_w_x[j], lru_b_x[j], lru_log_lambda[j], lru_w_out[j])
            x2 = _ffn_layer(x.reshape(bsz * s, d), norm_ffn[i], ffn_w_gate[j][None], ffn_w_up[j][None],
                            ffn_w_down[j][None])
        else:
            q, k, v, gate, ft = _fox_proj(x, norm_mix[i], fox_w_in[j], fox_b_f[j], fox_q_gain[j],
                                          fox_k_gain[j])
            o = _fox_attn(q, k, v, ft, fox_q_gain[j], fox_k_gain[j])
            x2 = _fox_out(x.reshape(bsz * s, d), o.reshape(bsz * s, d), gate.reshape(bsz * s, d),
                          fox_w_out[j])
            x2 = _ffn_layer(x2, norm_ffn[i], moe_w_gate[j], moe_w_up[j], moe_w_down[j], moe_router[j])
        x = x2.reshape(bsz, s, d)
    return x
```

```python
import functools

import jax
import jax.numpy as jnp
from jax import lax
from jax.experimental import pallas as pl
from jax.experimental.pallas import tpu as pltpu

RMS_EPS = 1e-6
LRU_C = 8.0
LRU_BLOCKS = 8
CONV_WIDTH = 4
FOX_HEADS = 16
N_EXPERTS = 8
LANES = 128
SUBLANES = 8
NEG_BIG = -1e30
LOG2E = 1.4426950408889634
ATTN_SHIFT_MAX = 40.0

F32 = jnp.float32
BF16 = jnp.bfloat16


def _dot(a, b):
    return jnp.dot(a, b, preferred_element_type=F32)


def _rms_rows(x, g):
    ms = jnp.mean(x * x, axis=-1, keepdims=True)
    return x * lax.rsqrt(ms + RMS_EPS) * g


def _sigmoid(x):
    return 1.0 / (1.0 + jnp.exp(-x))


def _softplus(x):
    return jnp.maximum(x, 0.0) + jnp.log1p(jnp.exp(-jnp.abs(x)))


def _gelu_tanh(x):
    return 0.5 * x * (1.0 + jnp.tanh(0.7978845608028654 * (x + 0.044715 * (x * x * x))))


def _split3(x):
    hi = x.astype(BF16)
    r1 = x - hi.astype(F32)
    mid = r1.astype(BF16)
    lo = (r1 - mid.astype(F32)).astype(BF16)
    return hi, mid, lo


def _lru_kernel(x_ref, g_ref, win_ref, cw_ref, cb_ref, wgate_ref, ba_ref, bx_ref, lam_ref, wout_ref,
                o_ref, xb_scr, hy_scr, carry_scr):
    t = pl.program_id(1)
    ts, d = x_ref.shape
    w = d

    @pl.when(t == 0)
    def _():
        xb_scr[0:SUBLANES, :] = jnp.zeros((SUBLANES, w), F32)
        carry_scr[...] = jnp.zeros_like(carry_scr)

    x = x_ref[...]
    hn = _rms_rows(x, g_ref[...]).astype(BF16)
    proj = _dot(hn, win_ref[...])
    xb = proj[:, w:]
    xb_scr[SUBLANES:SUBLANES + ts, :] = xb
    xc = cw_ref[3:4, :] * xb + cb_ref[...]
    for k in range(1, CONV_WIDTH):
        xc = xc + cw_ref[3 - k:4 - k, :] * xb_scr[SUBLANES - k:SUBLANES - k + ts, :]
    xb_scr[0:SUBLANES, :] = xb[ts - SUBLANES:, :]
    xcb = xc.astype(BF16)

    sp = _softplus(-lam_ref[...])
    rowmod = lax.broadcasted_iota(jnp.int32, (ts, LANES), 0) % SUBLANES
    for h in range(LRU_BLOCKS):
        sl = slice(h * LANES, (h + 1) * LANES)
        z = _dot(xcb[:, sl], wgate_ref[h])
        r = _sigmoid(z[:, :LANES] + ba_ref[:, sl])
        gi = _sigmoid(z[:, LANES:] + bx_ref[:, sl])
        log_a = (-LRU_C) * r * sp[:, sl]
        a = jnp.exp(log_a)
        b = jnp.sqrt(-jnp.tanh(log_a) * (a * a + 1.0)) * (gi * xc[:, sl])
        for sh in (1, 2, 4):
            keep = rowmod >= sh
            a_s = jnp.where(keep, pltpu.roll(a, sh, 0), 1.0)
            b_s = jnp.where(keep, pltpu.roll(b, sh, 0), 0.0)
            b = a * b_s + b
            a = a * a_s
        carry = carry_scr[h:h + 1, :]
        outs = []
        for gidx in range(ts // SUBLANES):
            rows = slice(gidx * SUBLANES, (gidx + 1) * SUBLANES)
            hg = a[rows, :] * carry + b[rows, :]
            outs.append(hg)
            carry = hg[SUBLANES - 1:SUBLANES, :]
        carry_scr[h:h + 1, :] = carry
        hseq = jnp.concatenate(outs, axis=0)
        yb = _gelu_tanh(proj[:, sl])
        hy_scr[:, sl] = (hseq * yb).astype(BF16)

    o_ref[...] = x + _dot(hy_scr[...], wout_ref[...])


def _lru_layer(x, g, w_in, conv_w, conv_b, w_a, b_a, w_x, b_x, log_lambda, w_out, *, ts=256):
    bsz, s, d = x.shape
    w = d
    wgate = jnp.concatenate([w_a, w_x], axis=-1).astype(BF16)
    row = lambda v: v.reshape(1, -1).astype(F32)
    const2 = lambda shape: pl.BlockSpec(shape, lambda b, t: (0, 0))
    return pl.pallas_call(
        _lru_kernel,
        grid=(bsz, s // ts),
        in_specs=[
            pl.BlockSpec((None, ts, d), lambda b, t: (b, t, 0)),
            const2((1, d)),
            const2((d, 2 * w)),
            const2((CONV_WIDTH, w)),
            const2((1, w)),
            pl.BlockSpec((LRU_BLOCKS, LANES, 2 * LANES), lambda b, t: (0, 0, 0)),
            const2((1, w)),
            const2((1, w)),
            const2((1, w)),
            const2((w, d)),
        ],
        out_specs=pl.BlockSpec((None, ts, d), lambda b, t: (b, t, 0)),
        out_shape=jax.ShapeDtypeStruct(x.shape, F32),
        scratch_shapes=[
            pltpu.VMEM((ts + SUBLANES, w), F32),
            pltpu.VMEM((ts, w), BF16),
            pltpu.VMEM((LRU_BLOCKS, LANES), F32),
        ],
        compiler_params=pltpu.CompilerParams(
            dimension_semantics=("arbitrary", "arbitrary"), vmem_limit_bytes=48 * 1024 * 1024),
        name="lru_mixer",
    )(x, row(g), w_in.astype(BF16), conv_w.astype(F32), row(conv_b), wgate, row(b_a), row(b_x),
      row(log_lambda), w_out.astype(BF16))


def _fox_proj_kernel(x_ref, g_ref, w_ref, wf_ref, bf_ref, qg_ref, kg_ref, hm_ref,
                     q_ref, k_ref, v_ref, gate_ref, f_ref, carry_scr):
    t = pl.program_id(1)
    ts, d = x_ref.shape

    @pl.when(t == 0)
    def _():
        carry_scr[...] = jnp.zeros_like(carry_scr)

    hn = _rms_rows(x_ref[...], g_ref[...]).astype(BF16)
    proj = _dot(hn, w_ref[...])
    hm = hm_ref[...]

    def head_norm(z, gain):
        ms = _dot((z * z).astype(BF16), hm)
        return z * lax.rsqrt(ms + RMS_EPS) * gain

    q_ref[...] = head_norm(proj[:, :d], qg_ref[...]).astype(BF16)
    k_ref[...] = head_norm(proj[:, d:2 * d], kg_ref[...]).astype(BF16)
    v_ref[...] = proj[:, 2 * d:3 * d].astype(BF16)
    gate_ref[...] = _sigmoid(proj[:, 3 * d:]).astype(BF16)

    f = _dot(hn, wf_ref[...]) + bf_ref[...]
    lf = -_softplus(-f)
    r_i = lax.broadcasted_iota(jnp.int32, (ts, ts), 0)
    c_i = lax.broadcasted_iota(jnp.int32, (ts, ts), 1)
    ltri = jnp.where(c_i <= r_i, 1.0, 0.0).astype(BF16)
    hi, mid, lo = _split3(lf)
    fcum = (_dot(ltri, hi) + _dot(ltri, mid)) + _dot(ltri, lo) + carry_scr[...]
    carry_scr[...] = fcum[ts - 1:ts, :]
    f_ref[...] = fcum * LOG2E


def _fox_proj(x, g, w_in, b_f, q_gain, k_gain, *, ts=256):
    bsz, s, d = x.shape
    hd = d // FOX_HEADS
    w_main = w_in[:, :4 * d].astype(BF16)
    w_f = jnp.pad(w_in[:, 4 * d:], ((0, 0), (0, LANES - FOX_HEADS))).astype(BF16)
    bf_row = jnp.pad(b_f.astype(F32), (0, LANES - FOX_HEADS)).reshape(1, LANES)
    scale = hd ** -0.5
    qg_row = (jnp.tile(q_gain.astype(F32), FOX_HEADS) * (scale * LOG2E)).reshape(1, d)
    kg_row = jnp.tile(k_gain.astype(F32), FOX_HEADS).reshape(1, d)
    head_id = jnp.arange(d) // hd
    hm = jnp.where(head_id[:, None] == head_id[None, :], 1.0 / hd, 0.0).astype(BF16)
    const2 = lambda shape: pl.BlockSpec(shape, lambda b, t: (0, 0))
    tok = pl.BlockSpec((None, ts, d), lambda b, t: (b, t, 0))
    act = jax.ShapeDtypeStruct((bsz, s, d), BF16)
    return pl.pallas_call(
        _fox_proj_kernel,
        grid=(bsz, s // ts),
        in_specs=[tok, const2((1, d)), const2((d, 4 * d)), const2((d, LANES)), const2((1, LANES)),
                  const2((1, d)), const2((1, d)), const2((d, d))],
        out_specs=[tok, tok, tok, tok, pl.BlockSpec((None, ts, LANES), lambda b, t: (b, t, 0))],
        out_shape=[act, act, act, act, jax.ShapeDtypeStruct((bsz, s, LANES), F32)],
        scratch_shapes=[pltpu.VMEM((1, LANES), F32)],
        compiler_params=pltpu.CompilerParams(
            dimension_semantics=("arbitrary", "arbitrary"), vmem_limit_bytes=48 * 1024 * 1024),
        name="fox_proj",
    )(x, g.reshape(1, d).astype(F32), w_main, w_f, bf_row, qg_row, kg_row, hm)


def _aug_lanes(hd):
    base = SUBLANES * hd
    return base, base + 3


def _attn_placement(pairs):
    pq = jnp.zeros((pairs, 3, LANES, LANES), F32)
    pk = jnp.zeros((pairs, 2, 3, LANES, LANES), F32)
    for p in range(pairs):
        for hd in range(2):
            q_lo, k_lo = _aug_lanes(hd)
            for i in range(3):
                pq = pq.at[p, i, 2 * p + hd, q_lo + i].set(1.0)
                pk = pk.at[p, hd, i, 2 * p + hd, k_lo + i].set(1.0)
    return pq.astype(BF16), pk.astype(BF16)


def _place(x, p_ref, idx):
    hi, mid, lo = _split3(x)
    return (_dot(hi, p_ref[idx + (0,)]) + _dot(mid, p_ref[idx + (1,)])) + _dot(lo, p_ref[idx + (2,)])


def _fox_attn_fast_kernel(c_ref, q_ref, k_ref, v_ref, f_ref, pq_ref, pk_ref, o_ref,
                          ka_scr, kb_scr, va_scr, vb_scr, acc_scr, *, chunk):
    qi = pl.program_id(2)
    tq = q_ref.shape[0]
    tk = tq
    s_len = k_ref.shape[0]
    half = LANES // 2
    k_scrs = (ka_scr, kb_scr)
    v_scrs = (va_scr, vb_scr)
    sum_lane = (half, half - 1)

    @pl.when(qi == 0)
    def _():
        def build(c, carry):
            rows = pl.ds(pl.multiple_of(c * chunk, chunk), chunk)
            kk = k_ref[rows, :]
            vv = v_ref[rows, :]
            lane = lax.broadcasted_iota(jnp.int32, (chunk, LANES), 1)
            nf = -f_ref[rows, :]
            for hd in range(2):
                ones_lo, _ = _aug_lanes(hd)
                aug = _place(nf, pk_ref, (hd,))
                aug = aug + jnp.where((lane >= ones_lo) & (lane < ones_lo + 3), 1.0, 0.0)
                keep = (lane < half) if hd == 0 else (lane >= half)
                k_scrs[hd][rows, 0:LANES] = jnp.where(keep, kk, jnp.zeros_like(kk))
                k_scrs[hd][rows, LANES:2 * LANES] = aug.astype(BF16)
                v_scrs[hd][rows, :] = jnp.where(lane == sum_lane[hd], jnp.ones_like(vv), vv)
            return carry
        lax.fori_loop(0, s_len // chunk, build, 0)

    lane_q = lax.broadcasted_iota(jnp.int32, (tq, LANES), 1)
    fq = f_ref[pl.ds(pl.multiple_of(qi * tq, tq), tq), :] - c_ref[...]
    augq = _place(fq, pq_ref, ())
    q_ones = ((lane_q >= 3) & (lane_q < 6)) | ((lane_q >= SUBLANES + 3) & (lane_q < SUBLANES + 6))
    augq = augq + jnp.where(q_ones, 1.0, 0.0)
    q_aug = jnp.concatenate([q_ref[...], augq.astype(BF16)], axis=1)
    acc_scr[...] = jnp.zeros_like(acc_scr)

    def attend(j, masked):
        rows = pl.ds(pl.multiple_of(j * tk, tk), tk)
        for hd in range(2):
            s = lax.dot_general(q_aug, k_scrs[hd][rows, :], (((1,), (1,)), ((), ())),
                                preferred_element_type=F32)
            p = jnp.exp2(s)
            if masked:
                r_i = lax.broadcasted_iota(jnp.int32, (tq, tk), 0)
                c_i = lax.broadcasted_iota(jnp.int32, (tq, tk), 1)
                p = jnp.where(c_i <= r_i, p, 0.0)
            acc_scr[hd] += _dot(p.astype(BF16), v_scrs[hd][rows, :])

    def body(j, carry):
        attend(j, False)
        return carry

    lax.fori_loop(0, qi, body, 0)
    attend(qi, True)

    acc_a = acc_scr[0]
    acc_b = acc_scr[1]
    oa = acc_a / acc_a[:, sum_lane[0]:sum_lane[0] + 1]
    ob = acc_b / acc_b[:, sum_lane[1]:sum_lane[1] + 1]
    o_ref[...] = jnp.where(lane_q < half, oa, ob).astype(o_ref.dtype)


def _fox_attn_fast(q, k, v, f_tok, shift, *, tq=512, chunk=1024):
    bsz, s, d = q.shape
    pairs = d // LANES
    pq, pk = _attn_placement(pairs)
    c_row = jnp.full((1, LANES), shift * LOG2E, F32)
    return pl.pallas_call(
        functools.partial(_fox_attn_fast_kernel, chunk=chunk),
        grid=(bsz, pairs, s // tq),
        in_specs=[
            pl.BlockSpec((1, LANES), lambda b, p, i: (0, 0)),
            pl.BlockSpec((None, tq, LANES), lambda b, p, i: (b, i, p)),
            pl.BlockSpec((None, s, LANES), lambda b, p, i: (b, 0, p)),
            pl.BlockSpec((None, s, LANES), lambda b, p, i: (b, 0, p)),
            pl.BlockSpec((None, s, LANES), lambda b, p, i: (b, 0, 0)),
            pl.BlockSpec((None, 3, LANES, LANES), lambda b, p, i: (p, 0, 0, 0)),
            pl.BlockSpec((None, 2, 3, LANES, LANES), lambda b, p, i: (p, 0, 0, 0, 0)),
        ],
        out_specs=pl.BlockSpec((None, tq, LANES), lambda b, p, i: (b, i, p)),
        out_shape=jax.ShapeDtypeStruct((bsz, s, d), BF16),
        scratch_shapes=[
            pltpu.VMEM((s, 2 * LANES), BF16),
            pltpu.VMEM((s, 2 * LANES), BF16),
            pltpu.VMEM((s, LANES), BF16),
            pltpu.VMEM((s, LANES), BF16),
            pltpu.VMEM((2, tq, LANES), F32),
        ],
        compiler_params=pltpu.CompilerParams(
            dimension_semantics=("arbitrary", "arbitrary", "arbitrary"),
            vmem_limit_bytes=48 * 1024 * 1024),
        name="fox_attn_fast",
    )(c_row, q, k, v, f_tok, pq, pk)


def _fox_attn_online_kernel(q_ref, k_ref, v_ref, ft_ref, o_ref, ka_scr, kb_scr, m_scr, l_scr, acc_scr):
    qi = pl.program_id(2)
    tq = q_ref.shape[0]
    tk = tq
    half = LANES // 2
    lane = lax.broadcasted_iota(jnp.int32, (1, LANES), 1)

    @pl.when(qi == 0)
    def _():
        kk = k_ref[...]
        lane_k = lax.broadcasted_iota(jnp.int32, kk.shape, 1)
        zero = jnp.zeros_like(kk)
        ka_scr[...] = jnp.where(lane_k < half, kk, zero)
        kb_scr[...] = jnp.where(lane_k >= half, kk, zero)

    m_scr[...] = jnp.full_like(m_scr, NEG_BIG)
    l_scr[...] = jnp.zeros_like(l_scr)
    acc_scr[...] = jnp.zeros_like(acc_scr)
    q = q_ref[...]

    def attend(j, masked):
        start = pl.multiple_of(j * tk, tk)
        v = v_ref[pl.ds(start, tk), :]
        for hd, k_scr in ((0, ka_scr), (1, kb_scr)):
            kb = k_scr[pl.ds(start, tk), :]
            s = lax.dot_general(q, kb, (((1,), (1,)), ((), ())), preferred_element_type=F32)
            s = s - ft_ref[hd, pl.ds(j, 1), :]
            if masked:
                r_i = lax.broadcasted_iota(jnp.int32, (tq, tk), 0)
                c_i = lax.broadcasted_iota(jnp.int32, (tq, tk), 1)
                s = jnp.where(c_i <= r_i, s, NEG_BIG)
            m_prev = m_scr[hd]
            m_new = jnp.maximum(m_prev, jnp.max(s, axis=1, keepdims=True))
            p = jnp.exp2(s - m_new[:, :1])
            alpha = jnp.exp2(m_prev - m_new)
            l_scr[hd] = alpha * l_scr[hd] + jnp.sum(p, axis=1, keepdims=True)
            acc_scr[hd] = alpha * acc_scr[hd] + _dot(p.astype(BF16), v)
            m_scr[hd] = m_new

    def body(j, carry):
        attend(j, False)
        return carry

    lax.fori_loop(0, qi, body, 0)
    attend(qi, True)

    oa = acc_scr[0] / l_scr[0]
    ob = acc_scr[1] / l_scr[1]
    o_ref[...] = jnp.where(lane < half, oa, ob).astype(o_ref.dtype)


def _fox_attn_online(q, k, v, f_tok, *, tq=512):
    bsz, s, d = q.shape
    pairs = d // LANES
    nq = s // tq
    ft5 = jnp.swapaxes(f_tok[:, :, :FOX_HEADS], 1, 2).reshape(bsz, pairs, 2, nq, tq)
    return pl.pallas_call(
        _fox_attn_online_kernel,
        grid=(bsz, pairs, nq),
        in_specs=[
            pl.BlockSpec((None, tq, LANES), lambda b, p, i: (b, i, p)),
            pl.BlockSpec((None, s, LANES), lambda b, p, i: (b, 0, p)),
            pl.BlockSpec((None, s, LANES), lambda b, p, i: (b, 0, p)),
            pl.BlockSpec((None, None, 2, nq, tq), lambda b, p, i: (b, p, 0, 0, 0)),
        ],
        out_specs=pl.BlockSpec((None, tq, LANES), lambda b, p, i: (b, i, p)),
        out_shape=jax.ShapeDtypeStruct((bsz, s, d), BF16),
        scratch_shapes=[
            pltpu.VMEM((s, LANES), BF16),
            pltpu.VMEM((s, LANES), BF16),
            pltpu.VMEM((2, tq, LANES), F32),
            pltpu.VMEM((2, tq, LANES), F32),
            pltpu.VMEM((2, tq, LANES), F32),
        ],
        compiler_params=pltpu.CompilerParams(
            dimension_semantics=("arbitrary", "arbitrary", "arbitrary"),
            vmem_limit_bytes=48 * 1024 * 1024),
        name="fox_attn_online",
    )(q, k, v, ft5)


def _fox_attn(q, k, v, f_tok, q_gain, k_gain):
    hd = q.shape[-1] // FOX_HEADS
    shift = (hd ** 0.5) * jnp.max(jnp.abs(q_gain)).astype(F32) * jnp.max(jnp.abs(k_gain)).astype(F32)
    return lax.cond(shift < ATTN_SHIFT_MAX,
                    lambda: _fox_attn_fast(q, k, v, f_tok, shift),
                    lambda: _fox_attn_online(q, k, v, f_tok))


def _fox_out_kernel(x_ref, o_ref, gate_ref, w_ref, out_ref):
    og = (o_ref[...].astype(F32) * gate_ref[...].astype(F32)).astype(BF16)
    out_ref[...] = x_ref[...] + _dot(og, w_ref[...])


def _fox_out(x2, o2, gate2, w_out, *, tm=512):
    t, d = x2.shape
    tok = pl.BlockSpec((tm, d), lambda i: (i, 0))
    return pl.pallas_call(
        _fox_out_kernel,
        grid=(t // tm,),
        in_specs=[tok, tok, tok, pl.BlockSpec((d, d), lambda i: (0, 0))],
        out_specs=tok,
        out_shape=jax.ShapeDtypeStruct((t, d), F32),
        compiler_params=pltpu.CompilerParams(dimension_semantics=("arbitrary",)),
        name="fox_out",
    )(x2, o2, gate2, w_out.astype(BF16))


def _ffn_kernel(x_ref, g_ref, wg_ref, wu_ref, wd_ref, o_ref, hn_scr, acc_scr):
    f = pl.program_id(1)

    @pl.when(f == 0)
    def _():
        x = x_ref[...]
        hn_scr[...] = _rms_rows(x, g_ref[...]).astype(BF16)
        acc_scr[...] = x

    hn = hn_scr[...]
    gg = _dot(hn, wg_ref[...])
    uu = _dot(hn, wu_ref[...])
    act = (gg * _sigmoid(gg)) * uu
    acc_scr[...] += _dot(act.astype(BF16), wd_ref[...])

    @pl.when(f == pl.num_programs(1) - 1)
    def _():
        o_ref[...] = acc_scr[...]


def _ffn_layer(x2, g, w_gate, w_up, w_down, *, tm=512, tf=896):
    t, d = x2.shape
    dff = w_gate.shape[1]
    tok = pl.BlockSpec((tm, d), lambda i, f: (i, 0))
    return pl.pallas_call(
        _ffn_kernel,
        grid=(t // tm, dff // tf),
        in_specs=[tok, pl.BlockSpec((1, d), lambda i, f: (0, 0)),
                  pl.BlockSpec((d, tf), lambda i, f: (0, f)),
                  pl.BlockSpec((d, tf), lambda i, f: (0, f)),
                  pl.BlockSpec((tf, d), lambda i, f: (f, 0))],
        out_specs=tok,
        out_shape=jax.ShapeDtypeStruct((t, d), F32),
        scratch_shapes=[pltpu.VMEM((tm, d), BF16), pltpu.VMEM((tm, d), F32)],
        compiler_params=pltpu.CompilerParams(
            dimension_semantics=("arbitrary", "arbitrary"), vmem_limit_bytes=48 * 1024 * 1024),
        name="dense_ffn",
    )(x2, g.reshape(1, d).astype(F32), w_gate.astype(BF16), w_up.astype(BF16), w_down.astype(BF16))


def _top2(logits, lane):
    m1 = jnp.max(logits, axis=1, keepdims=True)
    i1 = jnp.min(jnp.where(logits == m1, lane, LANES), axis=1, keepdims=True)
    rest = jnp.where(lane == i1, NEG_BIG, logits)
    m2 = jnp.max(rest, axis=1, keepdims=True)
    i2 = jnp.min(jnp.where(rest == m2, lane, LANES), axis=1, keepdims=True)
    return m1, i1, m2, i2


def _router_logits(hn, rhi_ref, rlo_ref):
    h_hi = hn.astype(BF16)
    h_lo = (hn - h_hi.astype(F32)).astype(BF16)
    logits = (_dot(h_hi, rhi_ref[...]) + _dot(h_lo, rhi_ref[...])) + _dot(h_hi, rlo_ref[...])
    lane = lax.broadcasted_iota(jnp.int32, logits.shape, 1)
    return jnp.where(lane < N_EXPERTS, logits, NEG_BIG), lane


def _moe_route_kernel(x_ref, g_ref, rhi_ref, rlo_ref, d_ref, w_ref, cnt_ref, carry_scr, *, cap):
    @pl.when(pl.program_id(0) == 0)
    def _():
        carry_scr[...] = jnp.zeros_like(carry_scr)

    tm = x_ref.shape[0]
    hn = _rms_rows(x_ref[...], g_ref[...])
    logits, lane = _router_logits(hn, rhi_ref, rlo_ref)
    m1, i1, m2, i2 = _top2(logits, lane)
    e2 = jnp.exp(m2 - m1)
    w1 = 1.0 / (1.0 + e2)
    w2 = e2 / (1.0 + e2)
    oh1 = lane == i1
    oh2 = lane == i2
    member = jnp.where(oh1 | oh2, 1.0, 0.0)
    r_i = lax.broadcasted_iota(jnp.int32, (tm, tm), 0)
    c_i = lax.broadcasted_iota(jnp.int32, (tm, tm), 1)
    lstrict = jnp.where(c_i < r_i, 1.0, 0.0).astype(BF16)
    before = _dot(lstrict, member.astype(BF16)) + carry_scr[...]
    rank1 = jnp.sum(jnp.where(oh1, before, 0.0), axis=1, keepdims=True)
    rank2 = jnp.sum(jnp.where(oh2, before, 0.0), axis=1, keepdims=True)
    dest1 = i1.astype(F32) * cap + rank1
    dest2 = i2.astype(F32) * cap + rank2
    carry_scr[...] += jnp.sum(member, axis=0, keepdims=True)
    cnt_ref[...] = carry_scr[...]
    w_ref[...] = jnp.where(lane == 0, w1, jnp.where(lane == 1, w2, 0.0))
    dd = jnp.where(lane == 0, dest1, jnp.where(lane == 1, dest2, 0.0))
    d_ref[...] = dd.T[:SUBLANES, :].astype(jnp.int32)


def _moe_dispatch_kernel(d_ref, x_ref, xs_ref, sem):
    tm = x_ref.shape[0]

    def body(t, carry):
        for slot in range(2):
            row = d_ref[slot, t]
            pltpu.make_async_copy(x_ref.at[pl.ds(t, 1), :], xs_ref.at[pl.ds(row, 1), :], sem).start()
        return carry

    lax.fori_loop(0, tm, body, 0, unroll=8)
    for slot in range(2):
        pltpu.make_async_copy(x_ref, xs_ref.at[pl.ds(0, tm), :], sem).wait()


def _moe_group_kernel(blk_ref, exp_ref, rows_ref, xs_ref, g_ref, wg_ref, wu_ref, wd_ref, ys_ref,
                      hn_scr, acc_scr):
    gi = pl.program_id(0)
    f = pl.program_id(1)
    nrows = rows_ref[gi]

    @pl.when(nrows > 0)
    def _():
        @pl.when(f == 0)
        def _():
            x = xs_ref[...]
            row = lax.broadcasted_iota(jnp.int32, x.shape, 0)
            x = jnp.where(row < nrows, x, 0.0)
            hn_scr[...] = _rms_rows(x, g_ref[...]).astype(BF16)
            acc_scr[...] = jnp.zeros_like(acc_scr)

        hn = hn_scr[...]
        gg = _dot(hn, wg_ref[...])
        uu = _dot(hn, wu_ref[...])
        act = (gg * _sigmoid(gg)) * uu
        acc_scr[...] += _dot(act.astype(BF16), wd_ref[...])

        @pl.when(f == pl.num_programs(1) - 1)
        def _():
            ys_ref[...] = acc_scr[...]


def _moe_combine_kernel(d_ref, x_ref, w_ref, ys_ref, o_ref, buf, sem):
    tm = x_ref.shape[0]

    def body(t, carry):
        for slot in range(2):
            row = d_ref[slot, t]
            pltpu.make_async_copy(ys_ref.at[pl.ds(row, 1), :], buf.at[slot, pl.ds(t, 1), :], sem).start()
        return carry

    lax.fori_loop(0, tm, body, 0, unroll=8)
    for slot in range(2):
        pltpu.make_async_copy(ys_ref.at[pl.ds(0, tm), :], buf.at[slot], sem).wait()
    w = w_ref[...]
    o_ref[...] = x_ref[...] + w[:, 0:1] * buf[0] + w[:, 1:2] * buf[1]


def _moe_layer(x2, g, w_gate, w_up, w_down, router, *, tm=512, tf=896):
    t, d = x2.shape
    n_e, _, dff = w_gate.shape
    nt = t // tm
    cap_blocks = nt + 1
    cap = cap_blocks * tm
    n_tiles = (2 * t) // tm + n_e
    g_row = g.reshape(1, d).astype(F32)
    r_pad = jnp.pad(router.astype(F32), ((0, 0), (0, LANES - n_e)))
    r_hi = r_pad.astype(BF16)
    r_lo = (r_pad - r_hi.astype(F32)).astype(BF16)

    dest, wts, counts = pl.pallas_call(
        functools.partial(_moe_route_kernel, cap=float(cap)),
        grid=(nt,),
        in_specs=[pl.BlockSpec((tm, d), lambda i: (i, 0)), pl.BlockSpec((1, d), lambda i: (0, 0)),
                  pl.BlockSpec((d, LANES), lambda i: (0, 0)), pl.BlockSpec((d, LANES), lambda i: (0, 0))],
        out_specs=[pl.BlockSpec((None, SUBLANES, tm), lambda i: (i, 0, 0)),
                   pl.BlockSpec((tm, LANES), lambda i: (i, 0)),
                   pl.BlockSpec((1, LANES), lambda i: (0, 0))],
        out_shape=[jax.ShapeDtypeStruct((nt, SUBLANES, tm), jnp.int32),
                   jax.ShapeDtypeStruct((t, LANES), F32),
                   jax.ShapeDtypeStruct((1, LANES), F32)],
        scratch_shapes=[pltpu.VMEM((1, LANES), F32)],
        compiler_params=pltpu.CompilerParams(dimension_semantics=("arbitrary",)),
        name="moe_route",
    )(x2, g_row, r_hi, r_lo)

    cnt = counts[0, :n_e].astype(jnp.int32)
    tiles_e = (cnt + tm - 1) // tm
    ends = jnp.cumsum(tiles_e)
    tile_ids = jnp.arange(n_tiles, dtype=jnp.int32)
    live = tile_ids < ends[-1]
    clamped = jnp.minimum(tile_ids, ends[-1] - 1)
    t_exp = jnp.minimum(jnp.searchsorted(ends, clamped, side="right"), n_e - 1).astype(jnp.int32)
    local = clamped - (ends[t_exp] - tiles_e[t_exp])
    t_blk = (t_exp * cap_blocks + local).astype(jnp.int32)
    t_rows = jnp.where(live, jnp.clip(cnt[t_exp] - local * tm, 0, tm), 0).astype(jnp.int32)

    smem_idx = pl.BlockSpec((None, SUBLANES, tm), lambda i: (i, 0, 0), memory_space=pltpu.SMEM)
    xs = pl.pallas_call(
        _moe_dispatch_kernel,
        grid=(nt,),
        in_specs=[smem_idx, pl.BlockSpec((tm, d), lambda i: (i, 0))],
        out_specs=pl.BlockSpec(memory_space=pl.ANY),
        out_shape=jax.ShapeDtypeStruct((n_e * cap, d), F32),
        scratch_shapes=[pltpu.SemaphoreType.DMA(())],
        compiler_params=pltpu.CompilerParams(dimension_semantics=("arbitrary",)),
        name="moe_dispatch",
    )(dest, x2)

    ys = pl.pallas_call(
        _moe_group_kernel,
        grid_spec=pltpu.PrefetchScalarGridSpec(
            num_scalar_prefetch=3,
            grid=(n_tiles, dff // tf),
            in_specs=[
                pl.BlockSpec((tm, d), lambda i, f, blk, ex, rows: (blk[i], 0)),
                pl.BlockSpec((1, d), lambda i, f, blk, ex, rows: (0, 0)),
                pl.BlockSpec((None, d, tf), lambda i, f, blk, ex, rows: (ex[i], 0, f)),
                pl.BlockSpec((None, d, tf), lambda i, f, blk, ex, rows: (ex[i], 0, f)),
                pl.BlockSpec((None, tf, d), lambda i, f, blk, ex, rows: (ex[i], f, 0)),
            ],
            out_specs=pl.BlockSpec((tm, d), lambda i, f, blk, ex, rows: (blk[i], 0)),
            scratch_shapes=[pltpu.VMEM((tm, d), BF16), pltpu.VMEM((tm, d), F32)],
        ),
        out_shape=jax.ShapeDtypeStruct((n_e * cap, d), F32),
        compiler_params=pltpu.CompilerParams(
            dimension_semantics=("arbitrary", "arbitrary"), vmem_limit_bytes=48 * 1024 * 1024),
        name="moe_group",
    )(t_blk, t_exp, t_rows, xs, g_row, w_gate.astype(BF16), w_up.astype(BF16), w_down.astype(BF16))

    return pl.pallas_call(
        _moe_combine_kernel,
        grid=(nt,),
        in_specs=[smem_idx, pl.BlockSpec((tm, d), lambda i: (i, 0)),
                  pl.BlockSpec((tm, LANES), lambda i: (i, 0)), pl.BlockSpec(memory_space=pl.ANY)],
        out_specs=pl.BlockSpec((tm, d), lambda i: (i, 0)),
        out_shape=jax.ShapeDtypeStruct((t, d), F32),
        scratch_shapes=[pltpu.VMEM((2, tm, d), F32), pltpu.SemaphoreType.DMA(())],
        compiler_params=pltpu.CompilerParams(dimension_semantics=("arbitrary",)),
        name="moe_combine",
    )(dest, x2, wts, ys)


def kernel(x, norm_mix, norm_ffn, lru_w_in, lru_conv_w, lru_conv_b, lru_w_a, lru_b_a, lru_w_x, lru_b_x,
           lru_log_lambda, lru_w_out, fox_w_in, fox_b_f, fox_q_gain, fox_k_gain, fox_w_out, ffn_w_gate,
           ffn_w_up, ffn_w_down, moe_router, moe_w_gate, moe_w_up, moe_w_down):
    bsz, s, d = x.shape
    depth = norm_mix.shape[0]
    for i in range(depth):
        j = i // 2
        if i % 2 == 0:
            x = _lru_layer(x, norm_mix[i], lru_w_in[j], lru_conv_w[j], lru_conv_b[j], lru_w_a[j],
                           lru_b_a[j], lru_w_x[j], lru_b_x[j], lru_log_lambda[j], lru_w_out[j])
            x2 = _ffn_layer(x.reshape(bsz * s, d), norm_ffn[i], ffn_w_gate[j], ffn_w_up[j], ffn_w_down[j])
        else:
            q, k, v, gate, ft = _fox_proj(x, norm_mix[i], fox_w_in[j], fox_b_f[j], fox_q_gain[j],
                                          fox_k_gain[j])
            o = _fox_attn(q, k, v, ft, fox_q_gain[j], fox_k_gain[j])
            x2 = _fox_out(x.reshape(bsz * s, d), o.reshape(bsz * s, d), gate.reshape(bsz * s, d),
                          fox_w_out[j])
            x2 = _moe_layer(x2, norm_ffn[i], moe_w_gate[j], moe_w_up[j], moe_w_down[j], moe_router[j])
        x = x2.reshape(bsz, s, d)
    return x
```

```python
import functools

import jax
import jax.numpy as jnp
from jax import lax
from jax.experimental import pallas as pl
from jax.experimental.pallas import tpu as pltpu

RMS_EPS = 1e-6
LRU_C = 8.0
LRU_BLOCKS = 8
CONV_WIDTH = 4
FOX_HEADS = 16
N_EXPERTS = 8
LANES = 128
SUBLANES = 8
NEG_BIG = -1e30
LOG2E = 1.4426950408889634
ATTN_SHIFT_MAX = 40.0
ATTN_DEAD_LOG2 = 160.0

F32 = jnp.float32
BF16 = jnp.bfloat16


def _dot(a, b):
    return jnp.dot(a, b, preferred_element_type=F32)


def _rms_rows(x, g):
    ms = jnp.mean(x * x, axis=-1, keepdims=True)
    return x * lax.rsqrt(ms + RMS_EPS) * g


def _sigmoid(x):
    return 1.0 / (1.0 + jnp.exp(-x))


def _softplus(x):
    return jnp.maximum(x, 0.0) + jnp.log1p(jnp.exp(-jnp.abs(x)))


def _gelu_tanh(x):
    return 0.5 * x * (1.0 + jnp.tanh(0.7978845608028654 * (x + 0.044715 * (x * x * x))))


def _split3(x):
    hi = x.astype(BF16)
    r1 = x - hi.astype(F32)
    mid = r1.astype(BF16)
    lo = (r1 - mid.astype(F32)).astype(BF16)
    return hi, mid, lo


def _lru_kernel(x_ref, g_ref, win_ref, cw_ref, cb_ref, wgate_ref, ba_ref, bx_ref, lam_ref, wout_ref,
                o_ref, xb_scr, hy_scr, carry_scr):
    t = pl.program_id(1)
    ts, d = x_ref.shape
    w = d

    @pl.when(t == 0)
    def _():
        xb_scr[0:SUBLANES, :] = jnp.zeros((SUBLANES, w), F32)
        carry_scr[...] = jnp.zeros_like(carry_scr)

    x = x_ref[...]
    hn = _rms_rows(x, g_ref[...]).astype(BF16)
    proj = _dot(hn, win_ref[...])
    xb = proj[:, w:]
    xb_scr[SUBLANES:SUBLANES + ts, :] = xb
    xc = cw_ref[3:4, :] * xb + cb_ref[...]
    for k in range(1, CONV_WIDTH):
        xc = xc + cw_ref[3 - k:4 - k, :] * xb_scr[SUBLANES - k:SUBLANES - k + ts, :]
    xb_scr[0:SUBLANES, :] = xb[ts - SUBLANES:, :]
    xcb = xc.astype(BF16)

    sp = _softplus(-lam_ref[...])
    rowmod = lax.broadcasted_iota(jnp.int32, (ts, LANES), 0) % SUBLANES
    for h in range(LRU_BLOCKS):
        sl = slice(h * LANES, (h + 1) * LANES)
        z = _dot(xcb[:, sl], wgate_ref[h])
        r = _sigmoid(z[:, :LANES] + ba_ref[:, sl])
        gi = _sigmoid(z[:, LANES:] + bx_ref[:, sl])
        log_a = (-LRU_C) * r * sp[:, sl]
        a = jnp.exp(log_a)
        b = jnp.sqrt(-jnp.tanh(log_a) * (a * a + 1.0)) * (gi * xc[:, sl])
        for sh in (1, 2, 4):
            keep = rowmod >= sh
            a_s = jnp.where(keep, pltpu.roll(a, sh, 0), 1.0)
            b_s = jnp.where(keep, pltpu.roll(b, sh, 0), 0.0)
            b = a * b_s + b
            a = a * a_s
        carry = carry_scr[h:h + 1, :]
        outs = []
        for gidx in range(ts // SUBLANES):
            rows = slice(gidx * SUBLANES, (gidx + 1) * SUBLANES)
            hg = a[rows, :] * carry + b[rows, :]
            outs.append(hg)
            carry = hg[SUBLANES - 1:SUBLANES, :]
        carry_scr[h:h + 1, :] = carry
        hseq = jnp.concatenate(outs, axis=0)
        yb = _gelu_tanh(proj[:, sl])
        hy_scr[:, sl] = (hseq * yb).astype(BF16)

    o_ref[...] = x + _dot(hy_scr[...], wout_ref[...])


def _lru_layer(x, g, w_in, conv_w, conv_b, w_a, b_a, w_x, b_x, log_lambda, w_out, *, ts=256):
    bsz, s, d = x.shape
    w = d
    wgate = jnp.concatenate([w_a, w_x], axis=-1).astype(BF16)
    row = lambda v: v.reshape(1, -1).astype(F32)
    const2 = lambda shape: pl.BlockSpec(shape, lambda b, t: (0, 0))
    return pl.pallas_call(
        _lru_kernel,
        grid=(bsz, s // ts),
        in_specs=[
            pl.BlockSpec((None, ts, d), lambda b, t: (b, t, 0)),
            const2((1, d)),
            const2((d, 2 * w)),
            const2((CONV_WIDTH, w)),
            const2((1, w)),
            pl.BlockSpec((LRU_BLOCKS, LANES, 2 * LANES), lambda b, t: (0, 0, 0)),
            const2((1, w)),
            const2((1, w)),
            const2((1, w)),
            const2((w, d)),
        ],
        out_specs=pl.BlockSpec((None, ts, d), lambda b, t: (b, t, 0)),
        out_shape=jax.ShapeDtypeStruct(x.shape, F32),
        scratch_shapes=[
            pltpu.VMEM((ts + SUBLANES, w), F32),
            pltpu.VMEM((ts, w), BF16),
            pltpu.VMEM((LRU_BLOCKS, LANES), F32),
        ],
        compiler_params=pltpu.CompilerParams(
            dimension_semantics=("arbitrary", "arbitrary"), vmem_limit_bytes=48 * 1024 * 1024),
        name="lru_mixer",
    )(x, row(g), w_in.astype(BF16), conv_w.astype(F32), row(conv_b), wgate, row(b_a), row(b_x),
      row(log_lambda), w_out.astype(BF16))


def _fox_proj_kernel(x_ref, g_ref, w_ref, wf_ref, bf_ref, qg_ref, kg_ref, hm_ref,
                     q_ref, k_ref, v_ref, gate_ref, f_ref, carry_scr):
    t = pl.program_id(1)
    ts, d = x_ref.shape

    @pl.when(t == 0)
    def _():
        carry_scr[...] = jnp.zeros_like(carry_scr)

    hn = _rms_rows(x_ref[...], g_ref[...]).astype(BF16)
    proj = _dot(hn, w_ref[...])
    hm = hm_ref[...]

    def head_norm(z, gain):
        ms = _dot((z * z).astype(BF16), hm)
        return z * lax.rsqrt(ms + RMS_EPS) * gain

    q_ref[...] = head_norm(proj[:, :d], qg_ref[...]).astype(BF16)
    k_ref[...] = head_norm(proj[:, d:2 * d], kg_ref[...]).astype(BF16)
    v_ref[...] = proj[:, 2 * d:3 * d].astype(BF16)
    gate_ref[...] = _sigmoid(proj[:, 3 * d:]).astype(BF16)

    f = _dot(hn, wf_ref[...]) + bf_ref[...]
    lf = -_softplus(-f)
    r_i = lax.broadcasted_iota(jnp.int32, (ts, ts), 0)
    c_i = lax.broadcasted_iota(jnp.int32, (ts, ts), 1)
    ltri = jnp.where(c_i <= r_i, 1.0, 0.0).astype(BF16)
    hi, mid, lo = _split3(lf)
    fcum = (_dot(ltri, hi) + _dot(ltri, mid)) + _dot(ltri, lo) + carry_scr[...]
    carry_scr[...] = fcum[ts - 1:ts, :]
    f_ref[...] = fcum * LOG2E


def _fox_proj(x, g, w_in, b_f, q_gain, k_gain, *, ts=256):
    bsz, s, d = x.shape
    hd = d // FOX_HEADS
    w_main = w_in[:, :4 * d].astype(BF16)
    w_f = jnp.pad(w_in[:, 4 * d:], ((0, 0), (0, LANES - FOX_HEADS))).astype(BF16)
    bf_row = jnp.pad(b_f.astype(F32), (0, LANES - FOX_HEADS)).reshape(1, LANES)
    scale = hd ** -0.5
    qg_row = (jnp.tile(q_gain.astype(F32), FOX_HEADS) * (scale * LOG2E)).reshape(1, d)
    kg_row = jnp.tile(k_gain.astype(F32), FOX_HEADS).reshape(1, d)
    head_id = jnp.arange(d) // hd
    hm = jnp.where(head_id[:, None] == head_id[None, :], 1.0 / hd, 0.0).astype(BF16)
    const2 = lambda shape: pl.BlockSpec(shape, lambda b, t: (0, 0))
    tok = pl.BlockSpec((None, ts, d), lambda b, t: (b, t, 0))
    act = jax.ShapeDtypeStruct((bsz, s, d), BF16)
    return pl.pallas_call(
        _fox_proj_kernel,
        grid=(bsz, s // ts),
        in_specs=[tok, const2((1, d)), const2((d, 4 * d)), const2((d, LANES)), const2((1, LANES)),
                  const2((1, d)), const2((1, d)), const2((d, d))],
        out_specs=[tok, tok, tok, tok, pl.BlockSpec((None, ts, LANES), lambda b, t: (b, t, 0))],
        out_shape=[act, act, act, act, jax.ShapeDtypeStruct((bsz, s, LANES), F32)],
        scratch_shapes=[pltpu.VMEM((1, LANES), F32)],
        compiler_params=pltpu.CompilerParams(
            dimension_semantics=("arbitrary", "arbitrary"), vmem_limit_bytes=48 * 1024 * 1024),
        name="fox_proj",
    )(x, g.reshape(1, d).astype(F32), w_main, w_f, bf_row, qg_row, kg_row, hm)


def _aug_lanes(hd):
    base = SUBLANES * hd
    return base, base + 3


def _attn_placement(pairs):
    pq = jnp.zeros((pairs, 3, LANES, LANES), F32)
    pk = jnp.zeros((pairs, 2, 3, LANES, LANES), F32)
    for p in range(pairs):
        for hd in range(2):
            q_lo, k_lo = _aug_lanes(hd)
            for i in range(3):
                pq = pq.at[p, i, 2 * p + hd, q_lo + i].set(1.0)
                pk = pk.at[p, hd, i, 2 * p + hd, k_lo + i].set(1.0)
    return pq.astype(BF16), pk.astype(BF16)


def _place(x, p_ref, idx):
    hi, mid, lo = _split3(x)
    return (_dot(hi, p_ref[idx + (0,)]) + _dot(mid, p_ref[idx + (1,)])) + _dot(lo, p_ref[idx + (2,)])


def _fox_attn_fast_kernel(js_ref, c_ref, q_ref, k_ref, v_ref, f_ref, pq_ref, pk_ref, o_ref,
                          ka_scr, kb_scr, va_scr, vb_scr, acc_scr, *, chunk):
    qi = pl.program_id(2)
    nq = pl.num_programs(2)
    js_base = ((pl.program_id(0) * pl.num_programs(1) + pl.program_id(1)) * 2) * nq + qi
    tq = q_ref.shape[0]
    tk = tq
    s_len = k_ref.shape[0]
    half = LANES // 2
    k_scrs = (ka_scr, kb_scr)
    v_scrs = (va_scr, vb_scr)
    sum_lane = (half, half - 1)

    @pl.when(qi == 0)
    def _():
        def build(c, carry):
            rows = pl.ds(pl.multiple_of(c * chunk, chunk), chunk)
            kk = k_ref[rows, :]
            vv = v_ref[rows, :]
            lane = lax.broadcasted_iota(jnp.int32, (chunk, LANES), 1)
            nf = -f_ref[rows, :]
            for hd in range(2):
                ones_lo, _ = _aug_lanes(hd)
                aug = _place(nf, pk_ref, (hd,))
                aug = aug + jnp.where((lane >= ones_lo) & (lane < ones_lo + 3), 1.0, 0.0)
                keep = (lane < half) if hd == 0 else (lane >= half)
                k_scrs[hd][rows, 0:LANES] = jnp.where(keep, kk, jnp.zeros_like(kk))
                k_scrs[hd][rows, LANES:2 * LANES] = aug.astype(BF16)
                v_scrs[hd][rows, :] = jnp.where(lane == sum_lane[hd], jnp.ones_like(vv), vv)
            return carry
        lax.fori_loop(0, s_len // chunk, build, 0)

    lane_q = lax.broadcasted_iota(jnp.int32, (tq, LANES), 1)
    fq = f_ref[pl.ds(pl.multiple_of(qi * tq, tq), tq), :] - c_ref[...]
    augq = _place(fq, pq_ref, ())
    q_ones = ((lane_q >= 3) & (lane_q < 6)) | ((lane_q >= SUBLANES + 3) & (lane_q < SUBLANES + 6))
    augq = augq + jnp.where(q_ones, 1.0, 0.0)
    q_aug = jnp.concatenate([q_ref[...], augq.astype(BF16)], axis=1)
    acc_scr[...] = jnp.zeros_like(acc_scr)

    def probs(j, hd, masked=False):
        rows = pl.ds(pl.multiple_of(j * tk, tk), tk)
        s = lax.dot_general(q_aug, k_scrs[hd][rows, :], (((1,), (1,)), ((), ())),
                            preferred_element_type=F32)
        p = jnp.exp2(s)
        if masked:
            r_i = lax.broadcasted_iota(jnp.int32, (tq, tk), 0)
            c_i = lax.broadcasted_iota(jnp.int32, (tq, tk), 1)
            p = jnp.where(c_i <= r_i, p, 0.0)
        return p.astype(BF16), v_scrs[hd][rows, :]

    for hd in range(2):
        j0 = js_ref[js_base + hd * nq]
        n = qi - j0
        odd = n & 1

        @pl.when(odd == 1)
        def _():
            p0, v0 = probs(j0, hd)
            acc_scr[hd] += _dot(p0, v0)

        def body(t, carry):
            j = j0 + odd + 2 * t
            p0, v0 = probs(j, hd)
            p1, v1 = probs(j + 1, hd)
            acc_scr[hd] += _dot(p0, v0) + _dot(p1, v1)
            return carry

        lax.fori_loop(0, lax.shift_right_logical(n, 1), body, 0)

    for hd in range(2):
        pd, vd = probs(qi, hd, masked=True)
        acc_scr[hd] += _dot(pd, vd)

    acc_a = acc_scr[0]
    acc_b = acc_scr[1]
    oa = acc_a / acc_a[:, sum_lane[0]:sum_lane[0] + 1]
    ob = acc_b / acc_b[:, sum_lane[1]:sum_lane[1] + 1]
    o_ref[...] = jnp.where(lane_q < half, oa, ob).astype(o_ref.dtype)


def _first_live_block(f_tok, tq):
    bsz, s, _ = f_tok.shape
    nq = s // tq
    f_heads = f_tok[:, :, :FOX_HEADS]
    f_first = f_heads[:, 0::tq, :]
    f_last = f_heads[:, tq - 1::tq, :]
    gap = f_first[:, :, None, :] - f_last[:, None, :, :]
    blk = jnp.arange(nq)
    dead = (gap < -ATTN_DEAD_LOG2) & (blk[None, None, :, None] < blk[None, :, None, None])
    first = jnp.sum(dead, axis=2).astype(jnp.int32)
    return jnp.transpose(first, (0, 2, 1)).reshape(-1)


def _fox_attn_fast(q, k, v, f_tok, shift, *, tq=512, chunk=1024):
    bsz, s, d = q.shape
    pairs = d // LANES
    pq, pk = _attn_placement(pairs)
    c_row = jnp.full((1, LANES), shift * LOG2E, F32)
    first_live = _first_live_block(f_tok, tq)
    return pl.pallas_call(
        functools.partial(_fox_attn_fast_kernel, chunk=chunk),
        grid_spec=pltpu.PrefetchScalarGridSpec(
            num_scalar_prefetch=1,
            grid=(bsz, pairs, s // tq),
            in_specs=[
                pl.BlockSpec((1, LANES), lambda b, p, i, js: (0, 0)),
                pl.BlockSpec((None, tq, LANES), lambda b, p, i, js: (b, i, p)),
                pl.BlockSpec((None, s, LANES), lambda b, p, i, js: (b, 0, p)),
                pl.BlockSpec((None, s, LANES), lambda b, p, i, js: (b, 0, p)),
                pl.BlockSpec((None, s, LANES), lambda b, p, i, js: (b, 0, 0)),
                pl.BlockSpec((None, 3, LANES, LANES), lambda b, p, i, js: (p, 0, 0, 0)),
                pl.BlockSpec((None, 2, 3, LANES, LANES), lambda b, p, i, js: (p, 0, 0, 0, 0)),
            ],
            out_specs=pl.BlockSpec((None, tq, LANES), lambda b, p, i, js: (b, i, p)),
            scratch_shapes=[
                pltpu.VMEM((s, 2 * LANES), BF16),
                pltpu.VMEM((s, 2 * LANES), BF16),
                pltpu.VMEM((s, LANES), BF16),
                pltpu.VMEM((s, LANES), BF16),
                pltpu.VMEM((2, tq, LANES), F32),
            ],
        ),
        out_shape=jax.ShapeDtypeStruct((bsz, s, d), BF16),
        compiler_params=pltpu.CompilerParams(
            dimension_semantics=("arbitrary", "arbitrary", "arbitrary"),
            vmem_limit_bytes=48 * 1024 * 1024),
        name="fox_attn_fast",
    )(first_live, c_row, q, k, v, f_tok, pq, pk)


def _fox_attn_online_kernel(q_ref, k_ref, v_ref, ft_ref, o_ref, ka_scr, kb_scr, m_scr, l_scr, acc_scr):
    qi = pl.program_id(2)
    tq = q_ref.shape[0]
    tk = tq
    half = LANES // 2
    lane = lax.broadcasted_iota(jnp.int32, (1, LANES), 1)

    @pl.when(qi == 0)
    def _():
        kk = k_ref[...]
        lane_k = lax.broadcasted_iota(jnp.int32, kk.shape, 1)
        zero = jnp.zeros_like(kk)
        ka_scr[...] = jnp.where(lane_k < half, kk, zero)
        kb_scr[...] = jnp.where(lane_k >= half, kk, zero)

    m_scr[...] = jnp.full_like(m_scr, NEG_BIG)
    l_scr[...] = jnp.zeros_like(l_scr)
    acc_scr[...] = jnp.zeros_like(acc_scr)
    q = q_ref[...]

    def attend(j, masked):
        start = pl.multiple_of(j * tk, tk)
        v = v_ref[pl.ds(start, tk), :]
        for hd, k_scr in ((0, ka_scr), (1, kb_scr)):
            kb = k_scr[pl.ds(start, tk), :]
            s = lax.dot_general(q, kb, (((1,), (1,)), ((), ())), preferred_element_type=F32)
            s = s - ft_ref[hd, pl.ds(j, 1), :]
            if masked:
                r_i = lax.broadcasted_iota(jnp.int32, (tq, tk), 0)
                c_i = lax.broadcasted_iota(jnp.int32, (tq, tk), 1)
                s = jnp.where(c_i <= r_i, s, NEG_BIG)
            m_prev = m_scr[hd]
            m_new = jnp.maximum(m_prev, jnp.max(s, axis=1, keepdims=True))
            p = jnp.exp2(s - m_new[:, :1])
            alpha = jnp.exp2(m_prev - m_new)
            l_scr[hd] = alpha * l_scr[hd] + jnp.sum(p, axis=1, keepdims=True)
            acc_scr[hd] = alpha * acc_scr[hd] + _dot(p.astype(BF16), v)
            m_scr[hd] = m_new

    def body(j, carry):
        attend(j, False)
        return carry

    lax.fori_loop(0, qi, body, 0)
    attend(qi, True)

    oa = acc_scr[0] / l_scr[0]
    ob = acc_scr[1] / l_scr[1]
    o_ref[...] = jnp.where(lane < half, oa, ob).astype(o_ref.dtype)


def _fox_attn_online(q, k, v, f_tok, *, tq=512):
    bsz, s, d = q.shape
    pairs = d // LANES
    nq = s // tq
    ft5 = jnp.swapaxes(f_tok[:, :, :FOX_HEADS], 1, 2).reshape(bsz, pairs, 2, nq, tq)
    return pl.pallas_call(
        _fox_attn_online_kernel,
        grid=(bsz, pairs, nq),
        in_specs=[
            pl.BlockSpec((None, tq, LANES), lambda b, p, i: (b, i, p)),
            pl.BlockSpec((None, s, LANES), lambda b, p, i: (b, 0, p)),
            pl.BlockSpec((None, s, LANES), lambda b, p, i: (b, 0, p)),
            pl.BlockSpec((None, None, 2, nq, tq), lambda b, p, i: (b, p, 0, 0, 0)),
        ],
        out_specs=pl.BlockSpec((None, tq, LANES), lambda b, p, i: (b, i, p)),
        out_shape=jax.ShapeDtypeStruct((bsz, s, d), BF16),
        scratch_shapes=[
            pltpu.VMEM((s, LANES), BF16),
            pltpu.VMEM((s, LANES), BF16),
            pltpu.VMEM((2, tq, LANES), F32),
            pltpu.VMEM((2, tq, LANES), F32),
            pltpu.VMEM((2, tq, LANES), F32),
        ],
        compiler_params=pltpu.CompilerParams(
            dimension_semantics=("arbitrary", "arbitrary", "arbitrary"),
            vmem_limit_bytes=48 * 1024 * 1024),
        name="fox_attn_online",
    )(q, k, v, ft5)


def _fox_attn(q, k, v, f_tok, q_gain, k_gain):
    hd = q.shape[-1] // FOX_HEADS
    shift = (hd ** 0.5) * jnp.max(jnp.abs(q_gain)).astype(F32) * jnp.max(jnp.abs(k_gain)).astype(F32)
    return lax.cond(shift < ATTN_SHIFT_MAX,
                    lambda: _fox_attn_fast(q, k, v, f_tok, shift),
                    lambda: _fox_attn_online(q, k, v, f_tok))


def _fox_out_kernel(x_ref, o_ref, gate_ref, w_ref, out_ref):
    og = (o_ref[...].astype(F32) * gate_ref[...].astype(F32)).astype(BF16)
    out_ref[...] = x_ref[...] + _dot(og, w_ref[...])


def _fox_out(x2, o2, gate2, w_out, *, tm=512):
    t, d = x2.shape
    tok = pl.BlockSpec((tm, d), lambda i: (i, 0))
    return pl.pallas_call(
        _fox_out_kernel,
        grid=(t // tm,),
        in_specs=[tok, tok, tok, pl.BlockSpec((d, d), lambda i: (0, 0))],
        out_specs=tok,
        out_shape=jax.ShapeDtypeStruct((t, d), F32),
        compiler_params=pltpu.CompilerParams(dimension_semantics=("arbitrary",)),
        name="fox_out",
    )(x2, o2, gate2, w_out.astype(BF16))


def _ffn_kernel(x_ref, g_ref, wg_ref, wu_ref, wd_ref, o_ref, hn_scr, acc_scr):
    f = pl.program_id(1)

    @pl.when(f == 0)
    def _():
        x = x_ref[...]
        hn_scr[...] = _rms_rows(x, g_ref[...]).astype(BF16)
        acc_scr[...] = x

    hn = hn_scr[...]
    gg = _dot(hn, wg_ref[...])
    uu = _dot(hn, wu_ref[...])
    act = (gg * _sigmoid(gg)) * uu
    acc_scr[...] += _dot(act.astype(BF16), wd_ref[...])

    @pl.when(f == pl.num_programs(1) - 1)
    def _():
        o_ref[...] = acc_scr[...]


def _ffn_layer(x2, g, w_gate, w_up, w_down, *, tm=512, tf=896):
    t, d = x2.shape
    dff = w_gate.shape[1]
    tok = pl.BlockSpec((tm, d), lambda i, f: (i, 0))
    return pl.pallas_call(
        _ffn_kernel,
        grid=(t // tm, dff // tf),
        in_specs=[tok, pl.BlockSpec((1, d), lambda i, f: (0, 0)),
                  pl.BlockSpec((d, tf), lambda i, f: (0, f)),
                  pl.BlockSpec((d, tf), lambda i, f: (0, f)),
                  pl.BlockSpec((tf, d), lambda i, f: (f, 0))],
        out_specs=tok,
        out_shape=jax.ShapeDtypeStruct((t, d), F32),
        scratch_shapes=[pltpu.VMEM((tm, d), BF16), pltpu.VMEM((tm, d), F32)],
        compiler_params=pltpu.CompilerParams(
            dimension_semantics=("arbitrary", "arbitrary"), vmem_limit_bytes=48 * 1024 * 1024),
        name="dense_ffn",
    )(x2, g.reshape(1, d).astype(F32), w_gate.astype(BF16), w_up.astype(BF16), w_down.astype(BF16))


def _top2(logits, lane):
    m1 = jnp.max(logits, axis=1, keepdims=True)
    i1 = jnp.min(jnp.where(logits == m1, lane, LANES), axis=1, keepdims=True)
    rest = jnp.where(lane == i1, NEG_BIG, logits)
    m2 = jnp.max(rest, axis=1, keepdims=True)
    i2 = jnp.min(jnp.where(rest == m2, lane, LANES), axis=1, keepdims=True)
    return m1, i1, m2, i2


def _router_logits(hn, rhi_ref, rlo_ref):
    h_hi = hn.astype(BF16)
    h_lo = (hn - h_hi.astype(F32)).astype(BF16)
    logits = (_dot(h_hi, rhi_ref[...]) + _dot(h_lo, rhi_ref[...])) + _dot(h_hi, rlo_ref[...])
    lane = lax.broadcasted_iota(jnp.int32, logits.shape, 1)
    return jnp.where(lane < N_EXPERTS, logits, NEG_BIG), lane


def _moe_route_kernel(x_ref, g_ref, rhi_ref, rlo_ref, idx_ref, w_ref, cnt_ref, carry_scr):
    @pl.when(pl.program_id(0) == 0)
    def _():
        carry_scr[...] = jnp.zeros_like(carry_scr)

    tm = x_ref.shape[0]
    hn = _rms_rows(x_ref[...], g_ref[...])
    logits, lane = _router_logits(hn, rhi_ref, rlo_ref)
    m1, i1, m2, i2 = _top2(logits, lane)
    e2 = jnp.exp(m2 - m1)
    w1 = 1.0 / (1.0 + e2)
    w2 = e2 / (1.0 + e2)
    oh1 = lane == i1
    oh2 = lane == i2
    member = jnp.where(oh1 | oh2, 1.0, 0.0)
    r_i = lax.broadcasted_iota(jnp.int32, (tm, tm), 0)
    c_i = lax.broadcasted_iota(jnp.int32, (tm, tm), 1)
    lstrict = jnp.where(c_i < r_i, 1.0, 0.0).astype(BF16)
    before = _dot(lstrict, member.astype(BF16)) + carry_scr[...]
    rank1 = jnp.sum(jnp.where(oh1, before, 0.0), axis=1, keepdims=True)
    rank2 = jnp.sum(jnp.where(oh2, before, 0.0), axis=1, keepdims=True)
    carry_scr[...] += jnp.sum(member, axis=0, keepdims=True)
    cnt_ref[...] = carry_scr[...]
    w_ref[...] = jnp.where(lane == 0, w1, jnp.where(lane == 1, w2, 0.0))
    cols = (i1.astype(F32), i2.astype(F32), rank1, rank2)
    packed = jnp.zeros((tm, LANES), F32)
    for c, col in enumerate(cols):
        packed = jnp.where(lane == c, col, packed)
    idx_ref[...] = packed.T[:SUBLANES, :].astype(jnp.int32)


def _moe_dispatch_kernel(meta_ref, d_ref, x_ref, xs_ref, zero_scr, sem, zsem, *, n_e):
    tm = x_ref.shape[0]

    def body(t, carry):
        for slot in range(2):
            row = d_ref[slot, t]
            pltpu.make_async_copy(x_ref.at[pl.ds(t, 1), :], xs_ref.at[pl.ds(row, 1), :], sem).start()
        return carry

    lax.fori_loop(0, tm, body, 0, unroll=8)
    for slot in range(2):
        pltpu.make_async_copy(x_ref, xs_ref.at[pl.ds(0, tm), :], sem).wait()

    @pl.when(pl.program_id(0) == pl.num_programs(0) - 1)
    def _():
        zero_scr[...] = jnp.zeros_like(zero_scr)
        zero_row = zero_scr.at[pl.ds(0, 1), :]
        for e in range(n_e):
            start = meta_ref[e]
            n_pad = meta_ref[n_e + e]

            def fill(r, carry):
                pltpu.make_async_copy(zero_row, xs_ref.at[pl.ds(start + r, 1), :], zsem).start()
                return carry

            def drain(r, carry):
                pltpu.make_async_copy(zero_row, xs_ref.at[pl.ds(0, 1), :], zsem).wait()
                return carry

            lax.fori_loop(0, n_pad, fill, 0)
            lax.fori_loop(0, n_pad, drain, 0)
        used = meta_ref[2 * n_e]

        def fill_tile(g, carry):
            row0 = pl.multiple_of((used + g) * tm, tm)
            cp = pltpu.make_async_copy(zero_scr, xs_ref.at[pl.ds(row0, tm), :], zsem)
            cp.start()
            cp.wait()
            return carry

        lax.fori_loop(0, meta_ref[2 * n_e + 1], fill_tile, 0)


def _moe_group_kernel(exp_ref, rows_ref, xs_ref, g_ref, wg_ref, wu_ref, wd_ref, ys_ref, hn_scr, acc_scr):
    gi = pl.program_id(0)
    f = pl.program_id(1)
    last = f == pl.num_programs(1) - 1
    live = rows_ref[gi] > 0

    @pl.when(live)
    def _():
        @pl.when(f == 0)
        def _():
            hn_scr[...] = _rms_rows(xs_ref[...], g_ref[...]).astype(BF16)
            acc_scr[...] = jnp.zeros_like(acc_scr)

        hn = hn_scr[...]
        gg = _dot(hn, wg_ref[...])
        uu = _dot(hn, wu_ref[...])
        act = (gg * _sigmoid(gg)) * uu
        acc_scr[...] += _dot(act.astype(BF16), wd_ref[...])

        @pl.when(last)
        def _():
            ys_ref[...] = acc_scr[...]

    @pl.when(jnp.logical_not(live) & last)
    def _():
        ys_ref[...] = jnp.zeros_like(ys_ref)


def _moe_combine_kernel(d_ref, x_ref, w_ref, ys_ref, o_ref, buf, sem):
    tm = x_ref.shape[0]

    def body(t, carry):
        for slot in range(2):
            row = d_ref[slot, t]
            pltpu.make_async_copy(ys_ref.at[pl.ds(row, 1), :], buf.at[slot, pl.ds(t, 1), :], sem).start()
        return carry

    lax.fori_loop(0, tm, body, 0, unroll=8)
    for slot in range(2):
        pltpu.make_async_copy(ys_ref.at[pl.ds(0, tm), :], buf.at[slot], sem).wait()
    w = w_ref[...]
    o_ref[...] = x_ref[...] + w[:, 0:1] * buf[0] + w[:, 1:2] * buf[1]


def _moe_layer(x2, g, w_gate, w_up, w_down, router, *, tm=512, tf=896):
    t, d = x2.shape
    n_e, _, dff = w_gate.shape
    nt = t // tm
    n_tiles = (2 * t) // tm + n_e
    g_row = g.reshape(1, d).astype(F32)
    r_pad = jnp.pad(router.astype(F32), ((0, 0), (0, LANES - n_e)))
    r_hi = r_pad.astype(BF16)
    r_lo = (r_pad - r_hi.astype(F32)).astype(BF16)

    idx, wts, counts = pl.pallas_call(
        _moe_route_kernel,
        grid=(nt,),
        in_specs=[pl.BlockSpec((tm, d), lambda i: (i, 0)), pl.BlockSpec((1, d), lambda i: (0, 0)),
                  pl.BlockSpec((d, LANES), lambda i: (0, 0)), pl.BlockSpec((d, LANES), lambda i: (0, 0))],
        out_specs=[pl.BlockSpec((None, SUBLANES, tm), lambda i: (i, 0, 0)),
                   pl.BlockSpec((tm, LANES), lambda i: (i, 0)),
                   pl.BlockSpec((1, LANES), lambda i: (0, 0))],
        out_shape=[jax.ShapeDtypeStruct((nt, SUBLANES, tm), jnp.int32),
                   jax.ShapeDtypeStruct((t, LANES), F32),
                   jax.ShapeDtypeStruct((1, LANES), F32)],
        scratch_shapes=[pltpu.VMEM((1, LANES), F32)],
        compiler_params=pltpu.CompilerParams(dimension_semantics=("arbitrary",)),
        name="moe_route",
    )(x2, g_row, r_hi, r_lo)

    cnt = counts[0, :n_e].astype(jnp.int32)
    tiles_e = (cnt + tm - 1) // tm
    ends = jnp.cumsum(tiles_e)
    starts = ends - tiles_e
    used = ends[-1]
    dest = starts[idx[:, 0:2, :]] * tm + idx[:, 2:4, :]
    dest = jnp.concatenate([dest, jnp.zeros((nt, SUBLANES - 2, tm), jnp.int32)], axis=1)
    tile_ids = jnp.arange(n_tiles, dtype=jnp.int32)
    t_exp = jnp.minimum(jnp.searchsorted(ends, tile_ids, side="right"), n_e - 1).astype(jnp.int32)
    t_rows = jnp.where(tile_ids < used,
                       jnp.clip(cnt[t_exp] - (tile_ids - starts[t_exp]) * tm, 0, tm), 0).astype(jnp.int32)
    meta = jnp.concatenate([starts * tm + cnt, tiles_e * tm - cnt,
                            jnp.stack([used, n_tiles - used])]).astype(jnp.int32)

    smem_rows = lambda: pl.BlockSpec((None, SUBLANES, tm), lambda i, *_: (i, 0, 0), memory_space=pltpu.SMEM)
    xs = pl.pallas_call(
        functools.partial(_moe_dispatch_kernel, n_e=n_e),
        grid_spec=pltpu.PrefetchScalarGridSpec(
            num_scalar_prefetch=1,
            grid=(nt,),
            in_specs=[smem_rows(), pl.BlockSpec((tm, d), lambda i, meta: (i, 0))],
            out_specs=pl.BlockSpec(memory_space=pl.ANY),
            scratch_shapes=[pltpu.VMEM((tm, d), F32), pltpu.SemaphoreType.DMA(()),
                            pltpu.SemaphoreType.DMA(())],
        ),
        out_shape=jax.ShapeDtypeStruct((n_tiles * tm, d), F32),
        compiler_params=pltpu.CompilerParams(dimension_semantics=("arbitrary",)),
        name="moe_dispatch",
    )(meta, dest, x2)

    ys = pl.pallas_call(
        _moe_group_kernel,
        grid_spec=pltpu.PrefetchScalarGridSpec(
            num_scalar_prefetch=2,
            grid=(n_tiles, dff // tf),
            in_specs=[
                pl.BlockSpec((tm, d), lambda i, f, ex, rows: (i, 0)),
                pl.BlockSpec((1, d), lambda i, f, ex, rows: (0, 0)),
                pl.BlockSpec((None, d, tf), lambda i, f, ex, rows: (ex[i], 0, f)),
                pl.BlockSpec((None, d, tf), lambda i, f, ex, rows: (ex[i], 0, f)),
                pl.BlockSpec((None, tf, d), lambda i, f, ex, rows: (ex[i], f, 0)),
            ],
            out_specs=pl.BlockSpec((tm, d), lambda i, f, ex, rows: (i, 0)),
            scratch_shapes=[pltpu.VMEM((tm, d), BF16), pltpu.VMEM((tm, d), F32)],
        ),
        out_shape=jax.ShapeDtypeStruct((n_tiles * tm, d), F32),
        compiler_params=pltpu.CompilerParams(
            dimension_semantics=("arbitrary", "arbitrary"), vmem_limit_bytes=48 * 1024 * 1024),
        name="moe_group",
    )(t_exp, t_rows, xs, g_row, w_gate.astype(BF16), w_up.astype(BF16), w_down.astype(BF16))

    return pl.pallas_call(
        _moe_combine_kernel,
        grid=(nt,),
        in_specs=[smem_rows(), pl.BlockSpec((tm, d), lambda i: (i, 0)),
                  pl.BlockSpec((tm, LANES), lambda i: (i, 0)), pl.BlockSpec(memory_space=pl.ANY)],
        out_specs=pl.BlockSpec((tm, d), lambda i: (i, 0)),
        out_shape=jax.ShapeDtypeStruct((t, d), F32),
        scratch_shapes=[pltpu.VMEM((2, tm, d), F32), pltpu.SemaphoreType.DMA(())],
        compiler_params=pltpu.CompilerParams(dimension_semantics=("arbitrary",)),
        name="moe_combine",
    )(dest, x2, wts, ys)


def kernel(x, norm_mix, norm_ffn, lru_w_in, lru_conv_w, lru_conv_b, lru_w_a, lru_b_a, lru_w_x, lru_b_x,
           lru_log_lambda, lru_w_out, fox_w_in, fox_b_f, fox_q_gain, fox_k_gain, fox_w_out, ffn_w_gate,
           ffn_w_up, ffn_w_down, moe_router, moe_w_gate, moe_w_up, moe_w_down):
    bsz, s, d = x.shape
    depth = norm_mix.shape[0]
    for i in range(depth):
        j = i // 2
        if i % 2 == 0:
            x = _lru_layer(x, norm_mix[i], lru_w_in[j], lru_conv_w[j], lru_conv_b[j], lru_w_a[j],
                           lru_b_a[j], lru_w_x[j], lru_b_x[j], lru_log_lambda[j], lru_w_out[j])
            x2 = _ffn_layer(x.reshape(bsz * s, d), norm_ffn[i], ffn_w_gate[j], ffn_w_up[j], ffn_w_down[j])
        else:
            q, k, v, gate, ft = _fox_proj(x, norm_mix[i], fox_w_in[j], fox_b_f[j], fox_q_gain[j],
                                          fox_k_gain[j])
            o = _fox_attn(q, k, v, ft, fox_q_gain[j], fox_k_gain[j])
            x2 = _fox_out(x.reshape(bsz * s, d), o.reshape(bsz * s, d), gate.reshape(bsz * s, d),
                          fox_w_out[j])
            x2 = _moe_layer(x2, norm_ffn[i], moe_w_gate[j], moe_w_up[j], moe_w_down[j], moe_router[j])
        x = x2.reshape(bsz, s, d)
    return x
```

```python
import functools

import jax
import jax.numpy as jnp
from jax import lax
from jax.experimental import pallas as pl
from jax.experimental.pallas import tpu as pltpu

RMS_EPS = 1e-6
LRU_C = 8.0
LRU_BLOCKS = 8
CONV_WIDTH = 4
FOX_HEADS = 16
N_EXPERTS = 8
LANES = 128
SUBLANES = 8
NEG_BIG = -1e30
LOG2E = 1.4426950408889634
ATTN_SHIFT_MAX = 40.0
ATTN_DEAD_LOG2 = 160.0
ATTN_BLOCKS_PER_STEP = 4

F32 = jnp.float32
BF16 = jnp.bfloat16


def _dot(a, b):
    return jnp.dot(a, b, preferred_element_type=F32)


def _rms_rows(x, g):
    ms = jnp.mean(x * x, axis=-1, keepdims=True)
    return x * lax.rsqrt(ms + RMS_EPS) * g


def _sigmoid(x):
    return 1.0 / (1.0 + jnp.exp(-x))


def _softplus(x):
    return jnp.maximum(x, 0.0) + jnp.log1p(jnp.exp(-jnp.abs(x)))


def _gelu_tanh(x):
    return 0.5 * x * (1.0 + jnp.tanh(0.7978845608028654 * (x + 0.044715 * (x * x * x))))


def _split3(x):
    hi = x.astype(BF16)
    r1 = x - hi.astype(F32)
    mid = r1.astype(BF16)
    lo = (r1 - mid.astype(F32)).astype(BF16)
    return hi, mid, lo


def _lru_kernel(x_ref, g_ref, win_ref, cw_ref, cb_ref, wgate_ref, ba_ref, bx_ref, lam_ref, wout_ref,
                o_ref, xb_scr, hy_scr, carry_scr):
    t = pl.program_id(1)
    ts, d = x_ref.shape
    w = d

    @pl.when(t == 0)
    def _():
        xb_scr[0:SUBLANES, :] = jnp.zeros((SUBLANES, w), F32)
        carry_scr[...] = jnp.zeros_like(carry_scr)

    x = x_ref[...]
    hn = _rms_rows(x, g_ref[...]).astype(BF16)
    proj = _dot(hn, win_ref[...])
    xb = proj[:, w:]
    xb_scr[SUBLANES:SUBLANES + ts, :] = xb
    xc = cw_ref[3:4, :] * xb + cb_ref[...]
    for k in range(1, CONV_WIDTH):
        xc = xc + cw_ref[3 - k:4 - k, :] * xb_scr[SUBLANES - k:SUBLANES - k + ts, :]
    xb_scr[0:SUBLANES, :] = xb[ts - SUBLANES:, :]
    xcb = xc.astype(BF16)

    sp = _softplus(-lam_ref[...])
    rowmod = lax.broadcasted_iota(jnp.int32, (ts, LANES), 0) % SUBLANES
    for h in range(LRU_BLOCKS):
        sl = slice(h * LANES, (h + 1) * LANES)
        z = _dot(xcb[:, sl], wgate_ref[h])
        r = _sigmoid(z[:, :LANES] + ba_ref[:, sl])
        gi = _sigmoid(z[:, LANES:] + bx_ref[:, sl])
        log_a = (-LRU_C) * r * sp[:, sl]
        a = jnp.exp(log_a)
        b = jnp.sqrt(-jnp.tanh(log_a) * (a * a + 1.0)) * (gi * xc[:, sl])
        for sh in (1, 2, 4):
            keep = rowmod >= sh
            a_s = jnp.where(keep, pltpu.roll(a, sh, 0), 1.0)
            b_s = jnp.where(keep, pltpu.roll(b, sh, 0), 0.0)
            b = a * b_s + b
            a = a * a_s
        carry = carry_scr[h:h + 1, :]
        outs = []
        for gidx in range(ts // SUBLANES):
            rows = slice(gidx * SUBLANES, (gidx + 1) * SUBLANES)
            hg = a[rows, :] * carry + b[rows, :]
            outs.append(hg)
            carry = hg[SUBLANES - 1:SUBLANES, :]
        carry_scr[h:h + 1, :] = carry
        hseq = jnp.concatenate(outs, axis=0)
        yb = _gelu_tanh(proj[:, sl])
        hy_scr[:, sl] = (hseq * yb).astype(BF16)

    o_ref[...] = x + _dot(hy_scr[...], wout_ref[...])


def _lru_layer(x, g, w_in, conv_w, conv_b, w_a, b_a, w_x, b_x, log_lambda, w_out, *, ts=256):
    bsz, s, d = x.shape
    w = d
    wgate = jnp.concatenate([w_a, w_x], axis=-1).astype(BF16)
    row = lambda v: v.reshape(1, -1).astype(F32)
    const2 = lambda shape: pl.BlockSpec(shape, lambda b, t: (0, 0))
    return pl.pallas_call(
        _lru_kernel,
        grid=(bsz, s // ts),
        in_specs=[
            pl.BlockSpec((None, ts, d), lambda b, t: (b, t, 0)),
            const2((1, d)),
            const2((d, 2 * w)),
            const2((CONV_WIDTH, w)),
            const2((1, w)),
            pl.BlockSpec((LRU_BLOCKS, LANES, 2 * LANES), lambda b, t: (0, 0, 0)),
            const2((1, w)),
            const2((1, w)),
            const2((1, w)),
            const2((w, d)),
        ],
        out_specs=pl.BlockSpec((None, ts, d), lambda b, t: (b, t, 0)),
        out_shape=jax.ShapeDtypeStruct(x.shape, F32),
        scratch_shapes=[
            pltpu.VMEM((ts + SUBLANES, w), F32),
            pltpu.VMEM((ts, w), BF16),
            pltpu.VMEM((LRU_BLOCKS, LANES), F32),
        ],
        compiler_params=pltpu.CompilerParams(
            dimension_semantics=("arbitrary", "arbitrary"), vmem_limit_bytes=48 * 1024 * 1024),
        name="lru_mixer",
    )(x, row(g), w_in.astype(BF16), conv_w.astype(F32), row(conv_b), wgate, row(b_a), row(b_x),
      row(log_lambda), w_out.astype(BF16))


def _fox_proj_kernel(x_ref, g_ref, w_ref, wf_ref, bf_ref, qg_ref, kg_ref, hm_ref,
                     q_ref, k_ref, v_ref, gate_ref, f_ref, carry_scr):
    t = pl.program_id(1)
    ts, d = x_ref.shape

    @pl.when(t == 0)
    def _():
        carry_scr[...] = jnp.zeros_like(carry_scr)

    hn = _rms_rows(x_ref[...], g_ref[...]).astype(BF16)
    proj = _dot(hn, w_ref[...])
    hm = hm_ref[...]

    def head_norm(z, gain):
        ms = _dot((z * z).astype(BF16), hm)
        return z * lax.rsqrt(ms + RMS_EPS) * gain

    q_ref[...] = head_norm(proj[:, :d], qg_ref[...]).astype(BF16)
    k_ref[...] = head_norm(proj[:, d:2 * d], kg_ref[...]).astype(BF16)
    v_ref[...] = proj[:, 2 * d:3 * d].astype(BF16)
    gate_ref[...] = _sigmoid(proj[:, 3 * d:]).astype(BF16)

    f = _dot(hn, wf_ref[...]) + bf_ref[...]
    lf = -_softplus(-f)
    r_i = lax.broadcasted_iota(jnp.int32, (ts, ts), 0)
    c_i = lax.broadcasted_iota(jnp.int32, (ts, ts), 1)
    ltri = jnp.where(c_i <= r_i, 1.0, 0.0).astype(BF16)
    hi, mid, lo = _split3(lf)
    fcum = (_dot(ltri, hi) + _dot(ltri, mid)) + _dot(ltri, lo) + carry_scr[...]
    carry_scr[...] = fcum[ts - 1:ts, :]
    f_ref[...] = fcum * LOG2E


def _fox_proj(x, g, w_in, b_f, q_gain, k_gain, *, ts=256):
    bsz, s, d = x.shape
    hd = d // FOX_HEADS
    w_main = w_in[:, :4 * d].astype(BF16)
    w_f = jnp.pad(w_in[:, 4 * d:], ((0, 0), (0, LANES - FOX_HEADS))).astype(BF16)
    bf_row = jnp.pad(b_f.astype(F32), (0, LANES - FOX_HEADS)).reshape(1, LANES)
    scale = hd ** -0.5
    qg_row = (jnp.tile(q_gain.astype(F32), FOX_HEADS) * (scale * LOG2E)).reshape(1, d)
    kg_row = jnp.tile(k_gain.astype(F32), FOX_HEADS).reshape(1, d)
    head_id = jnp.arange(d) // hd
    hm = jnp.where(head_id[:, None] == head_id[None, :], 1.0 / hd, 0.0).astype(BF16)
    const2 = lambda shape: pl.BlockSpec(shape, lambda b, t: (0, 0))
    tok = pl.BlockSpec((None, ts, d), lambda b, t: (b, t, 0))
    act = jax.ShapeDtypeStruct((bsz, s, d), BF16)
    return pl.pallas_call(
        _fox_proj_kernel,
        grid=(bsz, s // ts),
        in_specs=[tok, const2((1, d)), const2((d, 4 * d)), const2((d, LANES)), const2((1, LANES)),
                  const2((1, d)), const2((1, d)), const2((d, d))],
        out_specs=[tok, tok, tok, tok, pl.BlockSpec((None, ts, LANES), lambda b, t: (b, t, 0))],
        out_shape=[act, act, act, act, jax.ShapeDtypeStruct((bsz, s, LANES), F32)],
        scratch_shapes=[pltpu.VMEM((1, LANES), F32)],
        compiler_params=pltpu.CompilerParams(
            dimension_semantics=("arbitrary", "arbitrary"), vmem_limit_bytes=48 * 1024 * 1024),
        name="fox_proj",
    )(x, g.reshape(1, d).astype(F32), w_main, w_f, bf_row, qg_row, kg_row, hm)


def _aug_lanes(hd):
    base = SUBLANES * hd
    return base, base + 3


def _attn_placement(pairs):
    pq = jnp.zeros((pairs, 3, LANES, LANES), F32)
    pk = jnp.zeros((pairs, 2, 3, LANES, LANES), F32)
    for p in range(pairs):
        for hd in range(2):
            q_lo, k_lo = _aug_lanes(hd)
            for i in range(3):
                pq = pq.at[p, i, 2 * p + hd, q_lo + i].set(1.0)
                pk = pk.at[p, hd, i, 2 * p + hd, k_lo + i].set(1.0)
    return pq.astype(BF16), pk.astype(BF16)


def _place(x, p_ref, idx):
    hi, mid, lo = _split3(x)
    return (_dot(hi, p_ref[idx + (0,)]) + _dot(mid, p_ref[idx + (1,)])) + _dot(lo, p_ref[idx + (2,)])


def _fox_attn_fast_kernel(js_ref, c_ref, q_ref, k_ref, v_ref, f_ref, pq_ref, pk_ref, o_ref,
                          ka_scr, kb_scr, va_scr, vb_scr, acc_scr, *, chunk):
    qi = pl.program_id(2)
    nq = pl.num_programs(2)
    js_base = ((pl.program_id(0) * pl.num_programs(1) + pl.program_id(1)) * 2) * nq + qi
    tq = q_ref.shape[0]
    tk = tq
    s_len = k_ref.shape[0]
    half = LANES // 2
    k_scrs = (ka_scr, kb_scr)
    v_scrs = (va_scr, vb_scr)
    sum_lane = (half, half - 1)

    @pl.when(qi == 0)
    def _():
        def build(c, carry):
            rows = pl.ds(pl.multiple_of(c * chunk, chunk), chunk)
            kk = k_ref[rows, :]
            vv = v_ref[rows, :]
            lane = lax.broadcasted_iota(jnp.int32, (chunk, LANES), 1)
            nf = -f_ref[rows, :]
            for hd in range(2):
                ones_lo, _ = _aug_lanes(hd)
                aug = _place(nf, pk_ref, (hd,))
                aug = aug + jnp.where((lane >= ones_lo) & (lane < ones_lo + 3), 1.0, 0.0)
                keep = (lane < half) if hd == 0 else (lane >= half)
                k_scrs[hd][rows, 0:LANES] = jnp.where(keep, kk, jnp.zeros_like(kk))
                k_scrs[hd][rows, LANES:2 * LANES] = aug.astype(BF16)
                v_scrs[hd][rows, :] = jnp.where(lane == sum_lane[hd], jnp.ones_like(vv), vv)
            return carry
        lax.fori_loop(0, s_len // chunk, build, 0)

    lane_q = lax.broadcasted_iota(jnp.int32, (tq, LANES), 1)
    fq = f_ref[pl.ds(pl.multiple_of(qi * tq, tq), tq), :] - c_ref[...]
    augq = _place(fq, pq_ref, ())
    q_ones = ((lane_q >= 3) & (lane_q < 6)) | ((lane_q >= SUBLANES + 3) & (lane_q < SUBLANES + 6))
    augq = augq + jnp.where(q_ones, 1.0, 0.0)
    q_aug = jnp.concatenate([q_ref[...], augq.astype(BF16)], axis=1)
    acc_scr[...] = jnp.zeros_like(acc_scr)

    def probs(j, hd, masked=False):
        rows = pl.ds(pl.multiple_of(j * tk, tk), tk)
        s = lax.dot_general(q_aug, k_scrs[hd][rows, :], (((1,), (1,)), ((), ())),
                            preferred_element_type=F32)
        p = jnp.exp2(s)
        if masked:
            r_i = lax.broadcasted_iota(jnp.int32, (tq, tk), 0)
            c_i = lax.broadcasted_iota(jnp.int32, (tq, tk), 1)
            p = jnp.where(c_i <= r_i, p, 0.0)
        return p.astype(BF16), v_scrs[hd][rows, :]

    def attend(j, hd, count):
        total = None
        for u in range(count):
            pu, vu = probs(j + u, hd)
            term = _dot(pu, vu)
            total = term if total is None else total + term
        acc_scr[hd] += total

    for hd in range(2):
        j0 = js_ref[js_base + hd * nq]
        n = qi - j0
        one = n & 1
        two = n & 2

        @pl.when(one != 0)
        def _():
            attend(j0, hd, 1)

        @pl.when(two != 0)
        def _():
            attend(j0 + one, hd, 2)

        def body(t, carry):
            attend(j0 + one + two + ATTN_BLOCKS_PER_STEP * t, hd, ATTN_BLOCKS_PER_STEP)
            return carry

        lax.fori_loop(0, lax.shift_right_logical(n, 2), body, 0)

    for hd in range(2):
        pd, vd = probs(qi, hd, masked=True)
        acc_scr[hd] += _dot(pd, vd)

    acc_a = acc_scr[0]
    acc_b = acc_scr[1]
    oa = acc_a / acc_a[:, sum_lane[0]:sum_lane[0] + 1]
    ob = acc_b / acc_b[:, sum_lane[1]:sum_lane[1] + 1]
    o_ref[...] = jnp.where(lane_q < half, oa, ob).astype(o_ref.dtype)


def _first_live_block(f_tok, tq):
    bsz, s, _ = f_tok.shape
    nq = s // tq
    f_heads = f_tok[:, :, :FOX_HEADS]
    f_first = f_heads[:, 0::tq, :]
    f_last = f_heads[:, tq - 1::tq, :]
    gap = f_first[:, :, None, :] - f_last[:, None, :, :]
    blk = jnp.arange(nq)
    dead = (gap < -ATTN_DEAD_LOG2) & (blk[None, None, :, None] < blk[None, :, None, None])
    first = jnp.sum(dead, axis=2).astype(jnp.int32)
    return jnp.transpose(first, (0, 2, 1)).reshape(-1)


def _fox_attn_fast(q, k, v, f_tok, shift, *, tq=512, chunk=1024):
    bsz, s, d = q.shape
    pairs = d // LANES
    pq, pk = _attn_placement(pairs)
    c_row = jnp.full((1, LANES), shift * LOG2E, F32)
    first_live = _first_live_block(f_tok, tq)
    return pl.pallas_call(
        functools.partial(_fox_attn_fast_kernel, chunk=chunk),
        grid_spec=pltpu.PrefetchScalarGridSpec(
            num_scalar_prefetch=1,
            grid=(bsz, pairs, s // tq),
            in_specs=[
                pl.BlockSpec((1, LANES), lambda b, p, i, js: (0, 0)),
                pl.BlockSpec((None, tq, LANES), lambda b, p, i, js: (b, i, p)),
                pl.BlockSpec((None, s, LANES), lambda b, p, i, js: (b, 0, p)),
                pl.BlockSpec((None, s, LANES), lambda b, p, i, js: (b, 0, p)),
                pl.BlockSpec((None, s, LANES), lambda b, p, i, js: (b, 0, 0)),
                pl.BlockSpec((None, 3, LANES, LANES), lambda b, p, i, js: (p, 0, 0, 0)),
                pl.BlockSpec((None, 2, 3, LANES, LANES), lambda b, p, i, js: (p, 0, 0, 0, 0)),
            ],
            out_specs=pl.BlockSpec((None, tq, LANES), lambda b, p, i, js: (b, i, p)),
            scratch_shapes=[
                pltpu.VMEM((s, 2 * LANES), BF16),
                pltpu.VMEM((s, 2 * LANES), BF16),
                pltpu.VMEM((s, LANES), BF16),
                pltpu.VMEM((s, LANES), BF16),
                pltpu.VMEM((2, tq, LANES), F32),
            ],
        ),
        out_shape=jax.ShapeDtypeStruct((bsz, s, d), BF16),
        compiler_params=pltpu.CompilerParams(
            dimension_semantics=("arbitrary", "arbitrary", "arbitrary"),
            vmem_limit_bytes=48 * 1024 * 1024),
        name="fox_attn_fast",
    )(first_live, c_row, q, k, v, f_tok, pq, pk)


def _fox_attn_online_kernel(q_ref, k_ref, v_ref, ft_ref, o_ref, ka_scr, kb_scr, m_scr, l_scr, acc_scr):
    qi = pl.program_id(2)
    tq = q_ref.shape[0]
    tk = tq
    half = LANES // 2
    lane = lax.broadcasted_iota(jnp.int32, (1, LANES), 1)

    @pl.when(qi == 0)
    def _():
        kk = k_ref[...]
        lane_k = lax.broadcasted_iota(jnp.int32, kk.shape, 1)
        zero = jnp.zeros_like(kk)
        ka_scr[...] = jnp.where(lane_k < half, kk, zero)
        kb_scr[...] = jnp.where(lane_k >= half, kk, zero)

    m_scr[...] = jnp.full_like(m_scr, NEG_BIG)
    l_scr[...] = jnp.zeros_like(l_scr)
    acc_scr[...] = jnp.zeros_like(acc_scr)
    q = q_ref[...]

    def attend(j, masked):
        start = pl.multiple_of(j * tk, tk)
        v = v_ref[pl.ds(start, tk), :]
        for hd, k_scr in ((0, ka_scr), (1, kb_scr)):
            kb = k_scr[pl.ds(start, tk), :]
            s = lax.dot_general(q, kb, (((1,), (1,)), ((), ())), preferred_element_type=F32)
            s = s - ft_ref[hd, pl.ds(j, 1), :]
            if masked:
                r_i = lax.broadcasted_iota(jnp.int32, (tq, tk), 0)
                c_i = lax.broadcasted_iota(jnp.int32, (tq, tk), 1)
                s = jnp.where(c_i <= r_i, s, NEG_BIG)
            m_prev = m_scr[hd]
            m_new = jnp.maximum(m_prev, jnp.max(s, axis=1, keepdims=True))
            p = jnp.exp2(s - m_new[:, :1])
            alpha = jnp.exp2(m_prev - m_new)
            l_scr[hd] = alpha * l_scr[hd] + jnp.sum(p, axis=1, keepdims=True)
            acc_scr[hd] = alpha * acc_scr[hd] + _dot(p.astype(BF16), v)
            m_scr[hd] = m_new

    def body(j, carry):
        attend(j, False)
        return carry

    lax.fori_loop(0, qi, body, 0)
    attend(qi, True)

    oa = acc_scr[0] / l_scr[0]
    ob = acc_scr[1] / l_scr[1]
    o_ref[...] = jnp.where(lane < half, oa, ob).astype(o_ref.dtype)


def _fox_attn_online(q, k, v, f_tok, *, tq=512):
    bsz, s, d = q.shape
    pairs = d // LANES
    nq = s // tq
    ft5 = jnp.swapaxes(f_tok[:, :, :FOX_HEADS], 1, 2).reshape(bsz, pairs, 2, nq, tq)
    return pl.pallas_call(
        _fox_attn_online_kernel,
        grid=(bsz, pairs, nq),
        in_specs=[
            pl.BlockSpec((None, tq, LANES), lambda b, p, i: (b, i, p)),
            pl.BlockSpec((None, s, LANES), lambda b, p, i: (b, 0, p)),
            pl.BlockSpec((None, s, LANES), lambda b, p, i: (b, 0, p)),
            pl.BlockSpec((None, None, 2, nq, tq), lambda b, p, i: (b, p, 0, 0, 0)),
        ],
        out_specs=pl.BlockSpec((None, tq, LANES), lambda b, p, i: (b, i, p)),
        out_shape=jax.ShapeDtypeStruct((bsz, s, d), BF16),
        scratch_shapes=[
            pltpu.VMEM((s, LANES), BF16),
            pltpu.VMEM((s, LANES), BF16),
            pltpu.VMEM((2, tq, LANES), F32),
            pltpu.VMEM((2, tq, LANES), F32),
            pltpu.VMEM((2, tq, LANES), F32),
        ],
        compiler_params=pltpu.CompilerParams(
            dimension_semantics=("arbitrary", "arbitrary", "arbitrary"),
            vmem_limit_bytes=48 * 1024 * 1024),
        name="fox_attn_online",
    )(q, k, v, ft5)


def _fox_attn(q, k, v, f_tok, q_gain, k_gain):
    hd = q.shape[-1] // FOX_HEADS
    shift = (hd ** 0.5) * jnp.max(jnp.abs(q_gain)).astype(F32) * jnp.max(jnp.abs(k_gain)).astype(F32)
    return lax.cond(shift < ATTN_SHIFT_MAX,
                    lambda: _fox_attn_fast(q, k, v, f_tok, shift),
                    lambda: _fox_attn_online(q, k, v, f_tok))


def _fox_out_kernel(x_ref, o_ref, gate_ref, w_ref, out_ref):
    og = (o_ref[...].astype(F32) * gate_ref[...].astype(F32)).astype(BF16)
    out_ref[...] = x_ref[...] + _dot(og, w_ref[...])


def _fox_out(x2, o2, gate2, w_out, *, tm=512):
    t, d = x2.shape
    tok = pl.BlockSpec((tm, d), lambda i: (i, 0))
    return pl.pallas_call(
        _fox_out_kernel,
        grid=(t // tm,),
        in_specs=[tok, tok, tok, pl.BlockSpec((d, d), lambda i: (0, 0))],
        out_specs=tok,
        out_shape=jax.ShapeDtypeStruct((t, d), F32),
        compiler_params=pltpu.CompilerParams(dimension_semantics=("arbitrary",)),
        name="fox_out",
    )(x2, o2, gate2, w_out.astype(BF16))


FFN_CHUNK = 512


def _swiglu_rows(hn, wg_ref, wu_ref, wd_ref, acc_scr):
    dff = wg_ref.shape[-1]
    for c in range(dff // FFN_CHUNK):
        cols = slice(c * FFN_CHUNK, (c + 1) * FFN_CHUNK)
        gg = _dot(hn, wg_ref[:, cols])
        uu = _dot(hn, wu_ref[:, cols])
        act = (gg * _sigmoid(gg)) * uu
        acc_scr[...] += _dot(act.astype(BF16), wd_ref[cols, :])


def _ffn_kernel(x_ref, g_ref, wg_ref, wu_ref, wd_ref, o_ref, acc_scr):
    x = x_ref[...]
    acc_scr[...] = x
    _swiglu_rows(_rms_rows(x, g_ref[...]).astype(BF16), wg_ref, wu_ref, wd_ref, acc_scr)
    o_ref[...] = acc_scr[...]


def _resident(shape, index_map):
    return pl.BlockSpec(shape, index_map, pipeline_mode=pl.Buffered(1))


def _ffn_layer(x2, g, w_gate, w_up, w_down, *, tm=512):
    t, d = x2.shape
    dff = w_gate.shape[1]
    tok = pl.BlockSpec((tm, d), lambda i: (i, 0))
    return pl.pallas_call(
        _ffn_kernel,
        grid=(t // tm,),
        in_specs=[tok, pl.BlockSpec((1, d), lambda i: (0, 0)),
                  _resident((d, dff), lambda i: (0, 0)),
                  _resident((d, dff), lambda i: (0, 0)),
                  _resident((dff, d), lambda i: (0, 0))],
        out_specs=tok,
        out_shape=jax.ShapeDtypeStruct((t, d), F32),
        scratch_shapes=[pltpu.VMEM((tm, d), F32)],
        compiler_params=pltpu.CompilerParams(
            dimension_semantics=("arbitrary",), vmem_limit_bytes=48 * 1024 * 1024),
        name="dense_ffn",
    )(x2, g.reshape(1, d).astype(F32), w_gate.astype(BF16), w_up.astype(BF16), w_down.astype(BF16))


def _top2(logits, lane):
    m1 = jnp.max(logits, axis=1, keepdims=True)
    i1 = jnp.min(jnp.where(logits == m1, lane, LANES), axis=1, keepdims=True)
    rest = jnp.where(lane == i1, NEG_BIG, logits)
    m2 = jnp.max(rest, axis=1, keepdims=True)
    i2 = jnp.min(jnp.where(rest == m2, lane, LANES), axis=1, keepdims=True)
    return m1, i1, m2, i2


def _router_logits(hn, rhi_ref, rlo_ref):
    h_hi = hn.astype(BF16)
    h_lo = (hn - h_hi.astype(F32)).astype(BF16)
    logits = (_dot(h_hi, rhi_ref[...]) + _dot(h_lo, rhi_ref[...])) + _dot(h_hi, rlo_ref[...])
    lane = lax.broadcasted_iota(jnp.int32, logits.shape, 1)
    return jnp.where(lane < N_EXPERTS, logits, NEG_BIG), lane


def _moe_route_kernel(x_ref, g_ref, rhi_ref, rlo_ref, idx_ref, w_ref, cnt_ref, carry_scr):
    @pl.when(pl.program_id(0) == 0)
    def _():
        carry_scr[...] = jnp.zeros_like(carry_scr)

    tm = x_ref.shape[0]
    hn = _rms_rows(x_ref[...], g_ref[...])
    logits, lane = _router_logits(hn, rhi_ref, rlo_ref)
    m1, i1, m2, i2 = _top2(logits, lane)
    e2 = jnp.exp(m2 - m1)
    w1 = 1.0 / (1.0 + e2)
    w2 = e2 / (1.0 + e2)
    oh1 = lane == i1
    oh2 = lane == i2
    member = jnp.where(oh1 | oh2, 1.0, 0.0)
    r_i = lax.broadcasted_iota(jnp.int32, (tm, tm), 0)
    c_i = lax.broadcasted_iota(jnp.int32, (tm, tm), 1)
    lstrict = jnp.where(c_i < r_i, 1.0, 0.0).astype(BF16)
    before = _dot(lstrict, member.astype(BF16)) + carry_scr[...]
    rank1 = jnp.sum(jnp.where(oh1, before, 0.0), axis=1, keepdims=True)
    rank2 = jnp.sum(jnp.where(oh2, before, 0.0), axis=1, keepdims=True)
    carry_scr[...] += jnp.sum(member, axis=0, keepdims=True)
    cnt_ref[...] = carry_scr[...]
    w_ref[...] = jnp.where(lane == 0, w1, jnp.where(lane == 1, w2, 0.0))
    cols = (i1.astype(F32), i2.astype(F32), rank1, rank2)
    packed = jnp.zeros((tm, LANES), F32)
    for c, col in enumerate(cols):
        packed = jnp.where(lane == c, col, packed)
    idx_ref[...] = packed.T[:SUBLANES, :].astype(jnp.int32)


def _moe_dispatch_kernel(meta_ref, d_ref, x_ref, xs_ref, zero_scr, sem, zsem, *, n_e):
    tm = x_ref.shape[0]

    def body(t, carry):
        for slot in range(2):
            row = d_ref[slot, t]
            pltpu.make_async_copy(x_ref.at[pl.ds(t, 1), :], xs_ref.at[pl.ds(row, 1), :], sem).start()
        return carry

    lax.fori_loop(0, tm, body, 0, unroll=8)
    for slot in range(2):
        pltpu.make_async_copy(x_ref, xs_ref.at[pl.ds(0, tm), :], sem).wait()

    @pl.when(pl.program_id(0) == pl.num_programs(0) - 1)
    def _():
        zero_scr[...] = jnp.zeros_like(zero_scr)
        zero_row = zero_scr.at[pl.ds(0, 1), :]
        for e in range(n_e):
            start = meta_ref[e]
            n_pad = meta_ref[n_e + e]

            def fill(r, carry):
                pltpu.make_async_copy(zero_row, xs_ref.at[pl.ds(start + r, 1), :], zsem).start()
                return carry

            def drain(r, carry):
                pltpu.make_async_copy(zero_row, xs_ref.at[pl.ds(0, 1), :], zsem).wait()
                return carry

            lax.fori_loop(0, n_pad, fill, 0)
            lax.fori_loop(0, n_pad, drain, 0)
        used = meta_ref[2 * n_e]

        def fill_tile(g, carry):
            row0 = pl.multiple_of((used + g) * tm, tm)
            cp = pltpu.make_async_copy(zero_scr, xs_ref.at[pl.ds(row0, tm), :], zsem)
            cp.start()
            cp.wait()
            return carry

        lax.fori_loop(0, meta_ref[2 * n_e + 1], fill_tile, 0)


def _moe_group_kernel(exp_ref, rows_ref, xs_ref, g_ref, wg_ref, wu_ref, wd_ref, ys_ref, acc_scr):
    live = rows_ref[pl.program_id(0)] > 0

    @pl.when(live)
    def _():
        acc_scr[...] = jnp.zeros_like(acc_scr)
        _swiglu_rows(_rms_rows(xs_ref[...], g_ref[...]).astype(BF16), wg_ref, wu_ref, wd_ref, acc_scr)
        ys_ref[...] = acc_scr[...]

    @pl.when(jnp.logical_not(live))
    def _():
        ys_ref[...] = jnp.zeros_like(ys_ref)


def _moe_combine_kernel(d_ref, x_ref, w_ref, ys_ref, o_ref, buf, sem):
    tm = x_ref.shape[0]

    def body(t, carry):
        for slot in range(2):
            row = d_ref[slot, t]
            pltpu.make_async_copy(ys_ref.at[pl.ds(row, 1), :], buf.at[slot, pl.ds(t, 1), :], sem).start()
        return carry

    lax.fori_loop(0, tm, body, 0, unroll=8)
    for slot in range(2):
        pltpu.make_async_copy(ys_ref.at[pl.ds(0, tm), :], buf.at[slot], sem).wait()
    w = w_ref[...]
    o_ref[...] = x_ref[...] + w[:, 0:1] * buf[0] + w[:, 1:2] * buf[1]


def _moe_layer(x2, g, w_gate, w_up, w_down, router, *, tm=512):
    t, d = x2.shape
    n_e, _, dff = w_gate.shape
    nt = t // tm
    n_tiles = (2 * t) // tm + n_e
    g_row = g.reshape(1, d).astype(F32)
    r_pad = jnp.pad(router.astype(F32), ((0, 0), (0, LANES - n_e)))
    r_hi = r_pad.astype(BF16)
    r_lo = (r_pad - r_hi.astype(F32)).astype(BF16)

    idx, wts, counts = pl.pallas_call(
        _moe_route_kernel,
        grid=(nt,),
        in_specs=[pl.BlockSpec((tm, d), lambda i: (i, 0)), pl.BlockSpec((1, d), lambda i: (0, 0)),
                  pl.BlockSpec((d, LANES), lambda i: (0, 0)), pl.BlockSpec((d, LANES), lambda i: (0, 0))],
        out_specs=[pl.BlockSpec((None, SUBLANES, tm), lambda i: (i, 0, 0)),
                   pl.BlockSpec((tm, LANES), lambda i: (i, 0)),
                   pl.BlockSpec((1, LANES), lambda i: (0, 0))],
        out_shape=[jax.ShapeDtypeStruct((nt, SUBLANES, tm), jnp.int32),
                   jax.ShapeDtypeStruct((t, LANES), F32),
                   jax.ShapeDtypeStruct((1, LANES), F32)],
        scratch_shapes=[pltpu.VMEM((1, LANES), F32)],
        compiler_params=pltpu.CompilerParams(dimension_semantics=("arbitrary",)),
        name="moe_route",
    )(x2, g_row, r_hi, r_lo)

    cnt = counts[0, :n_e].astype(jnp.int32)
    tiles_e = (cnt + tm - 1) // tm
    ends = jnp.cumsum(tiles_e)
    starts = ends - tiles_e
    used = ends[-1]
    dest = starts[idx[:, 0:2, :]] * tm + idx[:, 2:4, :]
    dest = jnp.concatenate([dest, jnp.zeros((nt, SUBLANES - 2, tm), jnp.int32)], axis=1)
    tile_ids = jnp.arange(n_tiles, dtype=jnp.int32)
    t_exp = jnp.minimum(jnp.searchsorted(ends, tile_ids, side="right"), n_e - 1).astype(jnp.int32)
    t_rows = jnp.where(tile_ids < used,
                       jnp.clip(cnt[t_exp] - (tile_ids - starts[t_exp]) * tm, 0, tm), 0).astype(jnp.int32)
    meta = jnp.concatenate([starts * tm + cnt, tiles_e * tm - cnt,
                            jnp.stack([used, n_tiles - used])]).astype(jnp.int32)

    smem_rows = lambda: pl.BlockSpec((None, SUBLANES, tm), lambda i, *_: (i, 0, 0), memory_space=pltpu.SMEM)
    xs = pl.pallas_call(
        functools.partial(_moe_dispatch_kernel, n_e=n_e),
        grid_spec=pltpu.PrefetchScalarGridSpec(
            num_scalar_prefetch=1,
            grid=(nt,),
            in_specs=[smem_rows(), pl.BlockSpec((tm, d), lambda i, meta: (i, 0))],
            out_specs=pl.BlockSpec(memory_space=pl.ANY),
            scratch_shapes=[pltpu.VMEM((tm, d), F32), pltpu.SemaphoreType.DMA(()),
                            pltpu.SemaphoreType.DMA(())],
        ),
        out_shape=jax.ShapeDtypeStruct((n_tiles * tm, d), F32),
        compiler_params=pltpu.CompilerParams(dimension_semantics=("arbitrary",)),
        name="moe_dispatch",
    )(meta, dest, x2)

    ys = pl.pallas_call(
        _moe_group_kernel,
        grid_spec=pltpu.PrefetchScalarGridSpec(
            num_scalar_prefetch=2,
            grid=(n_tiles,),
            in_specs=[
                pl.BlockSpec((tm, d), lambda i, ex, rows: (i, 0)),
                pl.BlockSpec((1, d), lambda i, ex, rows: (0, 0)),
                _resident((None, d, dff), lambda i, ex, rows: (ex[i], 0, 0)),
                _resident((None, d, dff), lambda i, ex, rows: (ex[i], 0, 0)),
                _resident((None, dff, d), lambda i, ex, rows: (ex[i], 0, 0)),
            ],
            out_specs=pl.BlockSpec((tm, d), lambda i, ex, rows: (i, 0)),
            scratch_shapes=[pltpu.VMEM((tm, d), F32)],
        ),
        out_shape=jax.ShapeDtypeStruct((n_tiles * tm, d), F32),
        compiler_params=pltpu.CompilerParams(
            dimension_semantics=("arbitrary",), vmem_limit_bytes=48 * 1024 * 1024),
        name="moe_group",
    )(t_exp, t_rows, xs, g_row, w_gate.astype(BF16), w_up.astype(BF16), w_down.astype(BF16))

    return pl.pallas_call(
        _moe_combine_kernel,
        grid=(nt,),
        in_specs=[smem_rows(), pl.BlockSpec((tm, d), lambda i: (i, 0)),
                  pl.BlockSpec((tm, LANES), lambda i: (i, 0)), pl.BlockSpec(memory_space=pl.ANY)],
        out_specs=pl.BlockSpec((tm, d), lambda i: (i, 0)),
        out_shape=jax.ShapeDtypeStruct((t, d), F32),
        scratch_shapes=[pltpu.VMEM((2, tm, d), F32), pltpu.SemaphoreType.DMA(())],
        compiler_params=pltpu.CompilerParams(dimension_semantics=("arbitrary",)),
        name="moe_combine",
    )(dest, x2, wts, ys)


def kernel(x, norm_mix, norm_ffn, lru_w_in, lru_conv_w, lru_conv_b, lru_w_a, lru_b_a, lru_w_x, lru_b_x,
           lru_log_lambda, lru_w_out, fox_w_in, fox_b_f, fox_q_gain, fox_k_gain, fox_w_out, ffn_w_gate,
           ffn_w_up, ffn_w_down, moe_router, moe_w_gate, moe_w_up, moe_w_down):
    bsz, s, d = x.shape
    depth = norm_mix.shape[0]
    for i in range(depth):
        j = i // 2
        if i % 2 == 0:
            x = _lru_layer(x, norm_mix[i], lru_w_in[j], lru_conv_w[j], lru_conv_b[j], lru_w_a[j],
                           lru_b_a[j], lru_w_x[j], lru_b_x[j], lru_log_lambda[j], lru_w_out[j])
            x2 = _ffn_layer(x.reshape(bsz * s, d), norm_ffn[i], ffn_w_gate[j], ffn_w_up[j], ffn_w_down[j])
        else:
            q, k, v, gate, ft = _fox_proj(x, norm_mix[i], fox_w_in[j], fox_b_f[j], fox_q_gain[j],
                                          fox_k_gain[j])
            o = _fox_attn(q, k, v, ft, fox_q_gain[j], fox_k_gain[j])
            x2 = _fox_out(x.reshape(bsz * s, d), o.reshape(bsz * s, d), gate.reshape(bsz * s, d),
                          fox_w_out[j])
            x2 = _moe_layer(x2, norm_ffn[i], moe_w_gate[j], moe_w_up[j], moe_w_down[j], moe_router[j])
        x = x2.reshape(bsz, s, d)
    return x
```

```python
import functools

import jax
import jax.numpy as jnp
from jax import lax
from jax.experimental import pallas as pl
from jax.experimental.pallas import tpu as pltpu

RMS_EPS = 1e-6
LRU_C = 8.0
LRU_BLOCKS = 8
CONV_WIDTH = 4
FOX_HEADS = 16
N_EXPERTS = 8
LANES = 128
SUBLANES = 8
NEG_BIG = -1e30
LOG2E = 1.4426950408889634
ATTN_SHIFT_MAX = 40.0
ATTN_DEAD_LOG2 = 160.0
ATTN_BLOCKS_PER_STEP = 8

F32 = jnp.float32
BF16 = jnp.bfloat16


def _dot(a, b):
    return jnp.dot(a, b, preferred_element_type=F32)


def _rms_rows(x, g):
    ms = jnp.mean(x * x, axis=-1, keepdims=True)
    return x * lax.rsqrt(ms + RMS_EPS) * g


def _sigmoid(x):
    return 1.0 / (1.0 + jnp.exp(-x))


def _softplus(x):
    return jnp.maximum(x, 0.0) + jnp.log1p(jnp.exp(-jnp.abs(x)))


def _gelu_tanh(x):
    return 0.5 * x * (1.0 + jnp.tanh(0.7978845608028654 * (x + 0.044715 * (x * x * x))))


def _split3(x):
    hi = x.astype(BF16)
    r1 = x - hi.astype(F32)
    mid = r1.astype(BF16)
    lo = (r1 - mid.astype(F32)).astype(BF16)
    return hi, mid, lo


def _lru_kernel(x_ref, g_ref, win_ref, cw_ref, cb_ref, wgate_ref, ba_ref, bx_ref, lam_ref, wout_ref,
                o_ref, xb_scr, hy_scr, carry_scr):
    t = pl.program_id(1)
    ts, d = x_ref.shape
    w = d

    @pl.when(t == 0)
    def _():
        xb_scr[0:SUBLANES, :] = jnp.zeros((SUBLANES, w), F32)
        carry_scr[...] = jnp.zeros_like(carry_scr)

    x = x_ref[...]
    hn = _rms_rows(x, g_ref[...]).astype(BF16)
    proj = _dot(hn, win_ref[...])
    xb = proj[:, w:]
    xb_scr[SUBLANES:SUBLANES + ts, :] = xb
    xc = cw_ref[3:4, :] * xb + cb_ref[...]
    for k in range(1, CONV_WIDTH):
        xc = xc + cw_ref[3 - k:4 - k, :] * xb_scr[SUBLANES - k:SUBLANES - k + ts, :]
    xb_scr[0:SUBLANES, :] = xb[ts - SUBLANES:, :]
    xcb = xc.astype(BF16)

    sp = _softplus(-lam_ref[...])
    groups = ts // SUBLANES
    rowmod = lax.broadcasted_iota(jnp.int32, (groups, SUBLANES, LANES), 1)
    for h in range(LRU_BLOCKS):
        sl = slice(h * LANES, (h + 1) * LANES)
        z = _dot(xcb[:, sl], wgate_ref[h])
        r = _sigmoid(z[:, :LANES] + ba_ref[:, sl])
        gi = _sigmoid(z[:, LANES:] + bx_ref[:, sl])
        log_a = (-LRU_C) * r * sp[:, sl]
        a = jnp.exp(log_a)
        b = jnp.sqrt(-jnp.tanh(log_a) * (a * a + 1.0)) * (gi * xc[:, sl])
        a = a.reshape(groups, SUBLANES, LANES)
        b = b.reshape(groups, SUBLANES, LANES)
        for sh in (1, 2, 4):
            keep = rowmod >= sh
            a_s = jnp.where(keep, pltpu.roll(a, sh, 1), 1.0)
            b_s = jnp.where(keep, pltpu.roll(b, sh, 1), 0.0)
            b = a * b_s + b
            a = a * a_s
        carry = carry_scr[h:h + 1, :]
        outs = []
        for gidx in range(groups):
            hg = a[gidx] * carry + b[gidx]
            outs.append(hg)
            carry = hg[SUBLANES - 1:SUBLANES, :]
        carry_scr[h:h + 1, :] = carry
        hseq = jnp.concatenate(outs, axis=0)
        yb = _gelu_tanh(proj[:, sl])
        hy_scr[:, sl] = (hseq * yb).astype(BF16)

    o_ref[...] = x + _dot(hy_scr[...], wout_ref[...])


def _lru_layer(x, g, w_in, conv_w, conv_b, w_a, b_a, w_x, b_x, log_lambda, w_out, *, ts=256):
    bsz, s, d = x.shape
    w = d
    wgate = jnp.concatenate([w_a, w_x], axis=-1).astype(BF16)
    row = lambda v: v.reshape(1, -1).astype(F32)
    const2 = lambda shape: pl.BlockSpec(shape, lambda b, t: (0, 0))
    return pl.pallas_call(
        _lru_kernel,
        grid=(bsz, s // ts),
        in_specs=[
            pl.BlockSpec((None, ts, d), lambda b, t: (b, t, 0)),
            const2((1, d)),
            const2((d, 2 * w)),
            const2((CONV_WIDTH, w)),
            const2((1, w)),
            pl.BlockSpec((LRU_BLOCKS, LANES, 2 * LANES), lambda b, t: (0, 0, 0)),
            const2((1, w)),
            const2((1, w)),
            const2((1, w)),
            const2((w, d)),
        ],
        out_specs=pl.BlockSpec((None, ts, d), lambda b, t: (b, t, 0)),
        out_shape=jax.ShapeDtypeStruct(x.shape, F32),
        scratch_shapes=[
            pltpu.VMEM((ts + SUBLANES, w), F32),
            pltpu.VMEM((ts, w), BF16),
            pltpu.VMEM((LRU_BLOCKS, LANES), F32),
        ],
        compiler_params=pltpu.CompilerParams(
            dimension_semantics=("arbitrary", "arbitrary"), vmem_limit_bytes=48 * 1024 * 1024),
        name="lru_mixer",
    )(x, row(g), w_in.astype(BF16), conv_w.astype(F32), row(conv_b), wgate, row(b_a), row(b_x),
      row(log_lambda), w_out.astype(BF16))


def _fox_proj_kernel(x_ref, g_ref, w_ref, wf_ref, bf_ref, qg_ref, kg_ref, hm_ref,
                     q_ref, k_ref, v_ref, gate_ref, f_ref, carry_scr):
    t = pl.program_id(1)
    ts, d = x_ref.shape

    @pl.when(t == 0)
    def _():
        carry_scr[...] = jnp.zeros_like(carry_scr)

    hn = _rms_rows(x_ref[...], g_ref[...]).astype(BF16)
    proj = _dot(hn, w_ref[...])
    hm = hm_ref[...]

    def head_norm(z, gain):
        ms = _dot((z * z).astype(BF16), hm)
        return z * lax.rsqrt(ms + RMS_EPS) * gain

    q_ref[...] = head_norm(proj[:, :d], qg_ref[...]).astype(BF16)
    k_ref[...] = head_norm(proj[:, d:2 * d], kg_ref[...]).astype(BF16)
    v_ref[...] = proj[:, 2 * d:3 * d].astype(BF16)
    gate_ref[...] = _sigmoid(proj[:, 3 * d:]).astype(BF16)

    f = _dot(hn, wf_ref[...]) + bf_ref[...]
    lf = -_softplus(-f)
    r_i = lax.broadcasted_iota(jnp.int32, (ts, ts), 0)
    c_i = lax.broadcasted_iota(jnp.int32, (ts, ts), 1)
    ltri = jnp.where(c_i <= r_i, 1.0, 0.0).astype(BF16)
    hi, mid, lo = _split3(lf)
    fcum = (_dot(ltri, hi) + _dot(ltri, mid)) + _dot(ltri, lo) + carry_scr[...]
    carry_scr[...] = fcum[ts - 1:ts, :]
    f_ref[...] = fcum * LOG2E


def _fox_proj(x, g, w_in, b_f, q_gain, k_gain, *, ts=256):
    bsz, s, d = x.shape
    hd = d // FOX_HEADS
    w_main = w_in[:, :4 * d].astype(BF16)
    w_f = jnp.pad(w_in[:, 4 * d:], ((0, 0), (0, LANES - FOX_HEADS))).astype(BF16)
    bf_row = jnp.pad(b_f.astype(F32), (0, LANES - FOX_HEADS)).reshape(1, LANES)
    scale = hd ** -0.5
    qg_row = (jnp.tile(q_gain.astype(F32), FOX_HEADS) * (scale * LOG2E)).reshape(1, d)
    kg_row = jnp.tile(k_gain.astype(F32), FOX_HEADS).reshape(1, d)
    head_id = jnp.arange(d) // hd
    hm = jnp.where(head_id[:, None] == head_id[None, :], 1.0 / hd, 0.0).astype(BF16)
    const2 = lambda shape: pl.BlockSpec(shape, lambda b, t: (0, 0))
    tok = pl.BlockSpec((None, ts, d), lambda b, t: (b, t, 0))
    act = jax.ShapeDtypeStruct((bsz, s, d), BF16)
    return pl.pallas_call(
        _fox_proj_kernel,
        grid=(bsz, s // ts),
        in_specs=[tok, const2((1, d)), const2((d, 4 * d)), const2((d, LANES)), const2((1, LANES)),
                  const2((1, d)), const2((1, d)), const2((d, d))],
        out_specs=[tok, tok, tok, tok, pl.BlockSpec((None, ts, LANES), lambda b, t: (b, t, 0))],
        out_shape=[act, act, act, act, jax.ShapeDtypeStruct((bsz, s, LANES), F32)],
        scratch_shapes=[pltpu.VMEM((1, LANES), F32)],
        compiler_params=pltpu.CompilerParams(
            dimension_semantics=("arbitrary", "arbitrary"), vmem_limit_bytes=48 * 1024 * 1024),
        name="fox_proj",
    )(x, g.reshape(1, d).astype(F32), w_main, w_f, bf_row, qg_row, kg_row, hm)


def _aug_lanes(hd):
    base = SUBLANES * hd
    return base, base + 3


def _attn_placement(pairs):
    pq = jnp.zeros((pairs, 3, LANES, LANES), F32)
    pk = jnp.zeros((pairs, 2, 3, LANES, LANES), F32)
    for p in range(pairs):
        for hd in range(2):
            q_lo, k_lo = _aug_lanes(hd)
            for i in range(3):
                pq = pq.at[p, i, 2 * p + hd, q_lo + i].set(1.0)
                pk = pk.at[p, hd, i, 2 * p + hd, k_lo + i].set(1.0)
    return pq.astype(BF16), pk.astype(BF16)


def _place(x, p_ref, idx):
    hi, mid, lo = _split3(x)
    return (_dot(hi, p_ref[idx + (0,)]) + _dot(mid, p_ref[idx + (1,)])) + _dot(lo, p_ref[idx + (2,)])


def _fox_attn_fast_kernel(js_ref, c_ref, q_ref, k_ref, v_ref, f_ref, pq_ref, pk_ref, o_ref,
                          ka_scr, kb_scr, va_scr, vb_scr, acc_scr, *, chunk):
    qi = pl.program_id(2)
    nq = pl.num_programs(2)
    js_base = ((pl.program_id(0) * pl.num_programs(1) + pl.program_id(1)) * 2) * nq + qi
    tq = q_ref.shape[0]
    tk = tq
    s_len = k_ref.shape[0]
    half = LANES // 2
    k_scrs = (ka_scr, kb_scr)
    v_scrs = (va_scr, vb_scr)
    sum_lane = (half, half - 1)

    @pl.when(qi == 0)
    def _():
        def build(c, carry):
            rows = pl.ds(pl.multiple_of(c * chunk, chunk), chunk)
            kk = k_ref[rows, :]
            vv = v_ref[rows, :]
            lane = lax.broadcasted_iota(jnp.int32, (chunk, LANES), 1)
            nf = -f_ref[rows, :]
            for hd in range(2):
                ones_lo, _ = _aug_lanes(hd)
                aug = _place(nf, pk_ref, (hd,))
                aug = aug + jnp.where((lane >= ones_lo) & (lane < ones_lo + 3), 1.0, 0.0)
                keep = (lane < half) if hd == 0 else (lane >= half)
                k_scrs[hd][rows, 0:LANES] = jnp.where(keep, kk, jnp.zeros_like(kk))
                k_scrs[hd][rows, LANES:2 * LANES] = aug.astype(BF16)
                v_scrs[hd][rows, :] = jnp.where(lane == sum_lane[hd], jnp.ones_like(vv), vv)
            return carry
        lax.fori_loop(0, s_len // chunk, build, 0)

    lane_q = lax.broadcasted_iota(jnp.int32, (tq, LANES), 1)
    fq = f_ref[pl.ds(pl.multiple_of(qi * tq, tq), tq), :] - c_ref[...]
    augq = _place(fq, pq_ref, ())
    q_ones = ((lane_q >= 3) & (lane_q < 6)) | ((lane_q >= SUBLANES + 3) & (lane_q < SUBLANES + 6))
    augq = augq + jnp.where(q_ones, 1.0, 0.0)
    q_aug = jnp.concatenate([q_ref[...], augq.astype(BF16)], axis=1)
    acc_scr[...] = jnp.zeros_like(acc_scr)

    def probs(j, hd, masked=False):
        rows = pl.ds(pl.multiple_of(j * tk, tk), tk)
        s = lax.dot_general(q_aug, k_scrs[hd][rows, :], (((1,), (1,)), ((), ())),
                            preferred_element_type=F32)
        p = jnp.exp2(s)
        if masked:
            r_i = lax.broadcasted_iota(jnp.int32, (tq, tk), 0)
            c_i = lax.broadcasted_iota(jnp.int32, (tq, tk), 1)
            p = jnp.where(c_i <= r_i, p, 0.0)
        return p.astype(BF16), v_scrs[hd][rows, :]

    def attend(j, hd, count):
        total = None
        for u in range(count):
            pu, vu = probs(j + u, hd)
            term = _dot(pu, vu)
            total = term if total is None else total + term
        acc_scr[hd] += total

    for hd in range(2):
        j0 = js_ref[js_base + hd * nq]
        n = qi - j0
        j = j0
        arm = 1
        while arm < ATTN_BLOCKS_PER_STEP:
            bit = n & arm

            @pl.when(bit != 0)
            def _(j=j, arm=arm):
                attend(j, hd, arm)

            j = j + bit
            arm *= 2

        def body(t, carry, j=j):
            attend(j + ATTN_BLOCKS_PER_STEP * t, hd, ATTN_BLOCKS_PER_STEP)
            return carry

        lax.fori_loop(0, lax.shift_right_logical(n, ATTN_BLOCKS_PER_STEP.bit_length() - 1), body, 0)

    for hd in range(2):
        pd, vd = probs(qi, hd, masked=True)
        acc_scr[hd] += _dot(pd, vd)

    acc_a = acc_scr[0]
    acc_b = acc_scr[1]
    oa = acc_a / acc_a[:, sum_lane[0]:sum_lane[0] + 1]
    ob = acc_b / acc_b[:, sum_lane[1]:sum_lane[1] + 1]
    o_ref[...] = jnp.where(lane_q < half, oa, ob).astype(o_ref.dtype)


def _first_live_block(f_tok, tq):
    bsz, s, _ = f_tok.shape
    nq = s // tq
    f_heads = f_tok[:, :, :FOX_HEADS]
    f_first = f_heads[:, 0::tq, :]
    f_last = f_heads[:, tq - 1::tq, :]
    gap = f_first[:, :, None, :] - f_last[:, None, :, :]
    blk = jnp.arange(nq)
    dead = (gap < -ATTN_DEAD_LOG2) & (blk[None, None, :, None] < blk[None, :, None, None])
    first = jnp.sum(dead, axis=2).astype(jnp.int32)
    return jnp.transpose(first, (0, 2, 1)).reshape(-1)


def _fox_attn_fast(q, k, v, f_tok, shift, *, tq=512, chunk=1024):
    bsz, s, d = q.shape
    pairs = d // LANES
    pq, pk = _attn_placement(pairs)
    c_row = jnp.full((1, LANES), shift * LOG2E, F32)
    first_live = _first_live_block(f_tok, tq)
    return pl.pallas_call(
        functools.partial(_fox_attn_fast_kernel, chunk=chunk),
        grid_spec=pltpu.PrefetchScalarGridSpec(
            num_scalar_prefetch=1,
            grid=(bsz, pairs, s // tq),
            in_specs=[
                pl.BlockSpec((1, LANES), lambda b, p, i, js: (0, 0)),
                pl.BlockSpec((None, tq, LANES), lambda b, p, i, js: (b, i, p)),
                pl.BlockSpec((None, s, LANES), lambda b, p, i, js: (b, 0, p)),
                pl.BlockSpec((None, s, LANES), lambda b, p, i, js: (b, 0, p)),
                pl.BlockSpec((None, s, LANES), lambda b, p, i, js: (b, 0, 0)),
                pl.BlockSpec((None, 3, LANES, LANES), lambda b, p, i, js: (p, 0, 0, 0)),
                pl.BlockSpec((None, 2, 3, LANES, LANES), lambda b, p, i, js: (p, 0, 0, 0, 0)),
            ],
            out_specs=pl.BlockSpec((None, tq, LANES), lambda b, p, i, js: (b, i, p)),
            scratch_shapes=[
                pltpu.VMEM((s, 2 * LANES), BF16),
                pltpu.VMEM((s, 2 * LANES), BF16),
                pltpu.VMEM((s, LANES), BF16),
                pltpu.VMEM((s, LANES), BF16),
                pltpu.VMEM((2, tq, LANES), F32),
            ],
        ),
        out_shape=jax.ShapeDtypeStruct((bsz, s, d), BF16),
        compiler_params=pltpu.CompilerParams(
            dimension_semantics=("arbitrary", "arbitrary", "arbitrary"),
            vmem_limit_bytes=48 * 1024 * 1024),
        name="fox_attn_fast",
    )(first_live, c_row, q, k, v, f_tok, pq, pk)


def _fox_attn_online_kernel(q_ref, k_ref, v_ref, ft_ref, o_ref, ka_scr, kb_scr, m_scr, l_scr, acc_scr):
    qi = pl.program_id(2)
    tq = q_ref.shape[0]
    tk = tq
    half = LANES // 2
    lane = lax.broadcasted_iota(jnp.int32, (1, LANES), 1)

    @pl.when(qi == 0)
    def _():
        kk = k_ref[...]
        lane_k = lax.broadcasted_iota(jnp.int32, kk.shape, 1)
        zero = jnp.zeros_like(kk)
        ka_scr[...] = jnp.where(lane_k < half, kk, zero)
        kb_scr[...] = jnp.where(lane_k >= half, kk, zero)

    m_scr[...] = jnp.full_like(m_scr, NEG_BIG)
    l_scr[...] = jnp.zeros_like(l_scr)
    acc_scr[...] = jnp.zeros_like(acc_scr)
    q = q_ref[...]

    def attend(j, masked):
        start = pl.multiple_of(j * tk, tk)
        v = v_ref[pl.ds(start, tk), :]
        for hd, k_scr in ((0, ka_scr), (1, kb_scr)):
            kb = k_scr[pl.ds(start, tk), :]
            s = lax.dot_general(q, kb, (((1,), (1,)), ((), ())), preferred_element_type=F32)
            s = s - ft_ref[hd, pl.ds(j, 1), :]
            if masked:
                r_i = lax.broadcasted_iota(jnp.int32, (tq, tk), 0)
                c_i = lax.broadcasted_iota(jnp.int32, (tq, tk), 1)
                s = jnp.where(c_i <= r_i, s, NEG_BIG)
            m_prev = m_scr[hd]
            m_new = jnp.maximum(m_prev, jnp.max(s, axis=1, keepdims=True))
            p = jnp.exp2(s - m_new[:, :1])
            alpha = jnp.exp2(m_prev - m_new)
            l_scr[hd] = alpha * l_scr[hd] + jnp.sum(p, axis=1, keepdims=True)
            acc_scr[hd] = alpha * acc_scr[hd] + _dot(p.astype(BF16), v)
            m_scr[hd] = m_new

    def body(j, carry):
        attend(j, False)
        return carry

    lax.fori_loop(0, qi, body, 0)
    attend(qi, True)

    oa = acc_scr[0] / l_scr[0]
    ob = acc_scr[1] / l_scr[1]
    o_ref[...] = jnp.where(lane < half, oa, ob).astype(o_ref.dtype)


def _fox_attn_online(q, k, v, f_tok, *, tq=512):
    bsz, s, d = q.shape
    pairs = d // LANES
    nq = s // tq
    ft5 = jnp.swapaxes(f_tok[:, :, :FOX_HEADS], 1, 2).reshape(bsz, pairs, 2, nq, tq)
    return pl.pallas_call(
        _fox_attn_online_kernel,
        grid=(bsz, pairs, nq),
        in_specs=[
            pl.BlockSpec((None, tq, LANES), lambda b, p, i: (b, i, p)),
            pl.BlockSpec((None, s, LANES), lambda b, p, i: (b, 0, p)),
            pl.BlockSpec((None, s, LANES), lambda b, p, i: (b, 0, p)),
            pl.BlockSpec((None, None, 2, nq, tq), lambda b, p, i: (b, p, 0, 0, 0)),
        ],
        out_specs=pl.BlockSpec((None, tq, LANES), lambda b, p, i: (b, i, p)),
        out_shape=jax.ShapeDtypeStruct((bsz, s, d), BF16),
        scratch_shapes=[
            pltpu.VMEM((s, LANES), BF16),
            pltpu.VMEM((s, LANES), BF16),
            pltpu.VMEM((2, tq, LANES), F32),
            pltpu.VMEM((2, tq, LANES), F32),
            pltpu.VMEM((2, tq, LANES), F32),
        ],
        compiler_params=pltpu.CompilerParams(
            dimension_semantics=("arbitrary", "arbitrary", "arbitrary"),
            vmem_limit_bytes=48 * 1024 * 1024),
        name="fox_attn_online",
    )(q, k, v, ft5)


def _fox_attn(q, k, v, f_tok, q_gain, k_gain):
    hd = q.shape[-1] // FOX_HEADS
    shift = (hd ** 0.5) * jnp.max(jnp.abs(q_gain)).astype(F32) * jnp.max(jnp.abs(k_gain)).astype(F32)
    return lax.cond(shift < ATTN_SHIFT_MAX,
                    lambda: _fox_attn_fast(q, k, v, f_tok, shift),
                    lambda: _fox_attn_online(q, k, v, f_tok))


def _fox_out_kernel(x_ref, o_ref, gate_ref, w_ref, out_ref):
    og = (o_ref[...].astype(F32) * gate_ref[...].astype(F32)).astype(BF16)
    out_ref[...] = x_ref[...] + _dot(og, w_ref[...])


def _fox_out(x2, o2, gate2, w_out, *, tm=512):
    t, d = x2.shape
    tok = pl.BlockSpec((tm, d), lambda i: (i, 0))
    return pl.pallas_call(
        _fox_out_kernel,
        grid=(t // tm,),
        in_specs=[tok, tok, tok, pl.BlockSpec((d, d), lambda i: (0, 0))],
        out_specs=tok,
        out_shape=jax.ShapeDtypeStruct((t, d), F32),
        compiler_params=pltpu.CompilerParams(dimension_semantics=("arbitrary",)),
        name="fox_out",
    )(x2, o2, gate2, w_out.astype(BF16))


FFN_CHUNK = 512


def _swiglu_rows(hn, wg_ref, wu_ref, wd_ref, acc_scr):
    dff = wg_ref.shape[-1]
    for c in range(dff // FFN_CHUNK):
        cols = slice(c * FFN_CHUNK, (c + 1) * FFN_CHUNK)
        gg = _dot(hn, wg_ref[:, cols])
        uu = _dot(hn, wu_ref[:, cols])
        act = (gg * _sigmoid(gg)) * uu
        acc_scr[...] += _dot(act.astype(BF16), wd_ref[cols, :])


def _ffn_kernel(x_ref, g_ref, wg_ref, wu_ref, wd_ref, o_ref, acc_scr):
    x = x_ref[...]
    acc_scr[...] = x
    _swiglu_rows(_rms_rows(x, g_ref[...]).astype(BF16), wg_ref, wu_ref, wd_ref, acc_scr)
    o_ref[...] = acc_scr[...]


def _resident(shape, index_map):
    return pl.BlockSpec(shape, index_map, pipeline_mode=pl.Buffered(1))


def _ffn_layer(x2, g, w_gate, w_up, w_down, *, tm=512):
    t, d = x2.shape
    dff = w_gate.shape[1]
    tok = pl.BlockSpec((tm, d), lambda i: (i, 0))
    return pl.pallas_call(
        _ffn_kernel,
        grid=(t // tm,),
        in_specs=[tok, pl.BlockSpec((1, d), lambda i: (0, 0)),
                  _resident((d, dff), lambda i: (0, 0)),
                  _resident((d, dff), lambda i: (0, 0)),
                  _resident((dff, d), lambda i: (0, 0))],
        out_specs=tok,
        out_shape=jax.ShapeDtypeStruct((t, d), F32),
        scratch_shapes=[pltpu.VMEM((tm, d), F32)],
        compiler_params=pltpu.CompilerParams(
            dimension_semantics=("arbitrary",), vmem_limit_bytes=48 * 1024 * 1024),
        name="dense_ffn",
    )(x2, g.reshape(1, d).astype(F32), w_gate.astype(BF16), w_up.astype(BF16), w_down.astype(BF16))


def _top2(logits, lane):
    m1 = jnp.max(logits, axis=1, keepdims=True)
    i1 = jnp.min(jnp.where(logits == m1, lane, LANES), axis=1, keepdims=True)
    rest = jnp.where(lane == i1, NEG_BIG, logits)
    m2 = jnp.max(rest, axis=1, keepdims=True)
    i2 = jnp.min(jnp.where(rest == m2, lane, LANES), axis=1, keepdims=True)
    return m1, i1, m2, i2


def _router_logits(hn, rhi_ref, rlo_ref):
    h_hi = hn.astype(BF16)
    h_lo = (hn - h_hi.astype(F32)).astype(BF16)
    logits = (_dot(h_hi, rhi_ref[...]) + _dot(h_lo, rhi_ref[...])) + _dot(h_hi, rlo_ref[...])
    lane = lax.broadcasted_iota(jnp.int32, logits.shape, 1)
    return jnp.where(lane < N_EXPERTS, logits, NEG_BIG), lane


def _moe_route_kernel(x_ref, g_ref, rhi_ref, rlo_ref, idx_ref, w_ref, cnt_ref, carry_scr):
    @pl.when(pl.program_id(0) == 0)
    def _():
        carry_scr[...] = jnp.zeros_like(carry_scr)

    tm = x_ref.shape[0]
    hn = _rms_rows(x_ref[...], g_ref[...])
    logits, lane = _router_logits(hn, rhi_ref, rlo_ref)
    m1, i1, m2, i2 = _top2(logits, lane)
    e2 = jnp.exp(m2 - m1)
    w1 = 1.0 / (1.0 + e2)
    w2 = e2 / (1.0 + e2)
    oh1 = lane == i1
    oh2 = lane == i2
    member = jnp.where(oh1 | oh2, 1.0, 0.0)
    r_i = lax.broadcasted_iota(jnp.int32, (tm, tm), 0)
    c_i = lax.broadcasted_iota(jnp.int32, (tm, tm), 1)
    lstrict = jnp.where(c_i < r_i, 1.0, 0.0).astype(BF16)
    before = _dot(lstrict, member.astype(BF16)) + carry_scr[...]
    rank1 = jnp.sum(jnp.where(oh1, before, 0.0), axis=1, keepdims=True)
    rank2 = jnp.sum(jnp.where(oh2, before, 0.0), axis=1, keepdims=True)
    carry_scr[...] += jnp.sum(member, axis=0, keepdims=True)
    cnt_ref[...] = carry_scr[...]
    w_ref[...] = jnp.where(lane == 0, w1, jnp.where(lane == 1, w2, 0.0))
    cols = (i1.astype(F32), i2.astype(F32), rank1, rank2)
    packed = jnp.zeros((tm, LANES), F32)
    for c, col in enumerate(cols):
        packed = jnp.where(lane == c, col, packed)
    idx_ref[...] = packed.T[:SUBLANES, :].astype(jnp.int32)


def _moe_dispatch_kernel(meta_ref, d_ref, x_ref, xs_ref, zero_scr, sem, zsem, *, n_e):
    tm = x_ref.shape[0]

    for t in range(tm):
        for slot in range(2):
            row = d_ref[slot, t]
            pltpu.make_async_copy(x_ref.at[pl.ds(t, 1), :], xs_ref.at[pl.ds(row, 1), :], sem).start()
    for slot in range(2):
        pltpu.make_async_copy(x_ref, xs_ref.at[pl.ds(0, tm), :], sem).wait()

    @pl.when(pl.program_id(0) == pl.num_programs(0) - 1)
    def _():
        zero_scr[...] = jnp.zeros_like(zero_scr)
        zero_row = zero_scr.at[pl.ds(0, 1), :]
        for e in range(n_e):
            start = meta_ref[e]
            n_pad = meta_ref[n_e + e]

            def fill(r, carry):
                pltpu.make_async_copy(zero_row, xs_ref.at[pl.ds(start + r, 1), :], zsem).start()
                return carry

            def drain(r, carry):
                pltpu.make_async_copy(zero_row, xs_ref.at[pl.ds(0, 1), :], zsem).wait()
                return carry

            lax.fori_loop(0, n_pad, fill, 0)
            lax.fori_loop(0, n_pad, drain, 0)
        used = meta_ref[2 * n_e]

        def fill_tile(g, carry):
            row0 = pl.multiple_of((used + g) * tm, tm)
            cp = pltpu.make_async_copy(zero_scr, xs_ref.at[pl.ds(row0, tm), :], zsem)
            cp.start()
            cp.wait()
            return carry

        lax.fori_loop(0, meta_ref[2 * n_e + 1], fill_tile, 0)


def _moe_group_kernel(exp_ref, rows_ref, xs_ref, g_ref, wg_ref, wu_ref, wd_ref, ys_ref, acc_scr):
    live = rows_ref[pl.program_id(0)] > 0

    @pl.when(live)
    def _():
        acc_scr[...] = jnp.zeros_like(acc_scr)
        _swiglu_rows(_rms_rows(xs_ref[...], g_ref[...]).astype(BF16), wg_ref, wu_ref, wd_ref, acc_scr)
        ys_ref[...] = acc_scr[...]

    @pl.when(jnp.logical_not(live))
    def _():
        ys_ref[...] = jnp.zeros_like(ys_ref)


def _moe_combine_kernel(d_ref, x_ref, w_ref, ys_ref, o_ref, buf, sem):
    tm = x_ref.shape[0]

    for t in range(tm):
        for slot in range(2):
            row = d_ref[slot, t]
            pltpu.make_async_copy(ys_ref.at[pl.ds(row, 1), :], buf.at[slot, pl.ds(t, 1), :], sem).start()
    for slot in range(2):
        pltpu.make_async_copy(ys_ref.at[pl.ds(0, tm), :], buf.at[slot], sem).wait()
    w = w_ref[...]
    o_ref[...] = x_ref[...] + w[:, 0:1] * buf[0] + w[:, 1:2] * buf[1]


def _moe_layer(x2, g, w_gate, w_up, w_down, router, *, tm=512):
    t, d = x2.shape
    n_e, _, dff = w_gate.shape
    nt = t // tm
    n_tiles = (2 * t) // tm + n_e
    g_row = g.reshape(1, d).astype(F32)
    r_pad = jnp.pad(router.astype(F32), ((0, 0), (0, LANES - n_e)))
    r_hi = r_pad.astype(BF16)
    r_lo = (r_pad - r_hi.astype(F32)).astype(BF16)

    idx, wts, counts = pl.pallas_call(
        _moe_route_kernel,
        grid=(nt,),
        in_specs=[pl.BlockSpec((tm, d), lambda i: (i, 0)), pl.BlockSpec((1, d), lambda i: (0, 0)),
                  pl.BlockSpec((d, LANES), lambda i: (0, 0)), pl.BlockSpec((d, LANES), lambda i: (0, 0))],
        out_specs=[pl.BlockSpec((None, SUBLANES, tm), lambda i: (i, 0, 0)),
                   pl.BlockSpec((tm, LANES), lambda i: (i, 0)),
                   pl.BlockSpec((1, LANES), lambda i: (0, 0))],
        out_shape=[jax.ShapeDtypeStruct((nt, SUBLANES, tm), jnp.int32),
                   jax.ShapeDtypeStruct((t, LANES), F32),
                   jax.ShapeDtypeStruct((1, LANES), F32)],
        scratch_shapes=[pltpu.VMEM((1, LANES), F32)],
        compiler_params=pltpu.CompilerParams(dimension_semantics=("arbitrary",)),
        name="moe_route",
    )(x2, g_row, r_hi, r_lo)

    cnt = counts[0, :n_e].astype(jnp.int32)
    tiles_e = (cnt + tm - 1) // tm
    ends = jnp.cumsum(tiles_e)
    starts = ends - tiles_e
    used = ends[-1]
    dest = starts[idx[:, 0:2, :]] * tm + idx[:, 2:4, :]
    dest = jnp.concatenate([dest, jnp.zeros((nt, SUBLANES - 2, tm), jnp.int32)], axis=1)
    tile_ids = jnp.arange(n_tiles, dtype=jnp.int32)
    t_exp = jnp.minimum(jnp.searchsorted(ends, tile_ids, side="right"), n_e - 1).astype(jnp.int32)
    t_rows = jnp.where(tile_ids < used,
                       jnp.clip(cnt[t_exp] - (tile_ids - starts[t_exp]) * tm, 0, tm), 0).astype(jnp.int32)
    meta = jnp.concatenate([starts * tm + cnt, tiles_e * tm - cnt,
                            jnp.stack([used, n_tiles - used])]).astype(jnp.int32)

    smem_rows = lambda: pl.BlockSpec((None, SUBLANES, tm), lambda i, *_: (i, 0, 0), memory_space=pltpu.SMEM)
    xs = pl.pallas_call(
        functools.partial(_moe_dispatch_kernel, n_e=n_e),
        grid_spec=pltpu.PrefetchScalarGridSpec(
            num_scalar_prefetch=1,
            grid=(nt,),
            in_specs=[smem_rows(), pl.BlockSpec((tm, d), lambda i, meta: (i, 0))],
            out_specs=pl.BlockSpec(memory_space=pl.ANY),
            scratch_shapes=[pltpu.VMEM((tm, d), F32), pltpu.SemaphoreType.DMA(()),
                            pltpu.SemaphoreType.DMA(())],
        ),
        out_shape=jax.ShapeDtypeStruct((n_tiles * tm, d), F32),
        compiler_params=pltpu.CompilerParams(dimension_semantics=("arbitrary",)),
        name="moe_dispatch",
    )(meta, dest, x2)

    ys = pl.pallas_call(
        _moe_group_kernel,
        grid_spec=pltpu.PrefetchScalarGridSpec(
            num_scalar_prefetch=2,
            grid=(n_tiles,),
            in_specs=[
                pl.BlockSpec((tm, d), lambda i, ex, rows: (i, 0)),
                pl.BlockSpec((1, d), lambda i, ex, rows: (0, 0)),
                _resident((None, d, dff), lambda i, ex, rows: (ex[i], 0, 0)),
                _resident((None, d, dff), lambda i, ex, rows: (ex[i], 0, 0)),
                _resident((None, dff, d), lambda i, ex, rows: (ex[i], 0, 0)),
            ],
            out_specs=pl.BlockSpec((tm, d), lambda i, ex, rows: (i, 0)),
            scratch_shapes=[pltpu.VMEM((tm, d), F32)],
        ),
        out_shape=jax.ShapeDtypeStruct((n_tiles * tm, d), F32),
        compiler_params=pltpu.CompilerParams(
            dimension_semantics=("arbitrary",), vmem_limit_bytes=48 * 1024 * 1024),
        name="moe_group",
    )(t_exp, t_rows, xs, g_row, w_gate.astype(BF16), w_up.astype(BF16), w_down.astype(BF16))

    return pl.pallas_call(
        _moe_combine_kernel,
        grid=(nt,),
        in_specs=[smem_rows(), pl.BlockSpec((tm, d), lambda i: (i, 0)),
                  pl.BlockSpec((tm, LANES), lambda i: (i, 0)), pl.BlockSpec(memory_space=pl.ANY)],
        out_specs=pl.BlockSpec((tm, d), lambda i: (i, 0)),
        out_shape=jax.ShapeDtypeStruct((t, d), F32),
        scratch_shapes=[pltpu.VMEM((2, tm, d), F32), pltpu.SemaphoreType.DMA(())],
        compiler_params=pltpu.CompilerParams(dimension_semantics=("arbitrary",)),
        name="moe_combine",
    )(dest, x2, wts, ys)


def kernel(x, norm_mix, norm_ffn, lru_w_in, lru_conv_w, lru_conv_b, lru_w_a, lru_b_a, lru_w_x, lru_b_x,
           lru_log_lambda, lru_w_out, fox_w_in, fox_b_f, fox_q_gain, fox_k_gain, fox_w_out, ffn_w_gate,
           ffn_w_up, ffn_w_down, moe_router, moe_w_gate, moe_w_up, moe_w_down):
    bsz, s, d = x.shape
    depth = norm_mix.shape[0]
    for i in range(depth):
        j = i // 2
        if i % 2 == 0:
            x = _lru_layer(x, norm_mix[i], lru_w_in[j], lru_conv_w[j], lru_conv_b[j], lru_w_a[j],
                           lru_b_a[j], lru_w_x[j], lru_b_x[j], lru_log_lambda[j], lru_w_out[j])
            x2 = _ffn_layer(x.reshape(bsz * s, d), norm_ffn[i], ffn_w_gate[j], ffn_w_up[j], ffn_w_down[j])
        else:
            q, k, v, gate, ft = _fox_proj(x, norm_mix[i], fox_w_in[j], fox_b_f[j], fox_q_gain[j],
                                          fox_k_gain[j])
            o = _fox_attn(q, k, v, ft, fox_q_gain[j], fox_k_gain[j])
            x2 = _fox_out(x.reshape(bsz * s, d), o.reshape(bsz * s, d), gate.reshape(bsz * s, d),
                          fox_w_out[j])
            x2 = _moe_layer(x2, norm_ffn[i], moe_w_gate[j], moe_w_up[j], moe_w_down[j], moe_router[j])
        x = x2.reshape(bsz, s, d)
    return x
```

```python
import functools

import jax
import jax.numpy as jnp
from jax import lax
from jax.experimental import pallas as pl
from jax.experimental.pallas import tpu as pltpu

RMS_EPS = 1e-6
LRU_C = 8.0
LRU_BLOCKS = 8
CONV_WIDTH = 4
FOX_HEADS = 16
N_EXPERTS = 8
LANES = 128
SUBLANES = 8
NEG_BIG = -1e30
LOG2E = 1.4426950408889634
ATTN_SHIFT_MAX = 40.0
ATTN_DEAD_LOG2 = 160.0
ATTN_BLOCKS_PER_STEP = 8

F32 = jnp.float32
BF16 = jnp.bfloat16


def _dot(a, b):
    return jnp.dot(a, b, preferred_element_type=F32)


def _rms_rows(x, g):
    ms = jnp.mean(x * x, axis=-1, keepdims=True)
    return x * lax.rsqrt(ms + RMS_EPS) * g


def _sigmoid(x):
    return 1.0 / (1.0 + jnp.exp(-x))


def _softplus(x):
    return jnp.maximum(x, 0.0) + jnp.log1p(jnp.exp(-jnp.abs(x)))


def _gelu_tanh(x):
    return 0.5 * x * (1.0 + jnp.tanh(0.7978845608028654 * (x + 0.044715 * (x * x * x))))


def _split3(x):
    hi = x.astype(BF16)
    r1 = x - hi.astype(F32)
    mid = r1.astype(BF16)
    lo = (r1 - mid.astype(F32)).astype(BF16)
    return hi, mid, lo


def _lru_kernel(x_ref, g_ref, win_ref, cw_ref, cb_ref, wgate_ref, ba_ref, bx_ref, lam_ref, wout_ref,
                o_ref, xb_scr, hy_scr, carry_scr):
    t = pl.program_id(1)
    ts, d = x_ref.shape
    w = d

    @pl.when(t == 0)
    def _():
        xb_scr[0:SUBLANES, :] = jnp.zeros((SUBLANES, w), F32)
        carry_scr[...] = jnp.zeros_like(carry_scr)

    x = x_ref[...]
    hn = _rms_rows(x, g_ref[...]).astype(BF16)
    proj = _dot(hn, win_ref[...])
    xb = proj[:, w:]
    xb_scr[SUBLANES:SUBLANES + ts, :] = xb
    xc = cw_ref[3:4, :] * xb + cb_ref[...]
    for k in range(1, CONV_WIDTH):
        xc = xc + cw_ref[3 - k:4 - k, :] * xb_scr[SUBLANES - k:SUBLANES - k + ts, :]
    xb_scr[0:SUBLANES, :] = xb[ts - SUBLANES:, :]
    xcb = xc.astype(BF16)

    sp = _softplus(-lam_ref[...])
    groups = ts // SUBLANES
    rowmod = lax.broadcasted_iota(jnp.int32, (groups, SUBLANES, LANES), 1)
    for h in range(LRU_BLOCKS):
        sl = slice(h * LANES, (h + 1) * LANES)
        z = _dot(xcb[:, sl], wgate_ref[h])
        r = _sigmoid(z[:, :LANES] + ba_ref[:, sl])
        gi = _sigmoid(z[:, LANES:] + bx_ref[:, sl])
        log_a = (-LRU_C) * r * sp[:, sl]
        a = jnp.exp(log_a)
        b = jnp.sqrt(-jnp.tanh(log_a) * (a * a + 1.0)) * (gi * xc[:, sl])
        a = a.reshape(groups, SUBLANES, LANES)
        b = b.reshape(groups, SUBLANES, LANES)
        for sh in (1, 2, 4):
            keep = rowmod >= sh
            a_s = jnp.where(keep, pltpu.roll(a, sh, 1), 1.0)
            b_s = jnp.where(keep, pltpu.roll(b, sh, 1), 0.0)
            b = a * b_s + b
            a = a * a_s
        carry = carry_scr[h:h + 1, :]
        outs = []
        for gidx in range(groups):
            hg = a[gidx] * carry + b[gidx]
            outs.append(hg)
            carry = hg[SUBLANES - 1:SUBLANES, :]
        carry_scr[h:h + 1, :] = carry
        hseq = jnp.concatenate(outs, axis=0)
        yb = _gelu_tanh(proj[:, sl])
        hy_scr[:, sl] = (hseq * yb).astype(BF16)

    o_ref[...] = x + _dot(hy_scr[...], wout_ref[...])


def _lru_layer(x, g, w_in, conv_w, conv_b, w_a, b_a, w_x, b_x, log_lambda, w_out, *, ts=256):
    bsz, s, d = x.shape
    w = d
    wgate = jnp.concatenate([w_a, w_x], axis=-1).astype(BF16)
    row = lambda v: v.reshape(1, -1).astype(F32)
    const2 = lambda shape: pl.BlockSpec(shape, lambda b, t: (0, 0))
    return pl.pallas_call(
        _lru_kernel,
        grid=(bsz, s // ts),
        in_specs=[
            pl.BlockSpec((None, ts, d), lambda b, t: (b, t, 0)),
            const2((1, d)),
            const2((d, 2 * w)),
            const2((CONV_WIDTH, w)),
            const2((1, w)),
            pl.BlockSpec((LRU_BLOCKS, LANES, 2 * LANES), lambda b, t: (0, 0, 0)),
            const2((1, w)),
            const2((1, w)),
            const2((1, w)),
            const2((w, d)),
        ],
        out_specs=pl.BlockSpec((None, ts, d), lambda b, t: (b, t, 0)),
        out_shape=jax.ShapeDtypeStruct(x.shape, F32),
        scratch_shapes=[
            pltpu.VMEM((ts + SUBLANES, w), F32),
            pltpu.VMEM((ts, w), BF16),
            pltpu.VMEM((LRU_BLOCKS, LANES), F32),
        ],
        compiler_params=pltpu.CompilerParams(
            dimension_semantics=("arbitrary", "arbitrary"), vmem_limit_bytes=48 * 1024 * 1024),
        name="lru_mixer",
    )(x, row(g), w_in.astype(BF16), conv_w.astype(F32), row(conv_b), wgate, row(b_a), row(b_x),
      row(log_lambda), w_out.astype(BF16))


def _fox_proj_kernel(x_ref, g_ref, w_ref, wf_ref, bf_ref, qg_ref, kg_ref, hm_ref,
                     q_ref, k_ref, v_ref, gate_ref, f_ref, carry_scr):
    t = pl.program_id(1)
    ts, d = x_ref.shape

    @pl.when(t == 0)
    def _():
        carry_scr[...] = jnp.zeros_like(carry_scr)

    hn = _rms_rows(x_ref[...], g_ref[...]).astype(BF16)
    proj = _dot(hn, w_ref[...])
    hm = hm_ref[...]

    def head_norm(z, gain):
        ms = _dot((z * z).astype(BF16), hm)
        return z * lax.rsqrt(ms + RMS_EPS) * gain

    q_ref[...] = head_norm(proj[:, :d], qg_ref[...]).astype(BF16)
    k_ref[...] = head_norm(proj[:, d:2 * d], kg_ref[...]).astype(BF16)
    v_ref[...] = proj[:, 2 * d:3 * d].astype(BF16)
    gate_ref[...] = _sigmoid(proj[:, 3 * d:]).astype(BF16)

    f = _dot(hn, wf_ref[...]) + bf_ref[...]
    lf = -_softplus(-f)
    r_i = lax.broadcasted_iota(jnp.int32, (ts, ts), 0)
    c_i = lax.broadcasted_iota(jnp.int32, (ts, ts), 1)
    ltri = jnp.where(c_i <= r_i, 1.0, 0.0).astype(BF16)
    hi, mid, lo = _split3(lf)
    fcum = (_dot(ltri, hi) + _dot(ltri, mid)) + _dot(ltri, lo) + carry_scr[...]
    carry_scr[...] = fcum[ts - 1:ts, :]
    f_ref[...] = fcum * LOG2E


def _fox_proj(x, g, w_in, b_f, q_gain, k_gain, *, ts=256):
    bsz, s, d = x.shape
    hd = d // FOX_HEADS
    w_main = w_in[:, :4 * d].astype(BF16)
    w_f = jnp.pad(w_in[:, 4 * d:], ((0, 0), (0, LANES - FOX_HEADS))).astype(BF16)
    bf_row = jnp.pad(b_f.astype(F32), (0, LANES - FOX_HEADS)).reshape(1, LANES)
    scale = hd ** -0.5
    qg_row = (jnp.tile(q_gain.astype(F32), FOX_HEADS) * (scale * LOG2E)).reshape(1, d)
    kg_row = jnp.tile(k_gain.astype(F32), FOX_HEADS).reshape(1, d)
    head_id = jnp.arange(d) // hd
    hm = jnp.where(head_id[:, None] == head_id[None, :], 1.0 / hd, 0.0).astype(BF16)
    const2 = lambda shape: pl.BlockSpec(shape, lambda b, t: (0, 0))
    tok = pl.BlockSpec((None, ts, d), lambda b, t: (b, t, 0))
    act = jax.ShapeDtypeStruct((bsz, s, d), BF16)
    return pl.pallas_call(
        _fox_proj_kernel,
        grid=(bsz, s // ts),
        in_specs=[tok, const2((1, d)), const2((d, 4 * d)), const2((d, LANES)), const2((1, LANES)),
                  const2((1, d)), const2((1, d)), const2((d, d))],
        out_specs=[tok, tok, tok, tok, pl.BlockSpec((None, ts, LANES), lambda b, t: (b, t, 0))],
        out_shape=[act, act, act, act, jax.ShapeDtypeStruct((bsz, s, LANES), F32)],
        scratch_shapes=[pltpu.VMEM((1, LANES), F32)],
        compiler_params=pltpu.CompilerParams(
            dimension_semantics=("arbitrary", "arbitrary"), vmem_limit_bytes=48 * 1024 * 1024),
        name="fox_proj",
    )(x, g.reshape(1, d).astype(F32), w_main, w_f, bf_row, qg_row, kg_row, hm)


def _aug_lanes(hd):
    base = SUBLANES * hd
    return base, base + 3


def _attn_placement(pairs):
    pq = jnp.zeros((pairs, 3, LANES, LANES), F32)
    pk = jnp.zeros((pairs, 2, 3, LANES, LANES), F32)
    for p in range(pairs):
        for hd in range(2):
            q_lo, k_lo = _aug_lanes(hd)
            for i in range(3):
                pq = pq.at[p, i, 2 * p + hd, q_lo + i].set(1.0)
                pk = pk.at[p, hd, i, 2 * p + hd, k_lo + i].set(1.0)
    return pq.astype(BF16), pk.astype(BF16)


def _place(x, p_ref, idx):
    hi, mid, lo = _split3(x)
    return (_dot(hi, p_ref[idx + (0,)]) + _dot(mid, p_ref[idx + (1,)])) + _dot(lo, p_ref[idx + (2,)])


def _fox_attn_fast_kernel(js_ref, c_ref, q_ref, k_ref, v_ref, f_ref, pq_ref, pk_ref, o_ref,
                          ka_scr, kb_scr, va_scr, vb_scr, acc_scr, *, chunk):
    qi = pl.program_id(2)
    nq = pl.num_programs(2)
    js_base = ((pl.program_id(0) * pl.num_programs(1) + pl.program_id(1)) * 2) * nq + qi
    tq = q_ref.shape[0]
    tk = tq
    s_len = k_ref.shape[0]
    half = LANES // 2
    k_scrs = (ka_scr, kb_scr)
    v_scrs = (va_scr, vb_scr)
    sum_lane = (half, half - 1)

    @pl.when(qi == 0)
    def _():
        def build(c, carry):
            rows = pl.ds(pl.multiple_of(c * chunk, chunk), chunk)
            kk = k_ref[rows, :]
            vv = v_ref[rows, :]
            lane = lax.broadcasted_iota(jnp.int32, (chunk, LANES), 1)
            nf = -f_ref[rows, :]
            for hd in range(2):
                ones_lo, _ = _aug_lanes(hd)
                aug = _place(nf, pk_ref, (hd,))
                aug = aug + jnp.where((lane >= ones_lo) & (lane < ones_lo + 3), 1.0, 0.0)
                keep = (lane < half) if hd == 0 else (lane >= half)
                k_scrs[hd][rows, 0:LANES] = jnp.where(keep, kk, jnp.zeros_like(kk))
                k_scrs[hd][rows, LANES:2 * LANES] = aug.astype(BF16)
                v_scrs[hd][rows, :] = jnp.where(lane == sum_lane[hd], jnp.ones_like(vv), vv)
            return carry
        lax.fori_loop(0, s_len // chunk, build, 0)

    lane_q = lax.broadcasted_iota(jnp.int32, (tq, LANES), 1)
    fq = f_ref[pl.ds(pl.multiple_of(qi * tq, tq), tq), :] - c_ref[...]
    augq = _place(fq, pq_ref, ())
    q_ones = ((lane_q >= 3) & (lane_q < 6)) | ((lane_q >= SUBLANES + 3) & (lane_q < SUBLANES + 6))
    augq = augq + jnp.where(q_ones, 1.0, 0.0)
    q_aug = jnp.concatenate([q_ref[...], augq.astype(BF16)], axis=1)
    acc_scr[...] = jnp.zeros_like(acc_scr)

    def probs(j, hd, masked=False):
        rows = pl.ds(pl.multiple_of(j * tk, tk), tk)
        s = lax.dot_general(q_aug, k_scrs[hd][rows, :], (((1,), (1,)), ((), ())),
                            preferred_element_type=F32)
        p = jnp.exp2(s)
        if masked:
            r_i = lax.broadcasted_iota(jnp.int32, (tq, tk), 0)
            c_i = lax.broadcasted_iota(jnp.int32, (tq, tk), 1)
            p = jnp.where(c_i <= r_i, p, 0.0)
        return p.astype(BF16), v_scrs[hd][rows, :]

    def attend(j, hd, count):
        total = None
        for u in range(count):
            pu, vu = probs(j + u, hd)
            term = _dot(pu, vu)
            total = term if total is None else total + term
        acc_scr[hd] += total

    for hd in range(2):
        j0 = js_ref[js_base + hd * nq]
        n = qi - j0
        j = j0
        arm = 1
        while arm < ATTN_BLOCKS_PER_STEP:
            bit = n & arm

            @pl.when(bit != 0)
            def _(j=j, arm=arm):
                attend(j, hd, arm)

            j = j + bit
            arm *= 2

        def body(t, carry, j=j):
            attend(j + ATTN_BLOCKS_PER_STEP * t, hd, ATTN_BLOCKS_PER_STEP)
            return carry

        lax.fori_loop(0, lax.shift_right_logical(n, ATTN_BLOCKS_PER_STEP.bit_length() - 1), body, 0)

    for hd in range(2):
        pd, vd = probs(qi, hd, masked=True)
        acc_scr[hd] += _dot(pd, vd)

    acc_a = acc_scr[0]
    acc_b = acc_scr[1]
    oa = acc_a / acc_a[:, sum_lane[0]:sum_lane[0] + 1]
    ob = acc_b / acc_b[:, sum_lane[1]:sum_lane[1] + 1]
    o_ref[...] = jnp.where(lane_q < half, oa, ob).astype(o_ref.dtype)


def _first_live_block(f_tok, tq):
    bsz, s, _ = f_tok.shape
    nq = s // tq
    f_heads = f_tok[:, :, :FOX_HEADS]
    f_first = f_heads[:, 0::tq, :]
    f_last = f_heads[:, tq - 1::tq, :]
    gap = f_first[:, :, None, :] - f_last[:, None, :, :]
    blk = jnp.arange(nq)
    dead = (gap < -ATTN_DEAD_LOG2) & (blk[None, None, :, None] < blk[None, :, None, None])
    first = jnp.sum(dead, axis=2).astype(jnp.int32)
    return jnp.transpose(first, (0, 2, 1)).reshape(-1)


def _fox_attn_fast(q, k, v, f_tok, shift, *, tq=512, chunk=1024):
    bsz, s, d = q.shape
    pairs = d // LANES
    pq, pk = _attn_placement(pairs)
    c_row = jnp.full((1, LANES), shift * LOG2E, F32)
    first_live = _first_live_block(f_tok, tq)
    return pl.pallas_call(
        functools.partial(_fox_attn_fast_kernel, chunk=chunk),
        grid_spec=pltpu.PrefetchScalarGridSpec(
            num_scalar_prefetch=1,
            grid=(bsz, pairs, s // tq),
            in_specs=[
                pl.BlockSpec((1, LANES), lambda b, p, i, js: (0, 0)),
                pl.BlockSpec((None, tq, LANES), lambda b, p, i, js: (b, i, p)),
                pl.BlockSpec((None, s, LANES), lambda b, p, i, js: (b, 0, p)),
                pl.BlockSpec((None, s, LANES), lambda b, p, i, js: (b, 0, p)),
                pl.BlockSpec((None, s, LANES), lambda b, p, i, js: (b, 0, 0)),
                pl.BlockSpec((None, 3, LANES, LANES), lambda b, p, i, js: (p, 0, 0, 0)),
                pl.BlockSpec((None, 2, 3, LANES, LANES), lambda b, p, i, js: (p, 0, 0, 0, 0)),
            ],
            out_specs=pl.BlockSpec((None, tq, LANES), lambda b, p, i, js: (b, i, p)),
            scratch_shapes=[
                pltpu.VMEM((s, 2 * LANES), BF16),
                pltpu.VMEM((s, 2 * LANES), BF16),
                pltpu.VMEM((s, LANES), BF16),
                pltpu.VMEM((s, LANES), BF16),
                pltpu.VMEM((2, tq, LANES), F32),
            ],
        ),
        out_shape=jax.ShapeDtypeStruct((bsz, s, d), BF16),
        compiler_params=pltpu.CompilerParams(
            dimension_semantics=("arbitrary", "arbitrary", "arbitrary"),
            vmem_limit_bytes=48 * 1024 * 1024),
        name="fox_attn_fast",
    )(first_live, c_row, q, k, v, f_tok, pq, pk)


def _fox_attn_online_kernel(q_ref, k_ref, v_ref, ft_ref, o_ref, ka_scr, kb_scr, m_scr, l_scr, acc_scr):
    qi = pl.program_id(2)
    tq = q_ref.shape[0]
    tk = tq
    half = LANES // 2
    lane = lax.broadcasted_iota(jnp.int32, (1, LANES), 1)

    @pl.when(qi == 0)
    def _():
        kk = k_ref[...]
        lane_k = lax.broadcasted_iota(jnp.int32, kk.shape, 1)
        zero = jnp.zeros_like(kk)
        ka_scr[...] = jnp.where(lane_k < half, kk, zero)
        kb_scr[...] = jnp.where(lane_k >= half, kk, zero)

    m_scr[...] = jnp.full_like(m_scr, NEG_BIG)
    l_scr[...] = jnp.zeros_like(l_scr)
    acc_scr[...] = jnp.zeros_like(acc_scr)
    q = q_ref[...]

    def attend(j, masked):
        start = pl.multiple_of(j * tk, tk)
        v = v_ref[pl.ds(start, tk), :]
        for hd, k_scr in ((0, ka_scr), (1, kb_scr)):
            kb = k_scr[pl.ds(start, tk), :]
            s = lax.dot_general(q, kb, (((1,), (1,)), ((), ())), preferred_element_type=F32)
            s = s - ft_ref[hd, pl.ds(j, 1), :]
            if masked:
                r_i = lax.broadcasted_iota(jnp.int32, (tq, tk), 0)
                c_i = lax.broadcasted_iota(jnp.int32, (tq, tk), 1)
                s = jnp.where(c_i <= r_i, s, NEG_BIG)
            m_prev = m_scr[hd]
            m_new = jnp.maximum(m_prev, jnp.max(s, axis=1, keepdims=True))
            p = jnp.exp2(s - m_new[:, :1])
            alpha = jnp.exp2(m_prev - m_new)
            l_scr[hd] = alpha * l_scr[hd] + jnp.sum(p, axis=1, keepdims=True)
            acc_scr[hd] = alpha * acc_scr[hd] + _dot(p.astype(BF16), v)
            m_scr[hd] = m_new

    def body(j, carry):
        attend(j, False)
        return carry

    lax.fori_loop(0, qi, body, 0)
    attend(qi, True)

    oa = acc_scr[0] / l_scr[0]
    ob = acc_scr[1] / l_scr[1]
    o_ref[...] = jnp.where(lane < half, oa, ob).astype(o_ref.dtype)


def _fox_attn_online(q, k, v, f_tok, *, tq=512):
    bsz, s, d = q.shape
    pairs = d // LANES
    nq = s // tq
    ft5 = jnp.swapaxes(f_tok[:, :, :FOX_HEADS], 1, 2).reshape(bsz, pairs, 2, nq, tq)
    return pl.pallas_call(
        _fox_attn_online_kernel,
        grid=(bsz, pairs, nq),
        in_specs=[
            pl.BlockSpec((None, tq, LANES), lambda b, p, i: (b, i, p)),
            pl.BlockSpec((None, s, LANES), lambda b, p, i: (b, 0, p)),
            pl.BlockSpec((None, s, LANES), lambda b, p, i: (b, 0, p)),
            pl.BlockSpec((None, None, 2, nq, tq), lambda b, p, i: (b, p, 0, 0, 0)),
        ],
        out_specs=pl.BlockSpec((None, tq, LANES), lambda b, p, i: (b, i, p)),
        out_shape=jax.ShapeDtypeStruct((bsz, s, d), BF16),
        scratch_shapes=[
            pltpu.VMEM((s, LANES), BF16),
            pltpu.VMEM((s, LANES), BF16),
            pltpu.VMEM((2, tq, LANES), F32),
            pltpu.VMEM((2, tq, LANES), F32),
            pltpu.VMEM((2, tq, LANES), F32),
        ],
        compiler_params=pltpu.CompilerParams(
            dimension_semantics=("arbitrary", "arbitrary", "arbitrary"),
            vmem_limit_bytes=48 * 1024 * 1024),
        name="fox_attn_online",
    )(q, k, v, ft5)


def _fox_attn(q, k, v, f_tok, q_gain, k_gain):
    hd = q.shape[-1] // FOX_HEADS
    shift = (hd ** 0.5) * jnp.max(jnp.abs(q_gain)).astype(F32) * jnp.max(jnp.abs(k_gain)).astype(F32)
    return lax.cond(shift < ATTN_SHIFT_MAX,
                    lambda: _fox_attn_fast(q, k, v, f_tok, shift),
                    lambda: _fox_attn_online(q, k, v, f_tok))


def _fox_out_kernel(x_ref, o_ref, gate_ref, w_ref, out_ref):
    og = (o_ref[...].astype(F32) * gate_ref[...].astype(F32)).astype(BF16)
    out_ref[...] = x_ref[...] + _dot(og, w_ref[...])


def _fox_out(x2, o2, gate2, w_out, *, tm=512):
    t, d = x2.shape
    tok = pl.BlockSpec((tm, d), lambda i: (i, 0))
    return pl.pallas_call(
        _fox_out_kernel,
        grid=(t // tm,),
        in_specs=[tok, tok, tok, pl.BlockSpec((d, d), lambda i: (0, 0))],
        out_specs=tok,
        out_shape=jax.ShapeDtypeStruct((t, d), F32),
        compiler_params=pltpu.CompilerParams(dimension_semantics=("arbitrary",)),
        name="fox_out",
    )(x2, o2, gate2, w_out.astype(BF16))


FFN_CHUNK = 512


def _swiglu_rows(hn, wg_ref, wu_ref, wd_ref, acc_scr):
    dff = wg_ref.shape[-1]
    for c in range(dff // FFN_CHUNK):
        cols = slice(c * FFN_CHUNK, (c + 1) * FFN_CHUNK)
        gg = _dot(hn, wg_ref[:, cols])
        uu = _dot(hn, wu_ref[:, cols])
        act = (gg * _sigmoid(gg)) * uu
        acc_scr[...] += _dot(act.astype(BF16), wd_ref[cols, :])


def _ffn_kernel(x_ref, g_ref, wg_ref, wu_ref, wd_ref, o_ref, acc_scr):
    x = x_ref[...]
    acc_scr[...] = x
    _swiglu_rows(_rms_rows(x, g_ref[...]).astype(BF16), wg_ref, wu_ref, wd_ref, acc_scr)
    o_ref[...] = acc_scr[...]


def _resident(shape, index_map):
    return pl.BlockSpec(shape, index_map, pipeline_mode=pl.Buffered(1))


def _ffn_layer(x2, g, w_gate, w_up, w_down, *, tm=512):
    t, d = x2.shape
    dff = w_gate.shape[1]
    tok = pl.BlockSpec((tm, d), lambda i: (i, 0))
    return pl.pallas_call(
        _ffn_kernel,
        grid=(t // tm,),
        in_specs=[tok, pl.BlockSpec((1, d), lambda i: (0, 0)),
                  _resident((d, dff), lambda i: (0, 0)),
                  _resident((d, dff), lambda i: (0, 0)),
                  _resident((dff, d), lambda i: (0, 0))],
        out_specs=tok,
        out_shape=jax.ShapeDtypeStruct((t, d), F32),
        scratch_shapes=[pltpu.VMEM((tm, d), F32)],
        compiler_params=pltpu.CompilerParams(
            dimension_semantics=("arbitrary",), vmem_limit_bytes=48 * 1024 * 1024),
        name="dense_ffn",
    )(x2, g.reshape(1, d).astype(F32), w_gate.astype(BF16), w_up.astype(BF16), w_down.astype(BF16))


def _top2(logits, lane):
    m1 = jnp.max(logits, axis=1, keepdims=True)
    i1 = jnp.min(jnp.where(logits == m1, lane, LANES), axis=1, keepdims=True)
    rest = jnp.where(lane == i1, NEG_BIG, logits)
    m2 = jnp.max(rest, axis=1, keepdims=True)
    i2 = jnp.min(jnp.where(rest == m2, lane, LANES), axis=1, keepdims=True)
    return m1, i1, m2, i2


def _router_logits(hn, rhi_ref, rlo_ref):
    h_hi = hn.astype(BF16)
    h_lo = (hn - h_hi.astype(F32)).astype(BF16)
    logits = (_dot(h_hi, rhi_ref[...]) + _dot(h_lo, rhi_ref[...])) + _dot(h_hi, rlo_ref[...])
    lane = lax.broadcasted_iota(jnp.int32, logits.shape, 1)
    return jnp.where(lane < N_EXPERTS, logits, NEG_BIG), lane


def _moe_route_kernel(x_ref, g_ref, rhi_ref, rlo_ref, idx_ref, w_ref, cnt_ref, carry_scr):
    @pl.when(pl.program_id(0) == 0)
    def _():
        carry_scr[...] = jnp.zeros_like(carry_scr)

    tm = x_ref.shape[0]
    hn = _rms_rows(x_ref[...], g_ref[...])
    logits, lane = _router_logits(hn, rhi_ref, rlo_ref)
    m1, i1, m2, i2 = _top2(logits, lane)
    e2 = jnp.exp(m2 - m1)
    w1 = 1.0 / (1.0 + e2)
    w2 = e2 / (1.0 + e2)
    oh1 = lane == i1
    oh2 = lane == i2
    member = jnp.where(oh1 | oh2, 1.0, 0.0)
    r_i = lax.broadcasted_iota(jnp.int32, (tm, tm), 0)
    c_i = lax.broadcasted_iota(jnp.int32, (tm, tm), 1)
    lstrict = jnp.where(c_i < r_i, 1.0, 0.0).astype(BF16)
    before = _dot(lstrict, member.astype(BF16)) + carry_scr[...]
    rank1 = jnp.sum(jnp.where(oh1, before, 0.0), axis=1, keepdims=True)
    rank2 = jnp.sum(jnp.where(oh2, before, 0.0), axis=1, keepdims=True)
    carry_scr[...] += jnp.sum(member, axis=0, keepdims=True)
    cnt_ref[...] = carry_scr[...]
    w_ref[...] = jnp.where(lane == 0, w1, jnp.where(lane == 1, w2, 0.0))
    cols = (i1.astype(F32), i2.astype(F32), rank1, rank2)
    packed = jnp.zeros((tm, LANES), F32)
    for c, col in enumerate(cols):
        packed = jnp.where(lane == c, col, packed)
    idx_ref[...] = packed.T[:SUBLANES, :].astype(jnp.int32)


def _moe_dispatch_kernel(meta_ref, d_ref, x_ref, xs_ref, zero_scr, sem, zsem, *, n_e):
    tm = x_ref.shape[0]

    for t in range(tm):
        for slot in range(2):
            row = d_ref[slot, t]
            pltpu.make_async_copy(x_ref.at[pl.ds(t, 1), :], xs_ref.at[pl.ds(row, 1), :], sem).start(priority=slot)
    for slot in range(2):
        pltpu.make_async_copy(x_ref, xs_ref.at[pl.ds(0, tm), :], sem).wait()

    @pl.when(pl.program_id(0) == pl.num_programs(0) - 1)
    def _():
        zero_scr[...] = jnp.zeros_like(zero_scr)
        zero_row = zero_scr.at[pl.ds(0, 1), :]
        for e in range(n_e):
            start = meta_ref[e]
            n_pad = meta_ref[n_e + e]

            def fill(r, carry):
                pltpu.make_async_copy(zero_row, xs_ref.at[pl.ds(start + r, 1), :], zsem).start()
                return carry

            def drain(r, carry):
                pltpu.make_async_copy(zero_row, xs_ref.at[pl.ds(0, 1), :], zsem).wait()
                return carry

            lax.fori_loop(0, n_pad, fill, 0)
            lax.fori_loop(0, n_pad, drain, 0)
        used = meta_ref[2 * n_e]

        def fill_tile(g, carry):
            row0 = pl.multiple_of((used + g) * tm, tm)
            cp = pltpu.make_async_copy(zero_scr, xs_ref.at[pl.ds(row0, tm), :], zsem)
            cp.start()
            cp.wait()
            return carry

        lax.fori_loop(0, meta_ref[2 * n_e + 1], fill_tile, 0)


def _moe_group_kernel(exp_ref, rows_ref, xs_ref, g_ref, wg_ref, wu_ref, wd_ref, ys_ref, acc_scr):
    live = rows_ref[pl.program_id(0)] > 0

    @pl.when(live)
    def _():
        acc_scr[...] = jnp.zeros_like(acc_scr)
        _swiglu_rows(_rms_rows(xs_ref[...], g_ref[...]).astype(BF16), wg_ref, wu_ref, wd_ref, acc_scr)
        ys_ref[...] = acc_scr[...]

    @pl.when(jnp.logical_not(live))
    def _():
        ys_ref[...] = jnp.zeros_like(ys_ref)


def _moe_combine_kernel(d_ref, x_ref, w_ref, ys_ref, o_ref, buf, sem):
    tm = x_ref.shape[0]

    for t in range(tm):
        for slot in range(2):
            row = d_ref[slot, t]
            pltpu.make_async_copy(ys_ref.at[pl.ds(row, 1), :], buf.at[slot, pl.ds(t, 1), :],
                                  sem).start(priority=slot)
    for slot in range(2):
        pltpu.make_async_copy(ys_ref.at[pl.ds(0, tm), :], buf.at[slot], sem).wait()
    w = w_ref[...]
    o_ref[...] = x_ref[...] + w[:, 0:1] * buf[0] + w[:, 1:2] * buf[1]


def _moe_layer(x2, g, w_gate, w_up, w_down, router, *, tm=512):
    t, d = x2.shape
    n_e, _, dff = w_gate.shape
    nt = t // tm
    n_tiles = (2 * t) // tm + n_e
    g_row = g.reshape(1, d).astype(F32)
    r_pad = jnp.pad(router.astype(F32), ((0, 0), (0, LANES - n_e)))
    r_hi = r_pad.astype(BF16)
    r_lo = (r_pad - r_hi.astype(F32)).astype(BF16)

    idx, wts, counts = pl.pallas_call(
        _moe_route_kernel,
        grid=(nt,),
        in_specs=[pl.BlockSpec((tm, d), lambda i: (i, 0)), pl.BlockSpec((1, d), lambda i: (0, 0)),
                  pl.BlockSpec((d, LANES), lambda i: (0, 0)), pl.BlockSpec((d, LANES), lambda i: (0, 0))],
        out_specs=[pl.BlockSpec((None, SUBLANES, tm), lambda i: (i, 0, 0)),
                   pl.BlockSpec((tm, LANES), lambda i: (i, 0)),
                   pl.BlockSpec((1, LANES), lambda i: (0, 0))],
        out_shape=[jax.ShapeDtypeStruct((nt, SUBLANES, tm), jnp.int32),
                   jax.ShapeDtypeStruct((t, LANES), F32),
                   jax.ShapeDtypeStruct((1, LANES), F32)],
        scratch_shapes=[pltpu.VMEM((1, LANES), F32)],
        compiler_params=pltpu.CompilerParams(dimension_semantics=("arbitrary",)),
        name="moe_route",
    )(x2, g_row, r_hi, r_lo)

    cnt = counts[0, :n_e].astype(jnp.int32)
    tiles_e = (cnt + tm - 1) // tm
    ends = jnp.cumsum(tiles_e)
    starts = ends - tiles_e
    used = ends[-1]
    dest = starts[idx[:, 0:2, :]] * tm + idx[:, 2:4, :]
    dest = jnp.concatenate([dest, jnp.zeros((nt, SUBLANES - 2, tm), jnp.int32)], axis=1)
    tile_ids = jnp.arange(n_tiles, dtype=jnp.int32)
    t_exp = jnp.minimum(jnp.searchsorted(ends, tile_ids, side="right"), n_e - 1).astype(jnp.int32)
    t_rows = jnp.where(tile_ids < used,
                       jnp.clip(cnt[t_exp] - (tile_ids - starts[t_exp]) * tm, 0, tm), 0).astype(jnp.int32)
    meta = jnp.concatenate([starts * tm + cnt, tiles_e * tm - cnt,
                            jnp.stack([used, n_tiles - used])]).astype(jnp.int32)

    smem_rows = lambda: pl.BlockSpec((None, SUBLANES, tm), lambda i, *_: (i, 0, 0), memory_space=pltpu.SMEM)
    xs = pl.pallas_call(
        functools.partial(_moe_dispatch_kernel, n_e=n_e),
        grid_spec=pltpu.PrefetchScalarGridSpec(
            num_scalar_prefetch=1,
            grid=(nt,),
            in_specs=[smem_rows(), pl.BlockSpec((tm, d), lambda i, meta: (i, 0))],
            out_specs=pl.BlockSpec(memory_space=pl.ANY),
            scratch_shapes=[pltpu.VMEM((tm, d), F32), pltpu.SemaphoreType.DMA(()),
                            pltpu.SemaphoreType.DMA(())],
        ),
        out_shape=jax.ShapeDtypeStruct((n_tiles * tm, d), F32),
        compiler_params=pltpu.CompilerParams(dimension_semantics=("arbitrary",)),
        name="moe_dispatch",
    )(meta, dest, x2)

    ys = pl.pallas_call(
        _moe_group_kernel,
        grid_spec=pltpu.PrefetchScalarGridSpec(
            num_scalar_prefetch=2,
            grid=(n_tiles,),
            in_specs=[
                pl.BlockSpec((tm, d), lambda i, ex, rows: (i, 0)),
                pl.BlockSpec((1, d), lambda i, ex, rows: (0, 0)),
                _resident((None, d, dff), lambda i, ex, rows: (ex[i], 0, 0)),
                _resident((None, d, dff), lambda i, ex, rows: (ex[i], 0, 0)),
                _resident((None, dff, d), lambda i, ex, rows: (ex[i], 0, 0)),
            ],
            out_specs=pl.BlockSpec((tm, d), lambda i, ex, rows: (i, 0)),
            scratch_shapes=[pltpu.VMEM((tm, d), F32)],
        ),
        out_shape=jax.ShapeDtypeStruct((n_tiles * tm, d), F32),
        compiler_params=pltpu.CompilerParams(
            dimension_semantics=("arbitrary",), vmem_limit_bytes=48 * 1024 * 1024),
        name="moe_group",
    )(t_exp, t_rows, xs, g_row, w_gate.astype(BF16), w_up.astype(BF16), w_down.astype(BF16))

    return pl.pallas_call(
        _moe_combine_kernel,
        grid=(nt,),
        in_specs=[smem_rows(), pl.BlockSpec((tm, d), lambda i: (i, 0)),
                  pl.BlockSpec((tm, LANES), lambda i: (i, 0)), pl.BlockSpec(memory_space=pl.ANY)],
        out_specs=pl.BlockSpec((tm, d), lambda i: (i, 0)),
        out_shape=jax.ShapeDtypeStruct((t, d), F32),
        scratch_shapes=[pltpu.VMEM((2, tm, d), F32), pltpu.SemaphoreType.DMA(())],
        compiler_params=pltpu.CompilerParams(dimension_semantics=("arbitrary",)),
        name="moe_combine",
    )(dest, x2, wts, ys)


def kernel(x, norm_mix, norm_ffn, lru_w_in, lru_conv_w, lru_conv_b, lru_w_a, lru_b_a, lru_w_x, lru_b_x,
           lru_log_lambda, lru_w_out, fox_w_in, fox_b_f, fox_q_gain, fox_k_gain, fox_w_out, ffn_w_gate,
           ffn_w_up, ffn_w_down, moe_router, moe_w_gate, moe_w_up, moe_w_down):
    bsz, s, d = x.shape
    depth = norm_mix.shape[0]
    for i in range(depth):
        j = i // 2
        if i % 2 == 0:
            x = _lru_layer(x, norm_mix[i], lru_w_in[j], lru_conv_w[j], lru_conv_b[j], lru_w_a[j],
                           lru_b_a[j], lru_w_x[j], lru_b_x[j], lru_log_lambda[j], lru_w_out[j])
            x2 = _ffn_layer(x.reshape(bsz * s, d), norm_ffn[i], ffn_w_gate[j], ffn_w_up[j], ffn_w_down[j])
        else:
            q, k, v, gate, ft = _fox_proj(x, norm_mix[i], fox_w_in[j], fox_b_f[j], fox_q_gain[j],
                                          fox_k_gain[j])
            o = _fox_attn(q, k, v, ft, fox_q_gain[j], fox_k_gain[j])
            x2 = _fox_out(x.reshape(bsz * s, d), o.reshape(bsz * s, d), gate.reshape(bsz * s, d),
                          fox_w_out[j])
            x2 = _moe_layer(x2, norm_ffn[i], moe_w_gate[j], moe_w_up[j], moe_w_down[j], moe_router[j])
        x = x2.reshape(bsz, s, d)
    return x
```

```python
import functools

import jax
import jax.numpy as jnp
from jax import lax
from jax.experimental import pallas as pl
from jax.experimental.pallas import tpu as pltpu

RMS_EPS = 1e-6
LRU_C = 8.0
LRU_BLOCKS = 8
CONV_WIDTH = 4
FOX_HEADS = 16
N_EXPERTS = 8
LANES = 128
SUBLANES = 8
NEG_BIG = -1e30
LOG2E = 1.4426950408889634
ATTN_SHIFT_MAX = 40.0
ATTN_DEAD_LOG2 = 160.0
ATTN_BLOCKS_PER_STEP = 8

F32 = jnp.float32
BF16 = jnp.bfloat16


def _dot(a, b):
    return jnp.dot(a, b, preferred_element_type=F32)


def _rms_rows(x, g):
    ms = jnp.mean(x * x, axis=-1, keepdims=True)
    return x * lax.rsqrt(ms + RMS_EPS) * g


def _sigmoid(x):
    return 1.0 / (1.0 + jnp.exp(-x))


def _softplus(x):
    return jnp.maximum(x, 0.0) + jnp.log1p(jnp.exp(-jnp.abs(x)))


def _gelu_tanh(x):
    return 0.5 * x * (1.0 + jnp.tanh(0.7978845608028654 * (x + 0.044715 * (x * x * x))))


def _split3(x):
    hi = x.astype(BF16)
    r1 = x - hi.astype(F32)
    mid = r1.astype(BF16)
    lo = (r1 - mid.astype(F32)).astype(BF16)
    return hi, mid, lo


def _lru_kernel(x_ref, g_ref, win_ref, cw_ref, cb_ref, wgate_ref, ba_ref, bx_ref, lam_ref, wout_ref,
                o_ref, xb_scr, hy_scr, carry_scr):
    t = pl.program_id(1)
    ts, d = x_ref.shape
    w = d

    @pl.when(t == 0)
    def _():
        xb_scr[0:SUBLANES, :] = jnp.zeros((SUBLANES, w), F32)
        carry_scr[...] = jnp.zeros_like(carry_scr)

    x = x_ref[...]
    hn = _rms_rows(x, g_ref[...]).astype(BF16)
    proj = _dot(hn, win_ref[...])
    xb = proj[:, w:]
    xb_scr[SUBLANES:SUBLANES + ts, :] = xb
    xc = cw_ref[3:4, :] * xb + cb_ref[...]
    for k in range(1, CONV_WIDTH):
        xc = xc + cw_ref[3 - k:4 - k, :] * xb_scr[SUBLANES - k:SUBLANES - k + ts, :]
    xb_scr[0:SUBLANES, :] = xb[ts - SUBLANES:, :]
    xcb = xc.astype(BF16)

    sp = _softplus(-lam_ref[...])
    groups = ts // SUBLANES
    rowmod = lax.broadcasted_iota(jnp.int32, (groups, SUBLANES, LANES), 1)
    for h in range(LRU_BLOCKS):
        sl = slice(h * LANES, (h + 1) * LANES)
        z = _dot(xcb[:, sl], wgate_ref[h])
        r = _sigmoid(z[:, :LANES] + ba_ref[:, sl])
        gi = _sigmoid(z[:, LANES:] + bx_ref[:, sl])
        log_a = (-LRU_C) * r * sp[:, sl]
        a = jnp.exp(log_a)
        b = jnp.sqrt(-jnp.tanh(log_a) * (a * a + 1.0)) * (gi * xc[:, sl])
        a = a.reshape(groups, SUBLANES, LANES)
        b = b.reshape(groups, SUBLANES, LANES)
        for sh in (1, 2, 4):
            keep = rowmod >= sh
            a_s = jnp.where(keep, pltpu.roll(a, sh, 1), 1.0)
            b_s = jnp.where(keep, pltpu.roll(b, sh, 1), 0.0)
            b = a * b_s + b
            a = a * a_s
        carry = carry_scr[h:h + 1, :]
        outs = []
        for gidx in range(groups):
            hg = a[gidx] * carry + b[gidx]
            outs.append(hg)
            carry = hg[SUBLANES - 1:SUBLANES, :]
        carry_scr[h:h + 1, :] = carry
        hseq = jnp.concatenate(outs, axis=0)
        yb = _gelu_tanh(proj[:, sl])
        hy_scr[:, sl] = (hseq * yb).astype(BF16)

    o_ref[...] = x + _dot(hy_scr[...], wout_ref[...])


def _lru_layer(x, g, w_in, conv_w, conv_b, w_a, b_a, w_x, b_x, log_lambda, w_out, *, ts=256):
    bsz, s, d = x.shape
    w = d
    wgate = jnp.concatenate([w_a, w_x], axis=-1).astype(BF16)
    row = lambda v: v.reshape(1, -1).astype(F32)
    const2 = lambda shape: pl.BlockSpec(shape, lambda b, t: (0, 0))
    return pl.pallas_call(
        _lru_kernel,
        grid=(bsz, s // ts),
        in_specs=[
            pl.BlockSpec((None, ts, d), lambda b, t: (b, t, 0)),
            const2((1, d)),
            const2((d, 2 * w)),
            const2((CONV_WIDTH, w)),
            const2((1, w)),
            pl.BlockSpec((LRU_BLOCKS, LANES, 2 * LANES), lambda b, t: (0, 0, 0)),
            const2((1, w)),
            const2((1, w)),
            const2((1, w)),
            const2((w, d)),
        ],
        out_specs=pl.BlockSpec((None, ts, d), lambda b, t: (b, t, 0)),
        out_shape=jax.ShapeDtypeStruct(x.shape, F32),
        scratch_shapes=[
            pltpu.VMEM((ts + SUBLANES, w), F32),
            pltpu.VMEM((ts, w), BF16),
            pltpu.VMEM((LRU_BLOCKS, LANES), F32),
        ],
        compiler_params=pltpu.CompilerParams(
            dimension_semantics=("arbitrary", "arbitrary"), vmem_limit_bytes=48 * 1024 * 1024),
        name="lru_mixer",
    )(x, row(g), w_in.astype(BF16), conv_w.astype(F32), row(conv_b), wgate, row(b_a), row(b_x),
      row(log_lambda), w_out.astype(BF16))


def _fox_proj_kernel(x_ref, g_ref, w_ref, wf_ref, bf_ref, qg_ref, kg_ref, hm_ref,
                     q_ref, k_ref, v_ref, gate_ref, f_ref, carry_scr):
    t = pl.program_id(1)
    ts, d = x_ref.shape

    @pl.when(t == 0)
    def _():
        carry_scr[...] = jnp.zeros_like(carry_scr)

    hn = _rms_rows(x_ref[...], g_ref[...]).astype(BF16)
    proj = _dot(hn, w_ref[...])
    hm = hm_ref[...]

    def head_norm(z, gain):
        ms = _dot((z * z).astype(BF16), hm)
        return z * lax.rsqrt(ms + RMS_EPS) * gain

    q_ref[...] = head_norm(proj[:, :d], qg_ref[...]).astype(BF16)
    k_ref[...] = head_norm(proj[:, d:2 * d], kg_ref[...]).astype(BF16)
    v_ref[...] = proj[:, 2 * d:3 * d].astype(BF16)
    gate_ref[...] = _sigmoid(proj[:, 3 * d:]).astype(BF16)

    f = _dot(hn, wf_ref[...]) + bf_ref[...]
    lf = -_softplus(-f)
    r_i = lax.broadcasted_iota(jnp.int32, (ts, ts), 0)
    c_i = lax.broadcasted_iota(jnp.int32, (ts, ts), 1)
    ltri = jnp.where(c_i <= r_i, 1.0, 0.0).astype(BF16)
    hi, mid, lo = _split3(lf)
    fcum = (_dot(ltri, hi) + _dot(ltri, mid)) + _dot(ltri, lo) + carry_scr[...]
    carry_scr[...] = fcum[ts - 1:ts, :]
    f_ref[...] = fcum * LOG2E


def _fox_proj(x, g, w_in, b_f, q_gain, k_gain, *, ts=512):
    bsz, s, d = x.shape
    hd = d // FOX_HEADS
    w_main = w_in[:, :4 * d].astype(BF16)
    w_f = jnp.pad(w_in[:, 4 * d:], ((0, 0), (0, LANES - FOX_HEADS))).astype(BF16)
    bf_row = jnp.pad(b_f.astype(F32), (0, LANES - FOX_HEADS)).reshape(1, LANES)
    scale = hd ** -0.5
    qg_row = (jnp.tile(q_gain.astype(F32), FOX_HEADS) * (scale * LOG2E)).reshape(1, d)
    kg_row = jnp.tile(k_gain.astype(F32), FOX_HEADS).reshape(1, d)
    head_id = jnp.arange(d) // hd
    hm = jnp.where(head_id[:, None] == head_id[None, :], 1.0 / hd, 0.0).astype(BF16)
    const2 = lambda shape: pl.BlockSpec(shape, lambda b, t: (0, 0))
    tok = pl.BlockSpec((None, ts, d), lambda b, t: (b, t, 0))
    act = jax.ShapeDtypeStruct((bsz, s, d), BF16)
    return pl.pallas_call(
        _fox_proj_kernel,
        grid=(bsz, s // ts),
        in_specs=[tok, const2((1, d)), const2((d, 4 * d)), const2((d, LANES)), const2((1, LANES)),
                  const2((1, d)), const2((1, d)), const2((d, d))],
        out_specs=[tok, tok, tok, tok, pl.BlockSpec((None, ts, LANES), lambda b, t: (b, t, 0))],
        out_shape=[act, act, act, act, jax.ShapeDtypeStruct((bsz, s, LANES), F32)],
        scratch_shapes=[pltpu.VMEM((1, LANES), F32)],
        compiler_params=pltpu.CompilerParams(
            dimension_semantics=("arbitrary", "arbitrary"), vmem_limit_bytes=48 * 1024 * 1024),
        name="fox_proj",
    )(x, g.reshape(1, d).astype(F32), w_main, w_f, bf_row, qg_row, kg_row, hm)


def _aug_lanes(hd):
    base = SUBLANES * hd
    return base, base + 3


def _attn_placement(pairs):
    pq = jnp.zeros((pairs, 3, LANES, LANES), F32)
    pk = jnp.zeros((pairs, 2, 3, LANES, LANES), F32)
    for p in range(pairs):
        for hd in range(2):
            q_lo, k_lo = _aug_lanes(hd)
            for i in range(3):
                pq = pq.at[p, i, 2 * p + hd, q_lo + i].set(1.0)
                pk = pk.at[p, hd, i, 2 * p + hd, k_lo + i].set(1.0)
    return pq.astype(BF16), pk.astype(BF16)


def _place(x, p_ref, idx):
    hi, mid, lo = _split3(x)
    return (_dot(hi, p_ref[idx + (0,)]) + _dot(mid, p_ref[idx + (1,)])) + _dot(lo, p_ref[idx + (2,)])


def _fox_attn_fast_kernel(js_ref, c_ref, q_ref, k_ref, v_ref, f_ref, pq_ref, pk_ref, o_ref,
                          ka_scr, kb_scr, va_scr, vb_scr, acc_scr, *, chunk):
    qi = pl.program_id(2)
    nq = pl.num_programs(2)
    js_base = ((pl.program_id(0) * pl.num_programs(1) + pl.program_id(1)) * 2) * nq + qi
    tq = q_ref.shape[0]
    tk = tq
    s_len = k_ref.shape[0]
    half = LANES // 2
    k_scrs = (ka_scr, kb_scr)
    v_scrs = (va_scr, vb_scr)
    sum_lane = (half, half - 1)

    @pl.when(qi == 0)
    def _():
        def build(c, carry):
            rows = pl.ds(pl.multiple_of(c * chunk, chunk), chunk)
            kk = k_ref[rows, :]
            vv = v_ref[rows, :]
            lane = lax.broadcasted_iota(jnp.int32, (chunk, LANES), 1)
            nf = -f_ref[rows, :]
            for hd in range(2):
                ones_lo, _ = _aug_lanes(hd)
                aug = _place(nf, pk_ref, (hd,))
                aug = aug + jnp.where((lane >= ones_lo) & (lane < ones_lo + 3), 1.0, 0.0)
                keep = (lane < half) if hd == 0 else (lane >= half)
                k_scrs[hd][rows, 0:LANES] = jnp.where(keep, kk, jnp.zeros_like(kk))
                k_scrs[hd][rows, LANES:2 * LANES] = aug.astype(BF16)
                v_scrs[hd][rows, :] = jnp.where(lane == sum_lane[hd], jnp.ones_like(vv), vv)
            return carry
        lax.fori_loop(0, s_len // chunk, build, 0)

    lane_q = lax.broadcasted_iota(jnp.int32, (tq, LANES), 1)
    fq = f_ref[pl.ds(pl.multiple_of(qi * tq, tq), tq), :] - c_ref[...]
    augq = _place(fq, pq_ref, ())
    q_ones = ((lane_q >= 3) & (lane_q < 6)) | ((lane_q >= SUBLANES + 3) & (lane_q < SUBLANES + 6))
    augq = augq + jnp.where(q_ones, 1.0, 0.0)
    q_aug = jnp.concatenate([q_ref[...], augq.astype(BF16)], axis=1)
    acc_scr[...] = jnp.zeros_like(acc_scr)

    def probs(j, hd, masked=False):
        rows = pl.ds(pl.multiple_of(j * tk, tk), tk)
        s = lax.dot_general(q_aug, k_scrs[hd][rows, :], (((1,), (1,)), ((), ())),
                            preferred_element_type=F32)
        p = jnp.exp2(s)
        if masked:
            r_i = lax.broadcasted_iota(jnp.int32, (tq, tk), 0)
            c_i = lax.broadcasted_iota(jnp.int32, (tq, tk), 1)
            p = jnp.where(c_i <= r_i, p, 0.0)
        return p.astype(BF16), v_scrs[hd][rows, :]

    def attend(j, hd, count):
        total = None
        for u in range(count):
            pu, vu = probs(j + u, hd)
            term = _dot(pu, vu)
            total = term if total is None else total + term
        acc_scr[hd] += total

    for hd in range(2):
        j0 = js_ref[js_base + hd * nq]
        n = qi - j0
        j = j0
        arm = 1
        while arm < ATTN_BLOCKS_PER_STEP:
            bit = n & arm

            @pl.when(bit != 0)
            def _(j=j, arm=arm):
                attend(j, hd, arm)

            j = j + bit
            arm *= 2

        def body(t, carry, j=j):
            attend(j + ATTN_BLOCKS_PER_STEP * t, hd, ATTN_BLOCKS_PER_STEP)
            return carry

        lax.fori_loop(0, lax.shift_right_logical(n, ATTN_BLOCKS_PER_STEP.bit_length() - 1), body, 0)

    for hd in range(2):
        pd, vd = probs(qi, hd, masked=True)
        acc_scr[hd] += _dot(pd, vd)

    acc_a = acc_scr[0]
    acc_b = acc_scr[1]
    oa = acc_a / acc_a[:, sum_lane[0]:sum_lane[0] + 1]
    ob = acc_b / acc_b[:, sum_lane[1]:sum_lane[1] + 1]
    o_ref[...] = jnp.where(lane_q < half, oa, ob).astype(o_ref.dtype)


def _first_live_block(f_tok, tq):
    bsz, s, _ = f_tok.shape
    nq = s // tq
    f_heads = f_tok[:, :, :FOX_HEADS]
    f_first = f_heads[:, 0::tq, :]
    f_last = f_heads[:, tq - 1::tq, :]
    gap = f_first[:, :, None, :] - f_last[:, None, :, :]
    blk = jnp.arange(nq)
    dead = (gap < -ATTN_DEAD_LOG2) & (blk[None, None, :, None] < blk[None, :, None, None])
    first = jnp.sum(dead, axis=2).astype(jnp.int32)
    return jnp.transpose(first, (0, 2, 1)).reshape(-1)


def _fox_attn_fast(q, k, v, f_tok, shift, *, tq=512, chunk=1024):
    bsz, s, d = q.shape
    pairs = d // LANES
    pq, pk = _attn_placement(pairs)
    c_row = jnp.full((1, LANES), shift * LOG2E, F32)
    first_live = _first_live_block(f_tok, tq)
    return pl.pallas_call(
        functools.partial(_fox_attn_fast_kernel, chunk=chunk),
        grid_spec=pltpu.PrefetchScalarGridSpec(
            num_scalar_prefetch=1,
            grid=(bsz, pairs, s // tq),
            in_specs=[
                pl.BlockSpec((1, LANES), lambda b, p, i, js: (0, 0)),
                pl.BlockSpec((None, tq, LANES), lambda b, p, i, js: (b, i, p)),
                pl.BlockSpec((None, s, LANES), lambda b, p, i, js: (b, 0, p)),
                pl.BlockSpec((None, s, LANES), lambda b, p, i, js: (b, 0, p)),
                pl.BlockSpec((None, s, LANES), lambda b, p, i, js: (b, 0, 0)),
                pl.BlockSpec((None, 3, LANES, LANES), lambda b, p, i, js: (p, 0, 0, 0)),
                pl.BlockSpec((None, 2, 3, LANES, LANES), lambda b, p, i, js: (p, 0, 0, 0, 0)),
            ],
            out_specs=pl.BlockSpec((None, tq, LANES), lambda b, p, i, js: (b, i, p)),
            scratch_shapes=[
                pltpu.VMEM((s, 2 * LANES), BF16),
                pltpu.VMEM((s, 2 * LANES), BF16),
                pltpu.VMEM((s, LANES), BF16),
                pltpu.VMEM((s, LANES), BF16),
                pltpu.VMEM((2, tq, LANES), F32),
            ],
        ),
        out_shape=jax.ShapeDtypeStruct((bsz, s, d), BF16),
        compiler_params=pltpu.CompilerParams(
            dimension_semantics=("arbitrary", "arbitrary", "arbitrary"),
            vmem_limit_bytes=48 * 1024 * 1024),
        name="fox_attn_fast",
    )(first_live, c_row, q, k, v, f_tok, pq, pk)


def _fox_attn_online_kernel(q_ref, k_ref, v_ref, ft_ref, o_ref, ka_scr, kb_scr, m_scr, l_scr, acc_scr):
    qi = pl.program_id(2)
    tq = q_ref.shape[0]
    tk = tq
    half = LANES // 2
    lane = lax.broadcasted_iota(jnp.int32, (1, LANES), 1)

    @pl.when(qi == 0)
    def _():
        kk = k_ref[...]
        lane_k = lax.broadcasted_iota(jnp.int32, kk.shape, 1)
        zero = jnp.zeros_like(kk)
        ka_scr[...] = jnp.where(lane_k < half, kk, zero)
        kb_scr[...] = jnp.where(lane_k >= half, kk, zero)

    m_scr[...] = jnp.full_like(m_scr, NEG_BIG)
    l_scr[...] = jnp.zeros_like(l_scr)
    acc_scr[...] = jnp.zeros_like(acc_scr)
    q = q_ref[...]

    def attend(j, masked):
        start = pl.multiple_of(j * tk, tk)
        v = v_ref[pl.ds(start, tk), :]
        for hd, k_scr in ((0, ka_scr), (1, kb_scr)):
            kb = k_scr[pl.ds(start, tk), :]
            s = lax.dot_general(q, kb, (((1,), (1,)), ((), ())), preferred_element_type=F32)
            s = s - ft_ref[hd, pl.ds(j, 1), :]
            if masked:
                r_i = lax.broadcasted_iota(jnp.int32, (tq, tk), 0)
                c_i = lax.broadcasted_iota(jnp.int32, (tq, tk), 1)
                s = jnp.where(c_i <= r_i, s, NEG_BIG)
            m_prev = m_scr[hd]
            m_new = jnp.maximum(m_prev, jnp.max(s, axis=1, keepdims=True))
            p = jnp.exp2(s - m_new[:, :1])
            alpha = jnp.exp2(m_prev - m_new)
            l_scr[hd] = alpha * l_scr[hd] + jnp.sum(p, axis=1, keepdims=True)
            acc_scr[hd] = alpha * acc_scr[hd] + _dot(p.astype(BF16), v)
            m_scr[hd] = m_new

    def body(j, carry):
        attend(j, False)
        return carry

    lax.fori_loop(0, qi, body, 0)
    attend(qi, True)

    oa = acc_scr[0] / l_scr[0]
    ob = acc_scr[1] / l_scr[1]
    o_ref[...] = jnp.where(lane < half, oa, ob).astype(o_ref.dtype)


def _fox_attn_online(q, k, v, f_tok, *, tq=512):
    bsz, s, d = q.shape
    pairs = d // LANES
    nq = s // tq
    ft5 = jnp.swapaxes(f_tok[:, :, :FOX_HEADS], 1, 2).reshape(bsz, pairs, 2, nq, tq)
    return pl.pallas_call(
        _fox_attn_online_kernel,
        grid=(bsz, pairs, nq),
        in_specs=[
            pl.BlockSpec((None, tq, LANES), lambda b, p, i: (b, i, p)),
            pl.BlockSpec((None, s, LANES), lambda b, p, i: (b, 0, p)),
            pl.BlockSpec((None, s, LANES), lambda b, p, i: (b, 0, p)),
            pl.BlockSpec((None, None, 2, nq, tq), lambda b, p, i: (b, p, 0, 0, 0)),
        ],
        out_specs=pl.BlockSpec((None, tq, LANES), lambda b, p, i: (b, i, p)),
        out_shape=jax.ShapeDtypeStruct((bsz, s, d), BF16),
        scratch_shapes=[
            pltpu.VMEM((s, LANES), BF16),
            pltpu.VMEM((s, LANES), BF16),
            pltpu.VMEM((2, tq, LANES), F32),
            pltpu.VMEM((2, tq, LANES), F32),
            pltpu.VMEM((2, tq, LANES), F32),
        ],
        compiler_params=pltpu.CompilerParams(
            dimension_semantics=("arbitrary", "arbitrary", "arbitrary"),
            vmem_limit_bytes=48 * 1024 * 1024),
        name="fox_attn_online",
    )(q, k, v, ft5)


def _fox_attn(q, k, v, f_tok, q_gain, k_gain):
    hd = q.shape[-1] // FOX_HEADS
    shift = (hd ** 0.5) * jnp.max(jnp.abs(q_gain)).astype(F32) * jnp.max(jnp.abs(k_gain)).astype(F32)
    return lax.cond(shift < ATTN_SHIFT_MAX,
                    lambda: _fox_attn_fast(q, k, v, f_tok, shift),
                    lambda: _fox_attn_online(q, k, v, f_tok))


def _fox_out_kernel(x_ref, o_ref, gate_ref, w_ref, out_ref):
    og = (o_ref[...].astype(F32) * gate_ref[...].astype(F32)).astype(BF16)
    out_ref[...] = x_ref[...] + _dot(og, w_ref[...])


def _fox_out(x2, o2, gate2, w_out, *, tm=512):
    t, d = x2.shape
    tok = pl.BlockSpec((tm, d), lambda i: (i, 0))
    return pl.pallas_call(
        _fox_out_kernel,
        grid=(t // tm,),
        in_specs=[tok, tok, tok, pl.BlockSpec((d, d), lambda i: (0, 0))],
        out_specs=tok,
        out_shape=jax.ShapeDtypeStruct((t, d), F32),
        compiler_params=pltpu.CompilerParams(dimension_semantics=("arbitrary",)),
        name="fox_out",
    )(x2, o2, gate2, w_out.astype(BF16))


FFN_CHUNK = 512


def _swiglu_rows(hn, wg_ref, wu_ref, wd_ref, acc_scr):
    dff = wg_ref.shape[-1]
    for c in range(dff // FFN_CHUNK):
        cols = slice(c * FFN_CHUNK, (c + 1) * FFN_CHUNK)
        gg = _dot(hn, wg_ref[:, cols])
        uu = _dot(hn, wu_ref[:, cols])
        act = (gg * _sigmoid(gg)) * uu
        acc_scr[...] += _dot(act.astype(BF16), wd_ref[cols, :])


def _ffn_kernel(x_ref, g_ref, wg_ref, wu_ref, wd_ref, o_ref, acc_scr):
    x = x_ref[...]
    acc_scr[...] = x
    _swiglu_rows(_rms_rows(x, g_ref[...]).astype(BF16), wg_ref, wu_ref, wd_ref, acc_scr)
    o_ref[...] = acc_scr[...]


def _resident(shape, index_map):
    return pl.BlockSpec(shape, index_map, pipeline_mode=pl.Buffered(1))


def _ffn_layer(x2, g, w_gate, w_up, w_down, *, tm=512):
    t, d = x2.shape
    dff = w_gate.shape[1]
    tok = pl.BlockSpec((tm, d), lambda i: (i, 0))
    return pl.pallas_call(
        _ffn_kernel,
        grid=(t // tm,),
        in_specs=[tok, pl.BlockSpec((1, d), lambda i: (0, 0)),
                  _resident((d, dff), lambda i: (0, 0)),
                  _resident((d, dff), lambda i: (0, 0)),
                  _resident((dff, d), lambda i: (0, 0))],
        out_specs=tok,
        out_shape=jax.ShapeDtypeStruct((t, d), F32),
        scratch_shapes=[pltpu.VMEM((tm, d), F32)],
        compiler_params=pltpu.CompilerParams(
            dimension_semantics=("arbitrary",), vmem_limit_bytes=48 * 1024 * 1024),
        name="dense_ffn",
    )(x2, g.reshape(1, d).astype(F32), w_gate.astype(BF16), w_up.astype(BF16), w_down.astype(BF16))


def _top2(logits, lane):
    m1 = jnp.max(logits, axis=1, keepdims=True)
    i1 = jnp.min(jnp.where(logits == m1, lane, LANES), axis=1, keepdims=True)
    rest = jnp.where(lane == i1, NEG_BIG, logits)
    m2 = jnp.max(rest, axis=1, keepdims=True)
    i2 = jnp.min(jnp.where(rest == m2, lane, LANES), axis=1, keepdims=True)
    return m1, i1, m2, i2


def _router_logits(hn, rhi_ref, rlo_ref):
    h_hi = hn.astype(BF16)
    h_lo = (hn - h_hi.astype(F32)).astype(BF16)
    logits = (_dot(h_hi, rhi_ref[...]) + _dot(h_lo, rhi_ref[...])) + _dot(h_hi, rlo_ref[...])
    lane = lax.broadcasted_iota(jnp.int32, logits.shape, 1)
    return jnp.where(lane < N_EXPERTS, logits, NEG_BIG), lane


def _moe_route_kernel(x_ref, g_ref, rhi_ref, rlo_ref, idx_ref, w_ref, cnt_ref, carry_scr):
    @pl.when(pl.program_id(0) == 0)
    def _():
        carry_scr[...] = jnp.zeros_like(carry_scr)

    tm = x_ref.shape[0]
    hn = _rms_rows(x_ref[...], g_ref[...])
    logits, lane = _router_logits(hn, rhi_ref, rlo_ref)
    m1, i1, m2, i2 = _top2(logits, lane)
    e2 = jnp.exp(m2 - m1)
    w1 = 1.0 / (1.0 + e2)
    w2 = e2 / (1.0 + e2)
    oh1 = lane == i1
    oh2 = lane == i2
    member = jnp.where(oh1 | oh2, 1.0, 0.0)
    r_i = lax.broadcasted_iota(jnp.int32, (tm, tm), 0)
    c_i = lax.broadcasted_iota(jnp.int32, (tm, tm), 1)
    lstrict = jnp.where(c_i < r_i, 1.0, 0.0).astype(BF16)
    before = _dot(lstrict, member.astype(BF16)) + carry_scr[...]
    rank1 = jnp.sum(jnp.where(oh1, before, 0.0), axis=1, keepdims=True)
    rank2 = jnp.sum(jnp.where(oh2, before, 0.0), axis=1, keepdims=True)
    carry_scr[...] += jnp.sum(member, axis=0, keepdims=True)
    cnt_ref[...] = carry_scr[...]
    w_ref[...] = jnp.where(lane == 0, w1, jnp.where(lane == 1, w2, 0.0))
    cols = (i1.astype(F32), i2.astype(F32), rank1, rank2)
    packed = jnp.zeros((tm, LANES), F32)
    for c, col in enumerate(cols):
        packed = jnp.where(lane == c, col, packed)
    idx_ref[...] = packed.T[:SUBLANES, :].astype(jnp.int32)


def _moe_dispatch_kernel(meta_ref, d_ref, x_ref, xs_ref, zero_scr, sem, zsem, *, n_e):
    tm = x_ref.shape[0]

    for t in range(tm):
        for slot in range(2):
            row = d_ref[slot, t]
            pltpu.make_async_copy(x_ref.at[pl.ds(t, 1), :], xs_ref.at[pl.ds(row, 1), :],
                                  sem).start(priority=slot)
    for slot in range(2):
        pltpu.make_async_copy(x_ref, xs_ref.at[pl.ds(0, tm), :], sem).wait()

    @pl.when(pl.program_id(0) == pl.num_programs(0) - 1)
    def _():
        zero_scr[...] = jnp.zeros_like(zero_scr)
        zero_row = zero_scr.at[pl.ds(0, 1), :]
        for e in range(n_e):
            start = meta_ref[e]
            n_pad = meta_ref[n_e + e]

            def fill(r, carry):
                pltpu.make_async_copy(zero_row, xs_ref.at[pl.ds(start + r, 1), :], zsem).start()
                return carry

            def drain(r, carry):
                pltpu.make_async_copy(zero_row, xs_ref.at[pl.ds(0, 1), :], zsem).wait()
                return carry

            lax.fori_loop(0, n_pad, fill, 0)
            lax.fori_loop(0, n_pad, drain, 0)
        used = meta_ref[2 * n_e]
        tile = zero_scr.shape[0]

        def fill_tile(g, carry):
            row0 = pl.multiple_of((used + g) * tile, tile)
            cp = pltpu.make_async_copy(zero_scr, xs_ref.at[pl.ds(row0, tile), :], zsem)
            cp.start()
            cp.wait()
            return carry

        lax.fori_loop(0, meta_ref[2 * n_e + 1], fill_tile, 0)


def _moe_group_kernel(exp_ref, rows_ref, xs_ref, g_ref, wg_ref, wu_ref, wd_ref, ys_ref, acc_scr):
    live = rows_ref[pl.program_id(0)] > 0

    @pl.when(live)
    def _():
        acc_scr[...] = jnp.zeros_like(acc_scr)
        _swiglu_rows(_rms_rows(xs_ref[...], g_ref[...]).astype(BF16), wg_ref, wu_ref, wd_ref, acc_scr)
        ys_ref[...] = acc_scr[...]

    @pl.when(jnp.logical_not(live))
    def _():
        ys_ref[...] = jnp.zeros_like(ys_ref)


def _moe_combine_kernel(d_ref, x_ref, w_ref, ys_ref, o_ref, buf, sem):
    tm = x_ref.shape[0]

    for t in range(tm):
        for slot in range(2):
            row = d_ref[slot, t]
            pltpu.make_async_copy(ys_ref.at[pl.ds(row, 1), :], buf.at[slot, pl.ds(t, 1), :],
                                  sem).start(priority=slot)
    for slot in range(2):
        pltpu.make_async_copy(ys_ref.at[pl.ds(0, tm), :], buf.at[slot], sem).wait()
    w = w_ref[...]
    o_ref[...] = x_ref[...] + w[:, 0:1] * buf[0] + w[:, 1:2] * buf[1]


def _moe_layer(x2, g, w_gate, w_up, w_down, router, *, tr=1024, tm=512):
    t, d = x2.shape
    n_e, _, dff = w_gate.shape
    nt = t // tr
    n_tiles = (2 * t) // tm + n_e
    g_row = g.reshape(1, d).astype(F32)
    r_pad = jnp.pad(router.astype(F32), ((0, 0), (0, LANES - n_e)))
    r_hi = r_pad.astype(BF16)
    r_lo = (r_pad - r_hi.astype(F32)).astype(BF16)

    idx, wts, counts = pl.pallas_call(
        _moe_route_kernel,
        grid=(nt,),
        in_specs=[pl.BlockSpec((tr, d), lambda i: (i, 0)), pl.BlockSpec((1, d), lambda i: (0, 0)),
                  pl.BlockSpec((d, LANES), lambda i: (0, 0)), pl.BlockSpec((d, LANES), lambda i: (0, 0))],
        out_specs=[pl.BlockSpec((None, SUBLANES, tr), lambda i: (i, 0, 0)),
                   pl.BlockSpec((tr, LANES), lambda i: (i, 0)),
                   pl.BlockSpec((1, LANES), lambda i: (0, 0))],
        out_shape=[jax.ShapeDtypeStruct((nt, SUBLANES, tr), jnp.int32),
                   jax.ShapeDtypeStruct((t, LANES), F32),
                   jax.ShapeDtypeStruct((1, LANES), F32)],
        scratch_shapes=[pltpu.VMEM((1, LANES), F32)],
        compiler_params=pltpu.CompilerParams(dimension_semantics=("arbitrary",)),
        name="moe_route",
    )(x2, g_row, r_hi, r_lo)

    cnt = counts[0, :n_e].astype(jnp.int32)
    tiles_e = (cnt + tm - 1) // tm
    ends = jnp.cumsum(tiles_e)
    starts = ends - tiles_e
    used = ends[-1]
    dest = starts[idx[:, 0:2, :]] * tm + idx[:, 2:4, :]
    dest = jnp.concatenate([dest, jnp.zeros((nt, SUBLANES - 2, tr), jnp.int32)], axis=1)
    tile_ids = jnp.arange(n_tiles, dtype=jnp.int32)
    t_exp = jnp.minimum(jnp.searchsorted(ends, tile_ids, side="right"), n_e - 1).astype(jnp.int32)
    t_rows = jnp.where(tile_ids < used,
                       jnp.clip(cnt[t_exp] - (tile_ids - starts[t_exp]) * tm, 0, tm), 0).astype(jnp.int32)
    meta = jnp.concatenate([starts * tm + cnt, tiles_e * tm - cnt,
                            jnp.stack([used, n_tiles - used])]).astype(jnp.int32)

    smem_rows = lambda: pl.BlockSpec((None, SUBLANES, tr), lambda i, *_: (i, 0, 0), memory_space=pltpu.SMEM)
    xs = pl.pallas_call(
        functools.partial(_moe_dispatch_kernel, n_e=n_e),
        grid_spec=pltpu.PrefetchScalarGridSpec(
            num_scalar_prefetch=1,
            grid=(nt,),
            in_specs=[smem_rows(), pl.BlockSpec((tr, d), lambda i, meta: (i, 0))],
            out_specs=pl.BlockSpec(memory_space=pl.ANY),
            scratch_shapes=[pltpu.VMEM((tm, d), F32), pltpu.SemaphoreType.DMA(()),
                            pltpu.SemaphoreType.DMA(())],
        ),
        out_shape=jax.ShapeDtypeStruct((n_tiles * tm, d), F32),
        compiler_params=pltpu.CompilerParams(
            dimension_semantics=("arbitrary",), vmem_limit_bytes=48 * 1024 * 1024),
        name="moe_dispatch",
    )(meta, dest, x2)

    ys = pl.pallas_call(
        _moe_group_kernel,
        grid_spec=pltpu.PrefetchScalarGridSpec(
            num_scalar_prefetch=2,
            grid=(n_tiles,),
            in_specs=[
                pl.BlockSpec((tm, d), lambda i, ex, rows: (i, 0)),
                pl.BlockSpec((1, d), lambda i, ex, rows: (0, 0)),
                _resident((None, d, dff), lambda i, ex, rows: (ex[i], 0, 0)),
                _resident((None, d, dff), lambda i, ex, rows: (ex[i], 0, 0)),
                _resident((None, dff, d), lambda i, ex, rows: (ex[i], 0, 0)),
            ],
            out_specs=pl.BlockSpec((tm, d), lambda i, ex, rows: (i, 0)),
            scratch_shapes=[pltpu.VMEM((tm, d), F32)],
        ),
        out_shape=jax.ShapeDtypeStruct((n_tiles * tm, d), F32),
        compiler_params=pltpu.CompilerParams(
            dimension_semantics=("arbitrary",), vmem_limit_bytes=48 * 1024 * 1024),
        name="moe_group",
    )(t_exp, t_rows, xs, g_row, w_gate.astype(BF16), w_up.astype(BF16), w_down.astype(BF16))

    return pl.pallas_call(
        _moe_combine_kernel,
        grid=(nt,),
        in_specs=[smem_rows(), pl.BlockSpec((tr, d), lambda i: (i, 0)),
                  pl.BlockSpec((tr, LANES), lambda i: (i, 0)), pl.BlockSpec(memory_space=pl.ANY)],
        out_specs=pl.BlockSpec((tr, d), lambda i: (i, 0)),
        out_shape=jax.ShapeDtypeStruct((t, d), F32),
        scratch_shapes=[pltpu.VMEM((2, tr, d), F32), pltpu.SemaphoreType.DMA(())],
        compiler_params=pltpu.CompilerParams(
            dimension_semantics=("arbitrary",), vmem_limit_bytes=48 * 1024 * 1024),
        name="moe_combine",
    )(dest, x2, wts, ys)


def kernel(x, norm_mix, norm_ffn, lru_w_in, lru_conv_w, lru_conv_b, lru_w_a, lru_b_a, lru_w_x, lru_b_x,
           lru_log_lambda, lru_w_out, fox_w_in, fox_b_f, fox_q_gain, fox_k_gain, fox_w_out, ffn_w_gate,
           ffn_w_up, ffn_w_down, moe_router, moe_w_gate, moe_w_up, moe_w_down):
    bsz, s, d = x.shape
    depth = norm_mix.shape[0]
    for i in range(depth):
        j = i // 2
        if i % 2 == 0:
            x = _lru_layer(x, norm_mix[i], lru_w_in[j], lru_conv_w[j], lru_conv_b[j], lru_w_a[j],
                           lru_b_a[j], lru_w_x[j], lru_b_x[j], lru_log_lambda[j], lru_w_out[j])
            x2 = _ffn_layer(x.reshape(bsz * s, d), norm_ffn[i], ffn_w_gate[j], ffn_w_up[j], ffn_w_down[j])
        else:
            q, k, v, gate, ft = _fox_proj(x, norm_mix[i], fox_w_in[j], fox_b_f[j], fox_q_gain[j],
                                          fox_k_gain[j])
            o = _fox_attn(q, k, v, ft, fox_q_gain[j], fox_k_gain[j])
            x2 = _fox_out(x.reshape(bsz * s, d), o.reshape(bsz * s, d), gate.reshape(bsz * s, d),
                          fox_w_out[j])
            x2 = _moe_layer(x2, norm_ffn[i], moe_w_gate[j], moe_w_up[j], moe_w_down[j], moe_router[j])
        x = x2.reshape(bsz, s, d)
    return x
```

```python
import functools

import jax
import jax.numpy as jnp
from jax import lax
from jax.experimental import pallas as pl
from jax.experimental.pallas import tpu as pltpu

RMS_EPS = 1e-6
LRU_C = 8.0
LRU_BLOCKS = 8
CONV_WIDTH = 4
FOX_HEADS = 16
N_EXPERTS = 8
LANES = 128
SUBLANES = 8
NEG_BIG = -1e30
LOG2E = 1.4426950408889634
ATTN_SHIFT_MAX = 40.0
ATTN_DEAD_LOG2 = 160.0
ATTN_BLOCKS_PER_STEP = 8

F32 = jnp.float32
BF16 = jnp.bfloat16


def _dot(a, b):
    return jnp.dot(a, b, preferred_element_type=F32)


def _rms_rows(x, g):
    ms = jnp.mean(x * x, axis=-1, keepdims=True)
    return x * lax.rsqrt(ms + RMS_EPS) * g


def _sigmoid(x):
    return 1.0 / (1.0 + jnp.exp(-x))


def _softplus(x):
    return jnp.maximum(x, 0.0) + jnp.log1p(jnp.exp(-jnp.abs(x)))


def _gelu_tanh(x):
    return 0.5 * x * (1.0 + jnp.tanh(0.7978845608028654 * (x + 0.044715 * (x * x * x))))


def _split3(x):
    hi = x.astype(BF16)
    r1 = x - hi.astype(F32)
    mid = r1.astype(BF16)
    lo = (r1 - mid.astype(F32)).astype(BF16)
    return hi, mid, lo


def _lru_kernel(x_ref, g_ref, win_ref, cw_ref, cb_ref, wgate_ref, ba_ref, bx_ref, lam_ref, wout_ref,
                o_ref, xb_scr, hy_scr, carry_scr):
    t = pl.program_id(1)
    ts, d = x_ref.shape
    w = d

    @pl.when(t == 0)
    def _():
        xb_scr[0:SUBLANES, :] = jnp.zeros((SUBLANES, w), F32)
        carry_scr[...] = jnp.zeros_like(carry_scr)

    x = x_ref[...]
    hn = _rms_rows(x, g_ref[...]).astype(BF16)
    proj = _dot(hn, win_ref[...])
    xb = proj[:, w:]
    xb_scr[SUBLANES:SUBLANES + ts, :] = xb
    xc = cw_ref[3:4, :] * xb + cb_ref[...]
    for k in range(1, CONV_WIDTH):
        xc = xc + cw_ref[3 - k:4 - k, :] * xb_scr[SUBLANES - k:SUBLANES - k + ts, :]
    xb_scr[0:SUBLANES, :] = xb[ts - SUBLANES:, :]
    xcb = xc.astype(BF16)

    sp = _softplus(-lam_ref[...])
    groups = ts // SUBLANES
    rowmod = lax.broadcasted_iota(jnp.int32, (groups, SUBLANES, LANES), 1)
    for h in range(LRU_BLOCKS):
        sl = slice(h * LANES, (h + 1) * LANES)
        z = _dot(xcb[:, sl], wgate_ref[h])
        r = _sigmoid(z[:, :LANES] + ba_ref[:, sl])
        gi = _sigmoid(z[:, LANES:] + bx_ref[:, sl])
        log_a = (-LRU_C) * r * sp[:, sl]
        a = jnp.exp(log_a)
        b = jnp.sqrt(-jnp.tanh(log_a) * (a * a + 1.0)) * (gi * xc[:, sl])
        a = a.reshape(groups, SUBLANES, LANES)
        b = b.reshape(groups, SUBLANES, LANES)
        for sh in (1, 2, 4):
            keep = rowmod >= sh
            a_s = jnp.where(keep, pltpu.roll(a, sh, 1), 1.0)
            b_s = jnp.where(keep, pltpu.roll(b, sh, 1), 0.0)
            b = a * b_s + b
            a = a * a_s
        carry = carry_scr[h:h + 1, :]
        outs = []
        for gidx in range(groups):
            hg = a[gidx] * carry + b[gidx]
            outs.append(hg)
            carry = hg[SUBLANES - 1:SUBLANES, :]
        carry_scr[h:h + 1, :] = carry
        hseq = jnp.concatenate(outs, axis=0)
        yb = _gelu_tanh(proj[:, sl])
        hy_scr[:, sl] = (hseq * yb).astype(BF16)

    o_ref[...] = x + _dot(hy_scr[...], wout_ref[...])


def _lru_layer(x, g, w_in, conv_w, conv_b, w_a, b_a, w_x, b_x, log_lambda, w_out, *, ts=512):
    bsz, s, d = x.shape
    w = d
    wgate = jnp.concatenate([w_a, w_x], axis=-1).astype(BF16)
    row = lambda v: v.reshape(1, -1).astype(F32)
    const2 = lambda shape: pl.BlockSpec(shape, lambda b, t: (0, 0))
    return pl.pallas_call(
        _lru_kernel,
        grid=(bsz, s // ts),
        in_specs=[
            pl.BlockSpec((None, ts, d), lambda b, t: (b, t, 0)),
            const2((1, d)),
            const2((d, 2 * w)),
            const2((CONV_WIDTH, w)),
            const2((1, w)),
            pl.BlockSpec((LRU_BLOCKS, LANES, 2 * LANES), lambda b, t: (0, 0, 0)),
            const2((1, w)),
            const2((1, w)),
            const2((1, w)),
            const2((w, d)),
        ],
        out_specs=pl.BlockSpec((None, ts, d), lambda b, t: (b, t, 0)),
        out_shape=jax.ShapeDtypeStruct(x.shape, F32),
        scratch_shapes=[
            pltpu.VMEM((ts + SUBLANES, w), F32),
            pltpu.VMEM((ts, w), BF16),
            pltpu.VMEM((LRU_BLOCKS, LANES), F32),
        ],
        compiler_params=pltpu.CompilerParams(
            dimension_semantics=("arbitrary", "arbitrary"), vmem_limit_bytes=48 * 1024 * 1024),
        name="lru_mixer",
    )(x, row(g), w_in.astype(BF16), conv_w.astype(F32), row(conv_b), wgate, row(b_a), row(b_x),
      row(log_lambda), w_out.astype(BF16))


def _fox_proj_kernel(x_ref, g_ref, w_ref, wf_ref, bf_ref, qg_ref, kg_ref, hm_ref,
                     q_ref, k_ref, v_ref, gate_ref, f_ref, carry_scr):
    t = pl.program_id(1)
    ts, d = x_ref.shape

    @pl.when(t == 0)
    def _():
        carry_scr[...] = jnp.zeros_like(carry_scr)

    hn = _rms_rows(x_ref[...], g_ref[...]).astype(BF16)
    proj = _dot(hn, w_ref[...])
    hm = hm_ref[...]

    def head_norm(z, gain):
        ms = _dot((z * z).astype(BF16), hm)
        return z * lax.rsqrt(ms + RMS_EPS) * gain

    q_ref[...] = head_norm(proj[:, :d], qg_ref[...]).astype(BF16)
    k_ref[...] = head_norm(proj[:, d:2 * d], kg_ref[...]).astype(BF16)
    v_ref[...] = proj[:, 2 * d:3 * d].astype(BF16)
    gate_ref[...] = _sigmoid(proj[:, 3 * d:]).astype(BF16)

    f = _dot(hn, wf_ref[...]) + bf_ref[...]
    lf = -_softplus(-f)
    r_i = lax.broadcasted_iota(jnp.int32, (ts, ts), 0)
    c_i = lax.broadcasted_iota(jnp.int32, (ts, ts), 1)
    ltri = jnp.where(c_i <= r_i, 1.0, 0.0).astype(BF16)
    hi, mid, lo = _split3(lf)
    fcum = (_dot(ltri, hi) + _dot(ltri, mid)) + _dot(ltri, lo) + carry_scr[...]
    carry_scr[...] = fcum[ts - 1:ts, :]
    f_ref[...] = fcum * LOG2E


def _fox_proj(x, g, w_in, b_f, q_gain, k_gain, *, ts=256):
    bsz, s, d = x.shape
    hd = d // FOX_HEADS
    w_main = w_in[:, :4 * d].astype(BF16)
    w_f = jnp.pad(w_in[:, 4 * d:], ((0, 0), (0, LANES - FOX_HEADS))).astype(BF16)
    bf_row = jnp.pad(b_f.astype(F32), (0, LANES - FOX_HEADS)).reshape(1, LANES)
    scale = hd ** -0.5
    qg_row = (jnp.tile(q_gain.astype(F32), FOX_HEADS) * (scale * LOG2E)).reshape(1, d)
    kg_row = jnp.tile(k_gain.astype(F32), FOX_HEADS).reshape(1, d)
    head_id = jnp.arange(d) // hd
    hm = jnp.where(head_id[:, None] == head_id[None, :], 1.0 / hd, 0.0).astype(BF16)
    const2 = lambda shape: pl.BlockSpec(shape, lambda b, t: (0, 0))
    tok = pl.BlockSpec((None, ts, d), lambda b, t: (b, t, 0))
    act = jax.ShapeDtypeStruct((bsz, s, d), BF16)
    return pl.pallas_call(
        _fox_proj_kernel,
        grid=(bsz, s // ts),
        in_specs=[tok, const2((1, d)), const2((d, 4 * d)), const2((d, LANES)), const2((1, LANES)),
                  const2((1, d)), const2((1, d)), const2((d, d))],
        out_specs=[tok, tok, tok, tok, pl.BlockSpec((None, ts, LANES), lambda b, t: (b, t, 0))],
        out_shape=[act, act, act, act, jax.ShapeDtypeStruct((bsz, s, LANES), F32)],
        scratch_shapes=[pltpu.VMEM((1, LANES), F32)],
        compiler_params=pltpu.CompilerParams(
            dimension_semantics=("arbitrary", "arbitrary"), vmem_limit_bytes=48 * 1024 * 1024),
        name="fox_proj",
    )(x, g.reshape(1, d).astype(F32), w_main, w_f, bf_row, qg_row, kg_row, hm)


def _aug_lanes(hd):
    base = SUBLANES * hd
    return base, base + 3


def _attn_placement(pairs):
    pq = jnp.zeros((pairs, 3, LANES, LANES), F32)
    pk = jnp.zeros((pairs, 2, 3, LANES, LANES), F32)
    for p in range(pairs):
        for hd in range(2):
            q_lo, k_lo = _aug_lanes(hd)
            for i in range(3):
                pq = pq.at[p, i, 2 * p + hd, q_lo + i].set(1.0)
                pk = pk.at[p, hd, i, 2 * p + hd, k_lo + i].set(1.0)
    return pq.astype(BF16), pk.astype(BF16)


def _place(x, p_ref, idx):
    hi, mid, lo = _split3(x)
    return (_dot(hi, p_ref[idx + (0,)]) + _dot(mid, p_ref[idx + (1,)])) + _dot(lo, p_ref[idx + (2,)])


def _fox_attn_fast_kernel(js_ref, c_ref, q_ref, k_ref, v_ref, f_ref, pq_ref, pk_ref, o_ref,
                          ka_scr, kb_scr, va_scr, vb_scr, acc_scr, *, chunk, tq):
    tk = tq
    s_len = k_ref.shape[0]
    nq = s_len // tq
    js_pair = ((pl.program_id(0) * pl.num_programs(1) + pl.program_id(1)) * 2) * nq
    half = LANES // 2
    k_scrs = (ka_scr, kb_scr)
    v_scrs = (va_scr, vb_scr)
    sum_lane = (half, half - 1)

    def build(c, carry):
        rows = pl.ds(pl.multiple_of(c * chunk, chunk), chunk)
        kk = k_ref[rows, :]
        vv = v_ref[rows, :]
        lane = lax.broadcasted_iota(jnp.int32, (chunk, LANES), 1)
        nf = -f_ref[rows, :]
        for hd in range(2):
            ones_lo, _ = _aug_lanes(hd)
            aug = _place(nf, pk_ref, (hd,))
            aug = aug + jnp.where((lane >= ones_lo) & (lane < ones_lo + 3), 1.0, 0.0)
            keep = (lane < half) if hd == 0 else (lane >= half)
            k_scrs[hd][rows, 0:LANES] = jnp.where(keep, kk, jnp.zeros_like(kk))
            k_scrs[hd][rows, LANES:2 * LANES] = aug.astype(BF16)
            v_scrs[hd][rows, :] = jnp.where(lane == sum_lane[hd], jnp.ones_like(vv), vv)
        return carry

    lax.fori_loop(0, s_len // chunk, build, 0)
    lax.fori_loop(0, nq, functools.partial(
        _fox_attn_q_block, js_ref=js_ref, js_pair=js_pair, c_ref=c_ref, q_ref=q_ref, f_ref=f_ref, pq_ref=pq_ref,
        o_ref=o_ref, k_scrs=k_scrs, v_scrs=v_scrs, acc_scr=acc_scr, sum_lane=sum_lane, tq=tq, nq=nq), 0)


def _fox_attn_q_block(qi, carry, *, js_ref, js_pair, c_ref, q_ref, f_ref, pq_ref, o_ref, k_scrs, v_scrs, acc_scr,
                      sum_lane, tq, nq):
    tk = tq
    half = LANES // 2
    js_base = js_pair + qi
    q_rows = pl.ds(pl.multiple_of(qi * tq, tq), tq)
    lane_q = lax.broadcasted_iota(jnp.int32, (tq, LANES), 1)
    fq = f_ref[q_rows, :] - c_ref[...]
    augq = _place(fq, pq_ref, ())
    q_ones = ((lane_q >= 3) & (lane_q < 6)) | ((lane_q >= SUBLANES + 3) & (lane_q < SUBLANES + 6))
    augq = augq + jnp.where(q_ones, 1.0, 0.0)
    q_aug = jnp.concatenate([q_ref[q_rows, :], augq.astype(BF16)], axis=1)
    acc_scr[...] = jnp.zeros_like(acc_scr)

    def probs(j, hd, masked=False):
        rows = pl.ds(pl.multiple_of(j * tk, tk), tk)
        s = lax.dot_general(q_aug, k_scrs[hd][rows, :], (((1,), (1,)), ((), ())),
                            preferred_element_type=F32)
        p = jnp.exp2(s)
        if masked:
            r_i = lax.broadcasted_iota(jnp.int32, (tq, tk), 0)
            c_i = lax.broadcasted_iota(jnp.int32, (tq, tk), 1)
            p = jnp.where(c_i <= r_i, p, 0.0)
        return p.astype(BF16), v_scrs[hd][rows, :]

    def attend(j, hd, count):
        total = None
        for u in range(count):
            pu, vu = probs(j + u, hd)
            term = _dot(pu, vu)
            total = term if total is None else total + term
        acc_scr[hd] += total

    for hd in range(2):
        j0 = js_ref[js_base + hd * nq]
        n = qi - j0
        j = j0
        arm = 1
        while arm < ATTN_BLOCKS_PER_STEP:
            bit = n & arm

            @pl.when(bit != 0)
            def _(j=j, arm=arm):
                attend(j, hd, arm)

            j = j + bit
            arm *= 2

        def body(t, carry, j=j):
            attend(j + ATTN_BLOCKS_PER_STEP * t, hd, ATTN_BLOCKS_PER_STEP)
            return carry

        lax.fori_loop(0, lax.shift_right_logical(n, ATTN_BLOCKS_PER_STEP.bit_length() - 1), body, 0)

    for hd in range(2):
        pd, vd = probs(qi, hd, masked=True)
        acc_scr[hd] += _dot(pd, vd)

    acc_a = acc_scr[0]
    acc_b = acc_scr[1]
    oa = acc_a / acc_a[:, sum_lane[0]:sum_lane[0] + 1]
    ob = acc_b / acc_b[:, sum_lane[1]:sum_lane[1] + 1]
    o_ref[q_rows, :] = jnp.where(lane_q < half, oa, ob).astype(o_ref.dtype)
    return carry


def _first_live_block(f_tok, tq):
    bsz, s, _ = f_tok.shape
    nq = s // tq
    f_heads = f_tok[:, :, :FOX_HEADS]
    f_first = f_heads[:, 0::tq, :]
    f_last = f_heads[:, tq - 1::tq, :]
    gap = f_first[:, :, None, :] - f_last[:, None, :, :]
    blk = jnp.arange(nq)
    dead = (gap < -ATTN_DEAD_LOG2) & (blk[None, None, :, None] < blk[None, :, None, None])
    first = jnp.sum(dead, axis=2).astype(jnp.int32)
    return jnp.transpose(first, (0, 2, 1)).reshape(-1)


def _fox_attn_fast(q, k, v, f_tok, shift, *, tq=512, chunk=1024):
    bsz, s, d = q.shape
    pairs = d // LANES
    pq, pk = _attn_placement(pairs)
    c_row = jnp.full((1, LANES), shift * LOG2E, F32)
    first_live = _first_live_block(f_tok, tq)
    return pl.pallas_call(
        functools.partial(_fox_attn_fast_kernel, chunk=chunk, tq=tq),
        grid_spec=pltpu.PrefetchScalarGridSpec(
            num_scalar_prefetch=1,
            grid=(bsz, pairs),
            in_specs=[
                pl.BlockSpec((1, LANES), lambda b, p, js: (0, 0)),
                pl.BlockSpec((None, s, LANES), lambda b, p, js: (b, 0, p)),
                pl.BlockSpec((None, s, LANES), lambda b, p, js: (b, 0, p)),
                pl.BlockSpec((None, s, LANES), lambda b, p, js: (b, 0, p)),
                pl.BlockSpec((None, s, LANES), lambda b, p, js: (b, 0, 0)),
                pl.BlockSpec((None, 3, LANES, LANES), lambda b, p, js: (p, 0, 0, 0)),
                pl.BlockSpec((None, 2, 3, LANES, LANES), lambda b, p, js: (p, 0, 0, 0, 0)),
            ],
            out_specs=pl.BlockSpec((None, s, LANES), lambda b, p, js: (b, 0, p)),
            scratch_shapes=[
                pltpu.VMEM((s, 2 * LANES), BF16),
                pltpu.VMEM((s, 2 * LANES), BF16),
                pltpu.VMEM((s, LANES), BF16),
                pltpu.VMEM((s, LANES), BF16),
                pltpu.VMEM((2, tq, LANES), F32),
            ],
        ),
        out_shape=jax.ShapeDtypeStruct((bsz, s, d), BF16),
        compiler_params=pltpu.CompilerParams(
            dimension_semantics=("arbitrary", "arbitrary"), vmem_limit_bytes=48 * 1024 * 1024),
        name="fox_attn_fast",
    )(first_live, c_row, q, k, v, f_tok, pq, pk)


def _fox_attn_online_kernel(q_ref, k_ref, v_ref, ft_ref, o_ref, ka_scr, kb_scr, m_scr, l_scr, acc_scr):
    qi = pl.program_id(2)
    tq = q_ref.shape[0]
    tk = tq
    half = LANES // 2
    lane = lax.broadcasted_iota(jnp.int32, (1, LANES), 1)

    @pl.when(qi == 0)
    def _():
        kk = k_ref[...]
        lane_k = lax.broadcasted_iota(jnp.int32, kk.shape, 1)
        zero = jnp.zeros_like(kk)
        ka_scr[...] = jnp.where(lane_k < half, kk, zero)
        kb_scr[...] = jnp.where(lane_k >= half, kk, zero)

    m_scr[...] = jnp.full_like(m_scr, NEG_BIG)
    l_scr[...] = jnp.zeros_like(l_scr)
    acc_scr[...] = jnp.zeros_like(acc_scr)
    q = q_ref[...]

    def attend(j, masked):
        start = pl.multiple_of(j * tk, tk)
        v = v_ref[pl.ds(start, tk), :]
        for hd, k_scr in ((0, ka_scr), (1, kb_scr)):
            kb = k_scr[pl.ds(start, tk), :]
            s = lax.dot_general(q, kb, (((1,), (1,)), ((), ())), preferred_element_type=F32)
            s = s - ft_ref[hd, pl.ds(j, 1), :]
            if masked:
                r_i = lax.broadcasted_iota(jnp.int32, (tq, tk), 0)
                c_i = lax.broadcasted_iota(jnp.int32, (tq, tk), 1)
                s = jnp.where(c_i <= r_i, s, NEG_BIG)
            m_prev = m_scr[hd]
            m_new = jnp.maximum(m_prev, jnp.max(s, axis=1, keepdims=True))
            p = jnp.exp2(s - m_new[:, :1])
            alpha = jnp.exp2(m_prev - m_new)
            l_scr[hd] = alpha * l_scr[hd] + jnp.sum(p, axis=1, keepdims=True)
            acc_scr[hd] = alpha * acc_scr[hd] + _dot(p.astype(BF16), v)
            m_scr[hd] = m_new

    def body(j, carry):
        attend(j, False)
        return carry

    lax.fori_loop(0, qi, body, 0)
    attend(qi, True)

    oa = acc_scr[0] / l_scr[0]
    ob = acc_scr[1] / l_scr[1]
    o_ref[...] = jnp.where(lane < half, oa, ob).astype(o_ref.dtype)


def _fox_attn_online(q, k, v, f_tok, *, tq=512):
    bsz, s, d = q.shape
    pairs = d // LANES
    nq = s // tq
    ft5 = jnp.swapaxes(f_tok[:, :, :FOX_HEADS], 1, 2).reshape(bsz, pairs, 2, nq, tq)
    return pl.pallas_call(
        _fox_attn_online_kernel,
        grid=(bsz, pairs, nq),
        in_specs=[
            pl.BlockSpec((None, tq, LANES), lambda b, p, i: (b, i, p)),
            pl.BlockSpec((None, s, LANES), lambda b, p, i: (b, 0, p)),
            pl.BlockSpec((None, s, LANES), lambda b, p, i: (b, 0, p)),
            pl.BlockSpec((None, None, 2, nq, tq), lambda b, p, i: (b, p, 0, 0, 0)),
        ],
        out_specs=pl.BlockSpec((None, tq, LANES), lambda b, p, i: (b, i, p)),
        out_shape=jax.ShapeDtypeStruct((bsz, s, d), BF16),
        scratch_shapes=[
            pltpu.VMEM((s, LANES), BF16),
            pltpu.VMEM((s, LANES), BF16),
            pltpu.VMEM((2, tq, LANES), F32),
            pltpu.VMEM((2, tq, LANES), F32),
            pltpu.VMEM((2, tq, LANES), F32),
        ],
        compiler_params=pltpu.CompilerParams(
            dimension_semantics=("arbitrary", "arbitrary", "arbitrary"),
            vmem_limit_bytes=48 * 1024 * 1024),
        name="fox_attn_online",
    )(q, k, v, ft5)


def _fox_attn(q, k, v, f_tok, q_gain, k_gain):
    hd = q.shape[-1] // FOX_HEADS
    shift = (hd ** 0.5) * jnp.max(jnp.abs(q_gain)).astype(F32) * jnp.max(jnp.abs(k_gain)).astype(F32)
    return lax.cond(shift < ATTN_SHIFT_MAX,
                    lambda: _fox_attn_fast(q, k, v, f_tok, shift),
                    lambda: _fox_attn_online(q, k, v, f_tok))


def _fox_out_kernel(x_ref, o_ref, gate_ref, w_ref, out_ref):
    og = (o_ref[...].astype(F32) * gate_ref[...].astype(F32)).astype(BF16)
    out_ref[...] = x_ref[...] + _dot(og, w_ref[...])


def _fox_out(x2, o2, gate2, w_out, *, tm=512):
    t, d = x2.shape
    tok = pl.BlockSpec((tm, d), lambda i: (i, 0))
    return pl.pallas_call(
        _fox_out_kernel,
        grid=(t // tm,),
        in_specs=[tok, tok, tok, pl.BlockSpec((d, d), lambda i: (0, 0))],
        out_specs=tok,
        out_shape=jax.ShapeDtypeStruct((t, d), F32),
        compiler_params=pltpu.CompilerParams(dimension_semantics=("arbitrary",)),
        name="fox_out",
    )(x2, o2, gate2, w_out.astype(BF16))


FFN_CHUNK = 512


def _swiglu_rows(hn, wg_ref, wu_ref, wd_ref, acc_scr):
    dff = wg_ref.shape[-1]
    for c in range(dff // FFN_CHUNK):
        cols = slice(c * FFN_CHUNK, (c + 1) * FFN_CHUNK)
        gg = _dot(hn, wg_ref[:, cols])
        uu = _dot(hn, wu_ref[:, cols])
        act = (gg * _sigmoid(gg)) * uu
        acc_scr[...] += _dot(act.astype(BF16), wd_ref[cols, :])


def _ffn_kernel(x_ref, g_ref, wg_ref, wu_ref, wd_ref, o_ref, acc_scr):
    x = x_ref[...]
    acc_scr[...] = x
    _swiglu_rows(_rms_rows(x, g_ref[...]).astype(BF16), wg_ref, wu_ref, wd_ref, acc_scr)
    o_ref[...] = acc_scr[...]


def _resident(shape, index_map):
    return pl.BlockSpec(shape, index_map, pipeline_mode=pl.Buffered(1))


def _ffn_layer(x2, g, w_gate, w_up, w_down, *, tm=512):
    t, d = x2.shape
    dff = w_gate.shape[1]
    tok = pl.BlockSpec((tm, d), lambda i: (i, 0))
    return pl.pallas_call(
        _ffn_kernel,
        grid=(t // tm,),
        in_specs=[tok, pl.BlockSpec((1, d), lambda i: (0, 0)),
                  _resident((d, dff), lambda i: (0, 0)),
                  _resident((d, dff), lambda i: (0, 0)),
                  _resident((dff, d), lambda i: (0, 0))],
        out_specs=tok,
        out_shape=jax.ShapeDtypeStruct((t, d), F32),
        scratch_shapes=[pltpu.VMEM((tm, d), F32)],
        compiler_params=pltpu.CompilerParams(
            dimension_semantics=("arbitrary",), vmem_limit_bytes=48 * 1024 * 1024),
        name="dense_ffn",
    )(x2, g.reshape(1, d).astype(F32), w_gate.astype(BF16), w_up.astype(BF16), w_down.astype(BF16))


def _top2(logits, lane):
    m1 = jnp.max(logits, axis=1, keepdims=True)
    i1 = jnp.min(jnp.where(logits == m1, lane, LANES), axis=1, keepdims=True)
    rest = jnp.where(lane == i1, NEG_BIG, logits)
    m2 = jnp.max(rest, axis=1, keepdims=True)
    i2 = jnp.min(jnp.where(rest == m2, lane, LANES), axis=1, keepdims=True)
    return m1, i1, m2, i2


def _router_logits(hn, rhi_ref, rlo_ref):
    h_hi = hn.astype(BF16)
    h_lo = (hn - h_hi.astype(F32)).astype(BF16)
    logits = (_dot(h_hi, rhi_ref[...]) + _dot(h_lo, rhi_ref[...])) + _dot(h_hi, rlo_ref[...])
    lane = lax.broadcasted_iota(jnp.int32, logits.shape, 1)
    return jnp.where(lane < N_EXPERTS, logits, NEG_BIG), lane


def _moe_route_kernel(x_ref, g_ref, rhi_ref, rlo_ref, idx_ref, w_ref, cnt_ref, carry_scr):
    @pl.when(pl.program_id(0) == 0)
    def _():
        carry_scr[...] = jnp.zeros_like(carry_scr)

    tm = x_ref.shape[0]
    hn = _rms_rows(x_ref[...], g_ref[...])
    logits, lane = _router_logits(hn, rhi_ref, rlo_ref)
    m1, i1, m2, i2 = _top2(logits, lane)
    e2 = jnp.exp(m2 - m1)
    w1 = 1.0 / (1.0 + e2)
    w2 = e2 / (1.0 + e2)
    oh1 = lane == i1
    oh2 = lane == i2
    member = jnp.where(oh1 | oh2, 1.0, 0.0)
    r_i = lax.broadcasted_iota(jnp.int32, (tm, tm), 0)
    c_i = lax.broadcasted_iota(jnp.int32, (tm, tm), 1)
    lstrict = jnp.where(c_i < r_i, 1.0, 0.0).astype(BF16)
    before = _dot(lstrict, member.astype(BF16)) + carry_scr[...]
    rank1 = jnp.sum(jnp.where(oh1, before, 0.0), axis=1, keepdims=True)
    rank2 = jnp.sum(jnp.where(oh2, before, 0.0), axis=1, keepdims=True)
    carry_scr[...] += jnp.sum(member, axis=0, keepdims=True)
    cnt_ref[...] = carry_scr[...]
    w_ref[...] = jnp.where(lane == 0, w1, jnp.where(lane == 1, w2, 0.0))
    cols = (i1.astype(F32), i2.astype(F32), rank1, rank2)
    packed = jnp.zeros((tm, LANES), F32)
    for c, col in enumerate(cols):
        packed = jnp.where(lane == c, col, packed)
    idx_ref[...] = packed.T[:SUBLANES, :].astype(jnp.int32)


def _moe_dispatch_kernel(meta_ref, d_ref, x_ref, xs_ref, zero_scr, sem, zsem, *, n_e):
    tm = x_ref.shape[0]

    for t in range(tm):
        for slot in range(2):
            row = d_ref[slot, t]
            pltpu.make_async_copy(x_ref.at[pl.ds(t, 1), :], xs_ref.at[pl.ds(row, 1), :],
                                  sem).start(priority=slot)
    for slot in range(2):
        pltpu.make_async_copy(x_ref, xs_ref.at[pl.ds(0, tm), :], sem).wait()

    @pl.when(pl.program_id(0) == pl.num_programs(0) - 1)
    def _():
        zero_scr[...] = jnp.zeros_like(zero_scr)
        zero_row = zero_scr.at[pl.ds(0, 1), :]
        for e in range(n_e):
            start = meta_ref[e]
            n_pad = meta_ref[n_e + e]

            def fill(r, carry):
                pltpu.make_async_copy(zero_row, xs_ref.at[pl.ds(start + r, 1), :], zsem).start()
                return carry

            def drain(r, carry):
                pltpu.make_async_copy(zero_row, xs_ref.at[pl.ds(0, 1), :], zsem).wait()
                return carry

            lax.fori_loop(0, n_pad, fill, 0)
            lax.fori_loop(0, n_pad, drain, 0)
        used = meta_ref[2 * n_e]
        tile = zero_scr.shape[0]

        def fill_tile(g, carry):
            row0 = pl.multiple_of((used + g) * tile, tile)
            cp = pltpu.make_async_copy(zero_scr, xs_ref.at[pl.ds(row0, tile), :], zsem)
            cp.start()
            cp.wait()
            return carry

        lax.fori_loop(0, meta_ref[2 * n_e + 1], fill_tile, 0)


def _moe_group_kernel(exp_ref, rows_ref, xs_ref, g_ref, wg_ref, wu_ref, wd_ref, ys_ref, acc_scr):
    live = rows_ref[pl.program_id(0)] > 0

    @pl.when(live)
    def _():
        acc_scr[...] = jnp.zeros_like(acc_scr)
        _swiglu_rows(_rms_rows(xs_ref[...], g_ref[...]).astype(BF16), wg_ref, wu_ref, wd_ref, acc_scr)
        ys_ref[...] = acc_scr[...]

    @pl.when(jnp.logical_not(live))
    def _():
        ys_ref[...] = jnp.zeros_like(ys_ref)


def _moe_combine_kernel(d_ref, x_ref, w_ref, ys_ref, o_ref, buf, sem):
    tm = x_ref.shape[0]

    for t in range(tm):
        for slot in range(2):
            row = d_ref[slot, t]
            pltpu.make_async_copy(ys_ref.at[pl.ds(row, 1), :], buf.at[slot, pl.ds(t, 1), :],
                                  sem).start(priority=slot)
    for slot in range(2):
        pltpu.make_async_copy(ys_ref.at[pl.ds(0, tm), :], buf.at[slot], sem).wait()
    w = w_ref[...]
    o_ref[...] = x_ref[...] + w[:, 0:1] * buf[0] + w[:, 1:2] * buf[1]


def _moe_layer(x2, g, w_gate, w_up, w_down, router, *, tr=1024, tm=512):
    t, d = x2.shape
    n_e, _, dff = w_gate.shape
    nt = t // tr
    n_tiles = (2 * t) // tm + n_e
    g_row = g.reshape(1, d).astype(F32)
    r_pad = jnp.pad(router.astype(F32), ((0, 0), (0, LANES - n_e)))
    r_hi = r_pad.astype(BF16)
    r_lo = (r_pad - r_hi.astype(F32)).astype(BF16)

    idx, wts, counts = pl.pallas_call(
        _moe_route_kernel,
        grid=(nt,),
        in_specs=[pl.BlockSpec((tr, d), lambda i: (i, 0)), pl.BlockSpec((1, d), lambda i: (0, 0)),
                  pl.BlockSpec((d, LANES), lambda i: (0, 0)), pl.BlockSpec((d, LANES), lambda i: (0, 0))],
        out_specs=[pl.BlockSpec((None, SUBLANES, tr), lambda i: (i, 0, 0)),
                   pl.BlockSpec((tr, LANES), lambda i: (i, 0)),
                   pl.BlockSpec((1, LANES), lambda i: (0, 0))],
        out_shape=[jax.ShapeDtypeStruct((nt, SUBLANES, tr), jnp.int32),
                   jax.ShapeDtypeStruct((t, LANES), F32),
                   jax.ShapeDtypeStruct((1, LANES), F32)],
        scratch_shapes=[pltpu.VMEM((1, LANES), F32)],
        compiler_params=pltpu.CompilerParams(dimension_semantics=("arbitrary",)),
        name="moe_route",
    )(x2, g_row, r_hi, r_lo)

    cnt = counts[0, :n_e].astype(jnp.int32)
    tiles_e = (cnt + tm - 1) // tm
    ends = jnp.cumsum(tiles_e)
    starts = ends - tiles_e
    used = ends[-1]
    dest = starts[idx[:, 0:2, :]] * tm + idx[:, 2:4, :]
    dest = jnp.concatenate([dest, jnp.zeros((nt, SUBLANES - 2, tr), jnp.int32)], axis=1)
    tile_ids = jnp.arange(n_tiles, dtype=jnp.int32)
    t_exp = jnp.minimum(jnp.searchsorted(ends, tile_ids, side="right"), n_e - 1).astype(jnp.int32)
    t_rows = jnp.where(tile_ids < used,
                       jnp.clip(cnt[t_exp] - (tile_ids - starts[t_exp]) * tm, 0, tm), 0).astype(jnp.int32)
    meta = jnp.concatenate([starts * tm + cnt, tiles_e * tm - cnt,
                            jnp.stack([used, n_tiles - used])]).astype(jnp.int32)

    smem_rows = lambda: pl.BlockSpec((None, SUBLANES, tr), lambda i, *_: (i, 0, 0), memory_space=pltpu.SMEM)
    xs = pl.pallas_call(
        functools.partial(_moe_dispatch_kernel, n_e=n_e),
        grid_spec=pltpu.PrefetchScalarGridSpec(
            num_scalar_prefetch=1,
            grid=(nt,),
            in_specs=[smem_rows(), pl.BlockSpec((tr, d), lambda i, meta: (i, 0))],
            out_specs=pl.BlockSpec(memory_space=pl.ANY),
            scratch_shapes=[pltpu.VMEM((tm, d), F32), pltpu.SemaphoreType.DMA(()),
                            pltpu.SemaphoreType.DMA(())],
        ),
        out_shape=jax.ShapeDtypeStruct((n_tiles * tm, d), F32),
        compiler_params=pltpu.CompilerParams(
            dimension_semantics=("arbitrary",), vmem_limit_bytes=48 * 1024 * 1024),
        name="moe_dispatch",
    )(meta, dest, x2)

    ys = pl.pallas_call(
        _moe_group_kernel,
        grid_spec=pltpu.PrefetchScalarGridSpec(
            num_scalar_prefetch=2,
            grid=(n_tiles,),
            in_specs=[
                pl.BlockSpec((tm, d), lambda i, ex, rows: (i, 0)),
                pl.BlockSpec((1, d), lambda i, ex, rows: (0, 0)),
                _resident((None, d, dff), lambda i, ex, rows: (ex[i], 0, 0)),
                _resident((None, d, dff), lambda i, ex, rows: (ex[i], 0, 0)),
                _resident((None, dff, d), lambda i, ex, rows: (ex[i], 0, 0)),
            ],
            out_specs=pl.BlockSpec((tm, d), lambda i, ex, rows: (i, 0)),
            scratch_shapes=[pltpu.VMEM((tm, d), F32)],
        ),
        out_shape=jax.ShapeDtypeStruct((n_tiles * tm, d), F32),
        compiler_params=pltpu.CompilerParams(
            dimension_semantics=("arbitrary",), vmem_limit_bytes=48 * 1024 * 1024),
        name="moe_group",
    )(t_exp, t_rows, xs, g_row, w_gate.astype(BF16), w_up.astype(BF16), w_down.astype(BF16))

    return pl.pallas_call(
        _moe_combine_kernel,
        grid=(nt,),
        in_specs=[smem_rows(), pl.BlockSpec((tr, d), lambda i: (i, 0)),
                  pl.BlockSpec((tr, LANES), lambda i: (i, 0)), pl.BlockSpec(memory_space=pl.ANY)],
        out_specs=pl.BlockSpec((tr, d), lambda i: (i, 0)),
        out_shape=jax.ShapeDtypeStruct((t, d), F32),
        scratch_shapes=[pltpu.VMEM((2, tr, d), F32), pltpu.SemaphoreType.DMA(())],
        compiler_params=pltpu.CompilerParams(
            dimension_semantics=("arbitrary",), vmem_limit_bytes=48 * 1024 * 1024),
        name="moe_combine",
    )(dest, x2, wts, ys)


def kernel(x, norm_mix, norm_ffn, lru_w_in, lru_conv_w, lru_conv_b, lru_w_a, lru_b_a, lru_w_x, lru_b_x,
           lru_log_lambda, lru_w_out, fox_w_in, fox_b_f, fox_q_gain, fox_k_gain, fox_w_out, ffn_w_gate,
           ffn_w_up, ffn_w_down, moe_router, moe_w_gate, moe_w_up, moe_w_down):
    bsz, s, d = x.shape
    depth = norm_mix.shape[0]
    for i in range(depth):
        j = i // 2
        if i % 2 == 0:
            x = _lru_layer(x, norm_mix[i], lru_w_in[j], lru_conv_w[j], lru_conv_b[j], lru_w_a[j],
                           lru_b_a[j], lru_w_x[j], lru_b_x[j], lru_log_lambda[j], lru_w_out[j])
            x2 = _ffn_layer(x.reshape(bsz * s, d), norm_ffn[i], ffn_w_gate[j], ffn_w_up[j], ffn_w_down[j])
        else:
            q, k, v, gate, ft = _fox_proj(x, norm_mix[i], fox_w_in[j], fox_b_f[j], fox_q_gain[j],
                                          fox_k_gain[j])
            o = _fox_attn(q, k, v, ft, fox_q_gain[j], fox_k_gain[j])
            x2 = _fox_out(x.reshape(bsz * s, d), o.reshape(bsz * s, d), gate.reshape(bsz * s, d),
                          fox_w_out[j])
            x2 = _moe_layer(x2, norm_ffn[i], moe_w_gate[j], moe_w_up[j], moe_w_down[j], moe_router[j])
        x = x2.reshape(bsz, s, d)
    return x
```

```python
import functools

import jax
import jax.numpy as jnp
import numpy as np
from jax import lax
from jax.experimental import pallas as pl
from jax.experimental.pallas import tpu as pltpu

RMS_EPS = 1e-6
LRU_C = 8.0
LRU_BLOCKS = 8
CONV_WIDTH = 4
FOX_HEADS = 16
N_EXPERTS = 8
LANES = 128
SUBLANES = 8
NEG_BIG = -1e30
LOG2E = 1.4426950408889634
ATTN_SHIFT_MAX = 40.0
ATTN_DEAD_LOG2 = 160.0
ATTN_BLOCKS_PER_STEP = 8

F32 = jnp.float32
BF16 = jnp.bfloat16


def _dot(a, b):
    return jnp.dot(a, b, preferred_element_type=F32)


def _rms_rows(x, g):
    ms = jnp.mean(x * x, axis=-1, keepdims=True)
    return x * lax.rsqrt(ms + RMS_EPS) * g


def _sigmoid(x):
    return 1.0 / (1.0 + jnp.exp(-x))


def _softplus(x):
    return jnp.maximum(x, 0.0) + jnp.log1p(jnp.exp(-jnp.abs(x)))


def _gelu_tanh(x):
    return 0.5 * x * (1.0 + jnp.tanh(0.7978845608028654 * (x + 0.044715 * (x * x * x))))


def _split3(x):
    hi = x.astype(BF16)
    r1 = x - hi.astype(F32)
    mid = r1.astype(BF16)
    lo = (r1 - mid.astype(F32)).astype(BF16)
    return hi, mid, lo


def _lru_kernel(x_ref, g_ref, win_ref, cw_ref, cb_ref, wgate_ref, ba_ref, bx_ref, lam_ref, wout_ref,
                o_ref, xb_scr, hy_scr, carry_scr):
    t = pl.program_id(1)
    ts, d = x_ref.shape
    w = d

    @pl.when(t == 0)
    def _():
        xb_scr[0:SUBLANES, :] = jnp.zeros((SUBLANES, w), F32)
        carry_scr[...] = jnp.zeros_like(carry_scr)

    x = x_ref[...]
    hn = _rms_rows(x, g_ref[...]).astype(BF16)
    proj = _dot(hn, win_ref[...])
    xb = proj[:, w:]
    xb_scr[SUBLANES:SUBLANES + ts, :] = xb
    xc = cw_ref[3:4, :] * xb + cb_ref[...]
    for k in range(1, CONV_WIDTH):
        xc = xc + cw_ref[3 - k:4 - k, :] * xb_scr[SUBLANES - k:SUBLANES - k + ts, :]
    xb_scr[0:SUBLANES, :] = xb[ts - SUBLANES:, :]
    xcb = xc.astype(BF16)

    sp = _softplus(-lam_ref[...])
    groups = ts // SUBLANES
    rowmod = lax.broadcasted_iota(jnp.int32, (groups, SUBLANES, LANES), 1)
    for h in range(LRU_BLOCKS):
        sl = slice(h * LANES, (h + 1) * LANES)
        z = _dot(xcb[:, sl], wgate_ref[h])
        r = _sigmoid(z[:, :LANES] + ba_ref[:, sl])
        gi = _sigmoid(z[:, LANES:] + bx_ref[:, sl])
        log_a = (-LRU_C) * r * sp[:, sl]
        a = jnp.exp(log_a)
        b = jnp.sqrt(-jnp.tanh(log_a) * (a * a + 1.0)) * (gi * xc[:, sl])
        a = a.reshape(groups, SUBLANES, LANES)
        b = b.reshape(groups, SUBLANES, LANES)
        for sh in (1, 2, 4):
            keep = rowmod >= sh
            a_s = jnp.where(keep, pltpu.roll(a, sh, 1), 1.0)
            b_s = jnp.where(keep, pltpu.roll(b, sh, 1), 0.0)
            b = a * b_s + b
            a = a * a_s
        carry = carry_scr[h:h + 1, :]
        outs = []
        for gidx in range(groups):
            hg = a[gidx] * carry + b[gidx]
            outs.append(hg)
            carry = hg[SUBLANES - 1:SUBLANES, :]
        carry_scr[h:h + 1, :] = carry
        hseq = jnp.concatenate(outs, axis=0)
        yb = _gelu_tanh(proj[:, sl])
        hy_scr[:, sl] = (hseq * yb).astype(BF16)

    o_ref[...] = x + _dot(hy_scr[...], wout_ref[...])


def _lru_layer(x, g, w_in, conv_w, conv_b, w_a, b_a, w_x, b_x, log_lambda, w_out, *, ts=512):
    bsz, s, d = x.shape
    w = d
    wgate = jnp.concatenate([w_a, w_x], axis=-1).astype(BF16)
    row = lambda v: v.reshape(1, -1).astype(F32)
    const2 = lambda shape: pl.BlockSpec(shape, lambda b, t: (0, 0))
    return pl.pallas_call(
        _lru_kernel,
        grid=(bsz, s // ts),
        in_specs=[
            pl.BlockSpec((None, ts, d), lambda b, t: (b, t, 0)),
            const2((1, d)),
            const2((d, 2 * w)),
            const2((CONV_WIDTH, w)),
            const2((1, w)),
            pl.BlockSpec((LRU_BLOCKS, LANES, 2 * LANES), lambda b, t: (0, 0, 0)),
            const2((1, w)),
            const2((1, w)),
            const2((1, w)),
            const2((w, d)),
        ],
        out_specs=pl.BlockSpec((None, ts, d), lambda b, t: (b, t, 0)),
        out_shape=jax.ShapeDtypeStruct(x.shape, F32),
        scratch_shapes=[
            pltpu.VMEM((ts + SUBLANES, w), F32),
            pltpu.VMEM((ts, w), BF16),
            pltpu.VMEM((LRU_BLOCKS, LANES), F32),
        ],
        compiler_params=pltpu.CompilerParams(
            dimension_semantics=("arbitrary", "arbitrary"), vmem_limit_bytes=48 * 1024 * 1024),
        name="lru_mixer",
    )(x, row(g), w_in.astype(BF16), conv_w.astype(F32), row(conv_b), wgate, row(b_a), row(b_x),
      row(log_lambda), w_out.astype(BF16))


def _fox_proj_kernel(x_ref, g_ref, w_ref, wf_ref, bf_ref, qg_ref, kg_ref, hm_ref,
                     q_ref, k_ref, v_ref, gate_ref, f_ref, carry_scr):
    t = pl.program_id(1)
    ts, d = x_ref.shape

    @pl.when(t == 0)
    def _():
        carry_scr[...] = jnp.zeros_like(carry_scr)

    hn = _rms_rows(x_ref[...], g_ref[...]).astype(BF16)
    proj = _dot(hn, w_ref[...])
    hm = hm_ref[...]

    def head_norm(z, gain):
        ms = _dot((z * z).astype(BF16), hm)
        return z * lax.rsqrt(ms + RMS_EPS) * gain

    q_ref[...] = head_norm(proj[:, :d], qg_ref[...]).astype(BF16)
    k_ref[...] = head_norm(proj[:, d:2 * d], kg_ref[...]).astype(BF16)
    v_ref[...] = proj[:, 2 * d:3 * d].astype(BF16)
    gate_ref[...] = _sigmoid(proj[:, 3 * d:]).astype(BF16)

    f = _dot(hn, wf_ref[...]) + bf_ref[...]
    lf = -_softplus(-f)
    r_i = lax.broadcasted_iota(jnp.int32, (ts, ts), 0)
    c_i = lax.broadcasted_iota(jnp.int32, (ts, ts), 1)
    ltri = jnp.where(c_i <= r_i, 1.0, 0.0).astype(BF16)
    hi, mid, lo = _split3(lf)
    fcum = (_dot(ltri, hi) + _dot(ltri, mid)) + _dot(ltri, lo) + carry_scr[...]
    carry_scr[...] = fcum[ts - 1:ts, :]
    f_ref[...] = fcum * LOG2E


def _fox_proj(x, g, w_in, b_f, q_gain, k_gain, *, ts=256):
    bsz, s, d = x.shape
    hd = d // FOX_HEADS
    w_main = w_in[:, :4 * d].astype(BF16)
    w_f = jnp.pad(w_in[:, 4 * d:], ((0, 0), (0, LANES - FOX_HEADS))).astype(BF16)
    bf_row = jnp.pad(b_f.astype(F32), (0, LANES - FOX_HEADS)).reshape(1, LANES)
    scale = hd ** -0.5
    qg_row = (jnp.tile(q_gain.astype(F32), FOX_HEADS) * (scale * LOG2E)).reshape(1, d)
    kg_row = jnp.tile(k_gain.astype(F32), FOX_HEADS).reshape(1, d)
    head_id = jnp.arange(d) // hd
    hm = jnp.where(head_id[:, None] == head_id[None, :], 1.0 / hd, 0.0).astype(BF16)
    const2 = lambda shape: pl.BlockSpec(shape, lambda b, t: (0, 0))
    tok = pl.BlockSpec((None, ts, d), lambda b, t: (b, t, 0))
    act = jax.ShapeDtypeStruct((bsz, s, d), BF16)
    return pl.pallas_call(
        _fox_proj_kernel,
        grid=(bsz, s // ts),
        in_specs=[tok, const2((1, d)), const2((d, 4 * d)), const2((d, LANES)), const2((1, LANES)),
                  const2((1, d)), const2((1, d)), const2((d, d))],
        out_specs=[tok, tok, tok, tok, pl.BlockSpec((None, ts, LANES), lambda b, t: (b, t, 0))],
        out_shape=[act, act, act, act, jax.ShapeDtypeStruct((bsz, s, LANES), F32)],
        scratch_shapes=[pltpu.VMEM((1, LANES), F32)],
        compiler_params=pltpu.CompilerParams(
            dimension_semantics=("arbitrary", "arbitrary"), vmem_limit_bytes=48 * 1024 * 1024),
        name="fox_proj",
    )(x, g.reshape(1, d).astype(F32), w_main, w_f, bf_row, qg_row, kg_row, hm)


def _aug_lanes(hd):
    base = SUBLANES * hd
    return base, base + 3


def _attn_placement(pairs):
    pq = np.zeros((pairs, 3, LANES, LANES), np.float32)
    pk = np.zeros((pairs, 2, 3, LANES, LANES), np.float32)
    for p in range(pairs):
        for hd in range(2):
            q_lo, k_lo = _aug_lanes(hd)
            for i in range(3):
                pq[p, i, 2 * p + hd, q_lo + i] = 1.0
                pk[p, hd, i, 2 * p + hd, k_lo + i] = 1.0
    return jnp.asarray(pq, BF16), jnp.asarray(pk, BF16)


def _place(x, p_ref, idx):
    hi, mid, lo = _split3(x)
    return (_dot(hi, p_ref[idx + (0,)]) + _dot(mid, p_ref[idx + (1,)])) + _dot(lo, p_ref[idx + (2,)])


def _fox_attn_fast_kernel(js_ref, c_ref, q_ref, k_ref, v_ref, f_ref, pq_ref, pk_ref, o_ref,
                          ka_scr, kb_scr, va_scr, vb_scr, acc_scr, *, chunk, tq):
    tk = tq
    s_len = k_ref.shape[0]
    nq = s_len // tq
    js_pair = ((pl.program_id(0) * pl.num_programs(1) + pl.program_id(1)) * 2) * nq
    half = LANES // 2
    k_scrs = (ka_scr, kb_scr)
    v_scrs = (va_scr, vb_scr)
    sum_lane = (half, half - 1)

    def build(c, carry):
        rows = pl.ds(pl.multiple_of(c * chunk, chunk), chunk)
        kk = k_ref[rows, :]
        vv = v_ref[rows, :]
        lane = lax.broadcasted_iota(jnp.int32, (chunk, LANES), 1)
        nf = -f_ref[rows, :]
        for hd in range(2):
            ones_lo, _ = _aug_lanes(hd)
            aug = _place(nf, pk_ref, (hd,))
            aug = aug + jnp.where((lane >= ones_lo) & (lane < ones_lo + 3), 1.0, 0.0)
            keep = (lane < half) if hd == 0 else (lane >= half)
            k_scrs[hd][rows, 0:LANES] = jnp.where(keep, kk, jnp.zeros_like(kk))
            k_scrs[hd][rows, LANES:2 * LANES] = aug.astype(BF16)
            v_scrs[hd][rows, :] = jnp.where(lane == sum_lane[hd], jnp.ones_like(vv), vv)
        return carry

    lax.fori_loop(0, s_len // chunk, build, 0)
    lax.fori_loop(0, nq, functools.partial(
        _fox_attn_q_block, js_ref=js_ref, js_pair=js_pair, c_ref=c_ref, q_ref=q_ref, f_ref=f_ref, pq_ref=pq_ref,
        o_ref=o_ref, k_scrs=k_scrs, v_scrs=v_scrs, acc_scr=acc_scr, sum_lane=sum_lane, tq=tq, nq=nq), 0)


def _fox_attn_q_block(qi, carry, *, js_ref, js_pair, c_ref, q_ref, f_ref, pq_ref, o_ref, k_scrs, v_scrs, acc_scr,
                      sum_lane, tq, nq):
    tk = tq
    half = LANES // 2
    js_base = js_pair + qi
    q_rows = pl.ds(pl.multiple_of(qi * tq, tq), tq)
    lane_q = lax.broadcasted_iota(jnp.int32, (tq, LANES), 1)
    fq = f_ref[q_rows, :] - c_ref[...]
    augq = _place(fq, pq_ref, ())
    q_ones = ((lane_q >= 3) & (lane_q < 6)) | ((lane_q >= SUBLANES + 3) & (lane_q < SUBLANES + 6))
    augq = augq + jnp.where(q_ones, 1.0, 0.0)
    q_aug = jnp.concatenate([q_ref[q_rows, :], augq.astype(BF16)], axis=1)
    acc_scr[...] = jnp.zeros_like(acc_scr)

    def probs(j, hd, masked=False):
        rows = pl.ds(pl.multiple_of(j * tk, tk), tk)
        s = lax.dot_general(q_aug, k_scrs[hd][rows, :], (((1,), (1,)), ((), ())),
                            preferred_element_type=F32)
        p = jnp.exp2(s)
        if masked:
            r_i = lax.broadcasted_iota(jnp.int32, (tq, tk), 0)
            c_i = lax.broadcasted_iota(jnp.int32, (tq, tk), 1)
            p = jnp.where(c_i <= r_i, p, 0.0)
        return p.astype(BF16), v_scrs[hd][rows, :]

    def attend(j, hd, count):
        total = None
        for u in range(count):
            pu, vu = probs(j + u, hd)
            term = _dot(pu, vu)
            total = term if total is None else total + term
        acc_scr[hd] += total

    for hd in range(2):
        j0 = js_ref[js_base + hd * nq]
        n = qi - j0
        j = j0
        arm = 1
        while arm < ATTN_BLOCKS_PER_STEP:
            bit = n & arm

            @pl.when(bit != 0)
            def _(j=j, arm=arm):
                attend(j, hd, arm)

            j = j + bit
            arm *= 2

        def body(t, carry, j=j):
            attend(j + ATTN_BLOCKS_PER_STEP * t, hd, ATTN_BLOCKS_PER_STEP)
            return carry

        lax.fori_loop(0, lax.shift_right_logical(n, ATTN_BLOCKS_PER_STEP.bit_length() - 1), body, 0)

    for hd in range(2):
        pd, vd = probs(qi, hd, masked=True)
        acc_scr[hd] += _dot(pd, vd)

    acc_a = acc_scr[0]
    acc_b = acc_scr[1]
    oa = acc_a / acc_a[:, sum_lane[0]:sum_lane[0] + 1]
    ob = acc_b / acc_b[:, sum_lane[1]:sum_lane[1] + 1]
    o_ref[q_rows, :] = jnp.where(lane_q < half, oa, ob).astype(o_ref.dtype)
    return carry


def _first_live_block(f_tok, tq):
    bsz, s, _ = f_tok.shape
    nq = s // tq
    f_heads = f_tok[:, :, :FOX_HEADS]
    f_first = f_heads[:, 0::tq, :]
    f_last = f_heads[:, tq - 1::tq, :]
    gap = f_first[:, :, None, :] - f_last[:, None, :, :]
    blk = jnp.arange(nq)
    dead = (gap < -ATTN_DEAD_LOG2) & (blk[None, None, :, None] < blk[None, :, None, None])
    first = jnp.sum(dead, axis=2).astype(jnp.int32)
    return jnp.transpose(first, (0, 2, 1)).reshape(-1)


def _fox_attn_fast(q, k, v, f_tok, shift, *, tq=512, chunk=1024):
    bsz, s, d = q.shape
    pairs = d // LANES
    pq, pk = _attn_placement(pairs)
    c_row = jnp.full((1, LANES), shift * LOG2E, F32)
    first_live = _first_live_block(f_tok, tq)
    return pl.pallas_call(
        functools.partial(_fox_attn_fast_kernel, chunk=chunk, tq=tq),
        grid_spec=pltpu.PrefetchScalarGridSpec(
            num_scalar_prefetch=1,
            grid=(bsz, pairs),
            in_specs=[
                pl.BlockSpec((1, LANES), lambda b, p, js: (0, 0)),
                pl.BlockSpec((None, s, LANES), lambda b, p, js: (b, 0, p)),
                pl.BlockSpec((None, s, LANES), lambda b, p, js: (b, 0, p)),
                pl.BlockSpec((None, s, LANES), lambda b, p, js: (b, 0, p)),
                pl.BlockSpec((None, s, LANES), lambda b, p, js: (b, 0, 0)),
                pl.BlockSpec((None, 3, LANES, LANES), lambda b, p, js: (p, 0, 0, 0)),
                pl.BlockSpec((None, 2, 3, LANES, LANES), lambda b, p, js: (p, 0, 0, 0, 0)),
            ],
            out_specs=pl.BlockSpec((None, s, LANES), lambda b, p, js: (b, 0, p)),
            scratch_shapes=[
                pltpu.VMEM((s, 2 * LANES), BF16),
                pltpu.VMEM((s, 2 * LANES), BF16),
                pltpu.VMEM((s, LANES), BF16),
                pltpu.VMEM((s, LANES), BF16),
                pltpu.VMEM((2, tq, LANES), F32),
            ],
        ),
        out_shape=jax.ShapeDtypeStruct((bsz, s, d), BF16),
        compiler_params=pltpu.CompilerParams(
            dimension_semantics=("arbitrary", "arbitrary"), vmem_limit_bytes=48 * 1024 * 1024),
        name="fox_attn_fast",
    )(first_live, c_row, q, k, v, f_tok, pq, pk)


def _fox_attn_online_kernel(q_ref, k_ref, v_ref, ft_ref, o_ref, ka_scr, kb_scr, m_scr, l_scr, acc_scr):
    qi = pl.program_id(2)
    tq = q_ref.shape[0]
    tk = tq
    half = LANES // 2
    lane = lax.broadcasted_iota(jnp.int32, (1, LANES), 1)

    @pl.when(qi == 0)
    def _():
        kk = k_ref[...]
        lane_k = lax.broadcasted_iota(jnp.int32, kk.shape, 1)
        zero = jnp.zeros_like(kk)
        ka_scr[...] = jnp.where(lane_k < half, kk, zero)
        kb_scr[...] = jnp.where(lane_k >= half, kk, zero)

    m_scr[...] = jnp.full_like(m_scr, NEG_BIG)
    l_scr[...] = jnp.zeros_like(l_scr)
    acc_scr[...] = jnp.zeros_like(acc_scr)
    q = q_ref[...]

    def attend(j, masked):
        start = pl.multiple_of(j * tk, tk)
        v = v_ref[pl.ds(start, tk), :]
        for hd, k_scr in ((0, ka_scr), (1, kb_scr)):
            kb = k_scr[pl.ds(start, tk), :]
            s = lax.dot_general(q, kb, (((1,), (1,)), ((), ())), preferred_element_type=F32)
            s = s - ft_ref[hd, pl.ds(j, 1), :]
            if masked:
                r_i = lax.broadcasted_iota(jnp.int32, (tq, tk), 0)
                c_i = lax.broadcasted_iota(jnp.int32, (tq, tk), 1)
                s = jnp.where(c_i <= r_i, s, NEG_BIG)
            m_prev = m_scr[hd]
            m_new = jnp.maximum(m_prev, jnp.max(s, axis=1, keepdims=True))
            p = jnp.exp2(s - m_new[:, :1])
            alpha = jnp.exp2(m_prev - m_new)
            l_scr[hd] = alpha * l_scr[hd] + jnp.sum(p, axis=1, keepdims=True)
            acc_scr[hd] = alpha * acc_scr[hd] + _dot(p.astype(BF16), v)
            m_scr[hd] = m_new

    def body(j, carry):
        attend(j, False)
        return carry

    lax.fori_loop(0, qi, body, 0)
    attend(qi, True)

    oa = acc_scr[0] / l_scr[0]
    ob = acc_scr[1] / l_scr[1]
    o_ref[...] = jnp.where(lane < half, oa, ob).astype(o_ref.dtype)


def _fox_attn_online(q, k, v, f_tok, *, tq=512):
    bsz, s, d = q.shape
    pairs = d // LANES
    nq = s // tq
    ft5 = jnp.swapaxes(f_tok[:, :, :FOX_HEADS], 1, 2).reshape(bsz, pairs, 2, nq, tq)
    return pl.pallas_call(
        _fox_attn_online_kernel,
        grid=(bsz, pairs, nq),
        in_specs=[
            pl.BlockSpec((None, tq, LANES), lambda b, p, i: (b, i, p)),
            pl.BlockSpec((None, s, LANES), lambda b, p, i: (b, 0, p)),
            pl.BlockSpec((None, s, LANES), lambda b, p, i: (b, 0, p)),
            pl.BlockSpec((None, None, 2, nq, tq), lambda b, p, i: (b, p, 0, 0, 0)),
        ],
        out_specs=pl.BlockSpec((None, tq, LANES), lambda b, p, i: (b, i, p)),
        out_shape=jax.ShapeDtypeStruct((bsz, s, d), BF16),
        scratch_shapes=[
            pltpu.VMEM((s, LANES), BF16),
            pltpu.VMEM((s, LANES), BF16),
            pltpu.VMEM((2, tq, LANES), F32),
            pltpu.VMEM((2, tq, LANES), F32),
            pltpu.VMEM((2, tq, LANES), F32),
        ],
        compiler_params=pltpu.CompilerParams(
            dimension_semantics=("arbitrary", "arbitrary", "arbitrary"),
            vmem_limit_bytes=48 * 1024 * 1024),
        name="fox_attn_online",
    )(q, k, v, ft5)


def _fox_attn(q, k, v, f_tok, q_gain, k_gain):
    hd = q.shape[-1] // FOX_HEADS
    shift = (hd ** 0.5) * jnp.max(jnp.abs(q_gain)).astype(F32) * jnp.max(jnp.abs(k_gain)).astype(F32)
    return lax.cond(shift < ATTN_SHIFT_MAX,
                    lambda: _fox_attn_fast(q, k, v, f_tok, shift),
                    lambda: _fox_attn_online(q, k, v, f_tok))


FFN_CHUNK = 512


def _swiglu_rows(hn, wg_ref, wu_ref, wd_ref, acc_scr):
    dff = wg_ref.shape[-1]
    for c in range(dff // FFN_CHUNK):
        cols = slice(c * FFN_CHUNK, (c + 1) * FFN_CHUNK)
        gg = _dot(hn, wg_ref[:, cols])
        uu = _dot(hn, wu_ref[:, cols])
        act = (gg * _sigmoid(gg)) * uu
        acc_scr[...] += _dot(act.astype(BF16), wd_ref[cols, :])


def _ffn_kernel(x_ref, g_ref, wg_ref, wu_ref, wd_ref, o_ref, acc_scr):
    x = x_ref[...]
    acc_scr[...] = x
    _swiglu_rows(_rms_rows(x, g_ref[...]).astype(BF16), wg_ref, wu_ref, wd_ref, acc_scr)
    o_ref[...] = acc_scr[...]


def _resident(shape, index_map):
    return pl.BlockSpec(shape, index_map, pipeline_mode=pl.Buffered(1))


def _ffn_layer(x2, g, w_gate, w_up, w_down, *, tm=512):
    t, d = x2.shape
    dff = w_gate.shape[1]
    tok = pl.BlockSpec((tm, d), lambda i: (i, 0))
    return pl.pallas_call(
        _ffn_kernel,
        grid=(t // tm,),
        in_specs=[tok, pl.BlockSpec((1, d), lambda i: (0, 0)),
                  _resident((d, dff), lambda i: (0, 0)),
                  _resident((d, dff), lambda i: (0, 0)),
                  _resident((dff, d), lambda i: (0, 0))],
        out_specs=tok,
        out_shape=jax.ShapeDtypeStruct((t, d), F32),
        scratch_shapes=[pltpu.VMEM((tm, d), F32)],
        compiler_params=pltpu.CompilerParams(
            dimension_semantics=("arbitrary",), vmem_limit_bytes=48 * 1024 * 1024),
        name="dense_ffn",
    )(x2, g.reshape(1, d).astype(F32), w_gate.astype(BF16), w_up.astype(BF16), w_down.astype(BF16))


def _top2(logits, lane):
    m1 = jnp.max(logits, axis=1, keepdims=True)
    i1 = jnp.min(jnp.where(logits == m1, lane, LANES), axis=1, keepdims=True)
    rest = jnp.where(lane == i1, NEG_BIG, logits)
    m2 = jnp.max(rest, axis=1, keepdims=True)
    i2 = jnp.min(jnp.where(rest == m2, lane, LANES), axis=1, keepdims=True)
    return m1, i1, m2, i2


def _router_logits(hn, rhi_ref, rlo_ref):
    h_hi = hn.astype(BF16)
    h_lo = (hn - h_hi.astype(F32)).astype(BF16)
    logits = (_dot(h_hi, rhi_ref[...]) + _dot(h_lo, rhi_ref[...])) + _dot(h_hi, rlo_ref[...])
    lane = lax.broadcasted_iota(jnp.int32, logits.shape, 1)
    return jnp.where(lane < N_EXPERTS, logits, NEG_BIG), lane


def _moe_route_kernel(x_ref, o_ref, gate_ref, wout_ref, g_ref, rhi_ref, rlo_ref,
                      x2_ref, idx_ref, w_ref, cnt_ref, carry_scr):
    @pl.when(pl.program_id(0) == 0)
    def _():
        carry_scr[...] = jnp.zeros_like(carry_scr)

    tm = x_ref.shape[0]
    og = (o_ref[...].astype(F32) * gate_ref[...].astype(F32)).astype(BF16)
    x2 = x_ref[...] + _dot(og, wout_ref[...])
    x2_ref[...] = x2
    hn = _rms_rows(x2, g_ref[...])
    logits, lane = _router_logits(hn, rhi_ref, rlo_ref)
    m1, i1, m2, i2 = _top2(logits, lane)
    e2 = jnp.exp(m2 - m1)
    w1 = 1.0 / (1.0 + e2)
    w2 = e2 / (1.0 + e2)
    oh1 = lane == i1
    oh2 = lane == i2
    member = jnp.where(oh1 | oh2, 1.0, 0.0)
    r_i = lax.broadcasted_iota(jnp.int32, (tm, tm), 0)
    c_i = lax.broadcasted_iota(jnp.int32, (tm, tm), 1)
    lstrict = jnp.where(c_i < r_i, 1.0, 0.0).astype(BF16)
    before = _dot(lstrict, member.astype(BF16)) + carry_scr[...]
    rank1 = jnp.sum(jnp.where(oh1, before, 0.0), axis=1, keepdims=True)
    rank2 = jnp.sum(jnp.where(oh2, before, 0.0), axis=1, keepdims=True)
    carry_scr[...] += jnp.sum(member, axis=0, keepdims=True)
    cnt_ref[...] = carry_scr[...]
    w_ref[...] = jnp.where(lane == 0, w1, jnp.where(lane == 1, w2, 0.0))
    cols = (i1.astype(F32), i2.astype(F32), rank1, rank2)
    packed = jnp.zeros((tm, LANES), F32)
    for c, col in enumerate(cols):
        packed = jnp.where(lane == c, col, packed)
    idx_ref[...] = packed.T[:SUBLANES, :].astype(jnp.int32)


def _moe_dispatch_kernel(meta_ref, d_ref, x_ref, xs_ref, zero_scr, sem, zsem, *, n_e):
    tm = x_ref.shape[0]

    for t in range(tm):
        for slot in range(2):
            row = d_ref[slot, t]
            pltpu.make_async_copy(x_ref.at[pl.ds(t, 1), :], xs_ref.at[pl.ds(row, 1), :],
                                  sem).start(priority=slot)
    for slot in range(2):
        pltpu.make_async_copy(x_ref, xs_ref.at[pl.ds(0, tm), :], sem).wait()

    @pl.when(pl.program_id(0) == pl.num_programs(0) - 1)
    def _():
        zero_scr[...] = jnp.zeros_like(zero_scr)
        zero_row = zero_scr.at[pl.ds(0, 1), :]
        for e in range(n_e):
            start = meta_ref[e]
            n_pad = meta_ref[n_e + e]

            def fill(r, carry):
                pltpu.make_async_copy(zero_row, xs_ref.at[pl.ds(start + r, 1), :], zsem).start()
                return carry

            def drain(r, carry):
                pltpu.make_async_copy(zero_row, xs_ref.at[pl.ds(0, 1), :], zsem).wait()
                return carry

            lax.fori_loop(0, n_pad, fill, 0)
            lax.fori_loop(0, n_pad, drain, 0)
        used = meta_ref[2 * n_e]
        tile = zero_scr.shape[0]

        def fill_tile(g, carry):
            row0 = pl.multiple_of((used + g) * tile, tile)
            cp = pltpu.make_async_copy(zero_scr, xs_ref.at[pl.ds(row0, tile), :], zsem)
            cp.start()
            cp.wait()
            return carry

        lax.fori_loop(0, meta_ref[2 * n_e + 1], fill_tile, 0)


def _moe_group_kernel(exp_ref, rows_ref, xs_ref, g_ref, wg_ref, wu_ref, wd_ref, ys_ref, acc_scr):
    live = rows_ref[pl.program_id(0)] > 0

    @pl.when(live)
    def _():
        acc_scr[...] = jnp.zeros_like(acc_scr)
        _swiglu_rows(_rms_rows(xs_ref[...], g_ref[...]).astype(BF16), wg_ref, wu_ref, wd_ref, acc_scr)
        ys_ref[...] = acc_scr[...]

    @pl.when(jnp.logical_not(live))
    def _():
        ys_ref[...] = jnp.zeros_like(ys_ref)


def _moe_combine_kernel(d_ref, x_ref, w_ref, ys_ref, o_ref, buf, sem):
    tm = x_ref.shape[0]

    for t in range(tm):
        for slot in range(2):
            row = d_ref[slot, t]
            pltpu.make_async_copy(ys_ref.at[pl.ds(row, 1), :], buf.at[slot, pl.ds(t, 1), :],
                                  sem).start(priority=slot)
    for slot in range(2):
        pltpu.make_async_copy(ys_ref.at[pl.ds(0, tm), :], buf.at[slot], sem).wait()
    w = w_ref[...]
    o_ref[...] = x_ref[...] + w[:, 0:1] * buf[0] + w[:, 1:2] * buf[1]


def _fox_out_moe_layer(x1, o2, gate2, w_out, g, w_gate, w_up, w_down, router, *, tr=512, tm=512):
    t, d = x1.shape
    n_e, _, dff = w_gate.shape
    nt = t // tr
    n_tiles = (2 * t) // tm + n_e
    g_row = g.reshape(1, d).astype(F32)
    r_pad = jnp.pad(router.astype(F32), ((0, 0), (0, LANES - n_e)))
    r_hi = r_pad.astype(BF16)
    r_lo = (r_pad - r_hi.astype(F32)).astype(BF16)

    tok = pl.BlockSpec((tr, d), lambda i: (i, 0))
    x2, idx, wts, counts = pl.pallas_call(
        _moe_route_kernel,
        grid=(nt,),
        in_specs=[tok, tok, tok, _resident((d, d), lambda i: (0, 0)), pl.BlockSpec((1, d), lambda i: (0, 0)),
                  pl.BlockSpec((d, LANES), lambda i: (0, 0)), pl.BlockSpec((d, LANES), lambda i: (0, 0))],
        out_specs=[tok,
                   pl.BlockSpec((None, SUBLANES, tr), lambda i: (i, 0, 0)),
                   pl.BlockSpec((tr, LANES), lambda i: (i, 0)),
                   pl.BlockSpec((1, LANES), lambda i: (0, 0))],
        out_shape=[jax.ShapeDtypeStruct((t, d), F32),
                   jax.ShapeDtypeStruct((nt, SUBLANES, tr), jnp.int32),
                   jax.ShapeDtypeStruct((t, LANES), F32),
                   jax.ShapeDtypeStruct((1, LANES), F32)],
        scratch_shapes=[pltpu.VMEM((1, LANES), F32)],
        compiler_params=pltpu.CompilerParams(
            dimension_semantics=("arbitrary",), vmem_limit_bytes=48 * 1024 * 1024),
        name="fox_out_moe_route",
    )(x1, o2, gate2, w_out.astype(BF16), g_row, r_hi, r_lo)

    cnt = counts[0, :n_e].astype(jnp.int32)
    tiles_e = (cnt + tm - 1) // tm
    ends = jnp.cumsum(tiles_e)
    starts = ends - tiles_e
    used = ends[-1]
    dest = starts[idx[:, 0:2, :]] * tm + idx[:, 2:4, :]
    dest = jnp.concatenate([dest, jnp.zeros((nt, SUBLANES - 2, tr), jnp.int32)], axis=1)
    tile_ids = jnp.arange(n_tiles, dtype=jnp.int32)
    t_exp = jnp.minimum(jnp.searchsorted(ends, tile_ids, side="right"), n_e - 1).astype(jnp.int32)
    t_rows = jnp.where(tile_ids < used,
                       jnp.clip(cnt[t_exp] - (tile_ids - starts[t_exp]) * tm, 0, tm), 0).astype(jnp.int32)
    meta = jnp.concatenate([starts * tm + cnt, tiles_e * tm - cnt,
                            jnp.stack([used, n_tiles - used])]).astype(jnp.int32)

    smem_rows = lambda: pl.BlockSpec((None, SUBLANES, tr), lambda i, *_: (i, 0, 0), memory_space=pltpu.SMEM)
    xs = pl.pallas_call(
        functools.partial(_moe_dispatch_kernel, n_e=n_e),
        grid_spec=pltpu.PrefetchScalarGridSpec(
            num_scalar_prefetch=1,
            grid=(nt,),
            in_specs=[smem_rows(), pl.BlockSpec((tr, d), lambda i, meta: (i, 0))],
            out_specs=pl.BlockSpec(memory_space=pl.ANY),
            scratch_shapes=[pltpu.VMEM((tm, d), F32), pltpu.SemaphoreType.DMA(()),
                            pltpu.SemaphoreType.DMA(())],
        ),
        out_shape=jax.ShapeDtypeStruct((n_tiles * tm, d), F32),
        compiler_params=pltpu.CompilerParams(
            dimension_semantics=("arbitrary",), vmem_limit_bytes=48 * 1024 * 1024),
        name="moe_dispatch",
    )(meta, dest, x2)

    ys = pl.pallas_call(
        _moe_group_kernel,
        grid_spec=pltpu.PrefetchScalarGridSpec(
            num_scalar_prefetch=2,
            grid=(n_tiles,),
            in_specs=[
                pl.BlockSpec((tm, d), lambda i, ex, rows: (i, 0)),
                pl.BlockSpec((1, d), lambda i, ex, rows: (0, 0)),
                _resident((None, d, dff), lambda i, ex, rows: (ex[i], 0, 0)),
                _resident((None, d, dff), lambda i, ex, rows: (ex[i], 0, 0)),
                _resident((None, dff, d), lambda i, ex, rows: (ex[i], 0, 0)),
            ],
            out_specs=pl.BlockSpec((tm, d), lambda i, ex, rows: (i, 0)),
            scratch_shapes=[pltpu.VMEM((tm, d), F32)],
        ),
        out_shape=jax.ShapeDtypeStruct((n_tiles * tm, d), F32),
        compiler_params=pltpu.CompilerParams(
            dimension_semantics=("arbitrary",), vmem_limit_bytes=48 * 1024 * 1024),
        name="moe_group",
    )(t_exp, t_rows, xs, g_row, w_gate.astype(BF16), w_up.astype(BF16), w_down.astype(BF16))

    return pl.pallas_call(
        _moe_combine_kernel,
        grid=(nt,),
        in_specs=[smem_rows(), pl.BlockSpec((tr, d), lambda i: (i, 0)),
                  pl.BlockSpec((tr, LANES), lambda i: (i, 0)), pl.BlockSpec(memory_space=pl.ANY)],
        out_specs=pl.BlockSpec((tr, d), lambda i: (i, 0)),
        out_shape=jax.ShapeDtypeStruct((t, d), F32),
        scratch_shapes=[pltpu.VMEM((2, tr, d), F32), pltpu.SemaphoreType.DMA(())],
        compiler_params=pltpu.CompilerParams(
            dimension_semantics=("arbitrary",), vmem_limit_bytes=48 * 1024 * 1024),
        name="moe_combine",
    )(dest, x2, wts, ys)


def kernel(x, norm_mix, norm_ffn, lru_w_in, lru_conv_w, lru_conv_b, lru_w_a, lru_b_a, lru_w_x, lru_b_x,
           lru_log_lambda, lru_w_out, fox_w_in, fox_b_f, fox_q_gain, fox_k_gain, fox_w_out, ffn_w_gate,
           ffn_w_up, ffn_w_down, moe_router, moe_w_gate, moe_w_up, moe_w_down):
    bsz, s, d = x.shape
    depth = norm_mix.shape[0]
    for i in range(depth):
        j = i // 2
        if i % 2 == 0:
            x = _lru_layer(x, norm_mix[i], lru_w_in[j], lru_conv_w[j], lru_conv_b[j], lru_w_a[j],
                           lru_b_a[j], lru_w_x[j], lru_b_x[j], lru_log_lambda[j], lru_w_out[j])
            x2 = _ffn_layer(x.reshape(bsz * s, d), norm_ffn[i], ffn_w_gate[j], ffn_w_up[j], ffn_w_down[j])
        else:
            q, k, v, gate, ft = _fox_proj(x, norm_mix[i], fox_w_in[j], fox_b_f[j], fox_q_gain[j],
                                          fox_k_gain[j])
            o = _fox_attn(q, k, v, ft, fox_q_gain[j], fox_k_gain[j])
            x2 = _fox_out_moe_layer(x.reshape(bsz * s, d), o.reshape(bsz * s, d), gate.reshape(bsz * s, d),
                                    fox_w_out[j], norm_ffn[i], moe_w_gate[j], moe_w_up[j], moe_w_down[j],
                                    moe_router[j])
        x = x2.reshape(bsz, s, d)
    return x
```

```python
import functools

import jax
import jax.numpy as jnp
import numpy as np
from jax import lax
from jax.experimental import pallas as pl
from jax.experimental.pallas import tpu as pltpu

RMS_EPS = 1e-6
LRU_C = 8.0
LRU_BLOCKS = 8
CONV_WIDTH = 4
FOX_HEADS = 16
N_EXPERTS = 8
LANES = 128
SUBLANES = 8
NEG_BIG = -1e30
LOG2E = 1.4426950408889634
ATTN_SHIFT_MAX = 40.0
ATTN_DEAD_LOG2 = 160.0
ATTN_BLOCKS_PER_STEP = 8

F32 = jnp.float32
BF16 = jnp.bfloat16


def _dot(a, b):
    return jnp.dot(a, b, preferred_element_type=F32)


def _rms_rows(x, g):
    ms = jnp.mean(x * x, axis=-1, keepdims=True)
    return x * lax.rsqrt(ms + RMS_EPS) * g


def _sigmoid(x):
    return 1.0 / (1.0 + jnp.exp(-x))


def _softplus(x):
    return jnp.maximum(x, 0.0) + jnp.log1p(jnp.exp(-jnp.abs(x)))


def _gelu_tanh(x):
    return 0.5 * x * (1.0 + jnp.tanh(0.7978845608028654 * (x + 0.044715 * (x * x * x))))


def _split3(x):
    hi = x.astype(BF16)
    r1 = x - hi.astype(F32)
    mid = r1.astype(BF16)
    lo = (r1 - mid.astype(F32)).astype(BF16)
    return hi, mid, lo


def _lru_kernel(x_ref, g_ref, win_ref, cw_ref, cb_ref, wgate_ref, ba_ref, bx_ref, lam_ref, wout_ref,
                o_ref, xb_scr, hy_scr, carry_scr):
    t = pl.program_id(1)
    ts, d = x_ref.shape
    w = d

    @pl.when(t == 0)
    def _():
        xb_scr[0:SUBLANES, :] = jnp.zeros((SUBLANES, w), F32)
        carry_scr[...] = jnp.zeros_like(carry_scr)

    x = x_ref[...]
    hn = _rms_rows(x, g_ref[...]).astype(BF16)
    proj = _dot(hn, win_ref[...])
    xb = proj[:, w:]
    xb_scr[SUBLANES:SUBLANES + ts, :] = xb
    xc = cw_ref[3:4, :] * xb + cb_ref[...]
    for k in range(1, CONV_WIDTH):
        xc = xc + cw_ref[3 - k:4 - k, :] * xb_scr[SUBLANES - k:SUBLANES - k + ts, :]
    xb_scr[0:SUBLANES, :] = xb[ts - SUBLANES:, :]
    xcb = xc.astype(BF16)

    sp = _softplus(-lam_ref[...])
    groups = ts // SUBLANES
    rowmod = lax.broadcasted_iota(jnp.int32, (groups, SUBLANES, LANES), 1)
    for h in range(LRU_BLOCKS):
        sl = slice(h * LANES, (h + 1) * LANES)
        z = _dot(xcb[:, sl], wgate_ref[h])
        r = _sigmoid(z[:, :LANES] + ba_ref[:, sl])
        gi = _sigmoid(z[:, LANES:] + bx_ref[:, sl])
        log_a = (-LRU_C) * r * sp[:, sl]
        a = jnp.exp(log_a)
        b = jnp.sqrt(-jnp.tanh(log_a) * (a * a + 1.0)) * (gi * xc[:, sl])
        a = a.reshape(groups, SUBLANES, LANES)
        b = b.reshape(groups, SUBLANES, LANES)
        for sh in (1, 2, 4):
            keep = rowmod >= sh
            a_s = jnp.where(keep, pltpu.roll(a, sh, 1), 1.0)
            b_s = jnp.where(keep, pltpu.roll(b, sh, 1), 0.0)
            b = a * b_s + b
            a = a * a_s
        carry = carry_scr[h:h + 1, :]
        outs = []
        for gidx in range(groups):
            hg = a[gidx] * carry + b[gidx]
            outs.append(hg)
            carry = hg[SUBLANES - 1:SUBLANES, :]
        carry_scr[h:h + 1, :] = carry
        hseq = jnp.concatenate(outs, axis=0)
        yb = _gelu_tanh(proj[:, sl])
        hy_scr[:, sl] = (hseq * yb).astype(BF16)

    o_ref[...] = x + _dot(hy_scr[...], wout_ref[...])


def _lru_layer(x, g, w_in, conv_w, conv_b, w_a, b_a, w_x, b_x, log_lambda, w_out, *, ts=512):
    bsz, s, d = x.shape
    w = d
    wgate = jnp.concatenate([w_a, w_x], axis=-1).astype(BF16)
    row = lambda v: v.reshape(1, -1).astype(F32)
    const2 = lambda shape: pl.BlockSpec(shape, lambda b, t: (0, 0))
    return pl.pallas_call(
        _lru_kernel,
        grid=(bsz, s // ts),
        in_specs=[
            pl.BlockSpec((None, ts, d), lambda b, t: (b, t, 0)),
            const2((1, d)),
            const2((d, 2 * w)),
            const2((CONV_WIDTH, w)),
            const2((1, w)),
            pl.BlockSpec((LRU_BLOCKS, LANES, 2 * LANES), lambda b, t: (0, 0, 0)),
            const2((1, w)),
            const2((1, w)),
            const2((1, w)),
            const2((w, d)),
        ],
        out_specs=pl.BlockSpec((None, ts, d), lambda b, t: (b, t, 0)),
        out_shape=jax.ShapeDtypeStruct(x.shape, F32),
        scratch_shapes=[
            pltpu.VMEM((ts + SUBLANES, w), F32),
            pltpu.VMEM((ts, w), BF16),
            pltpu.VMEM((LRU_BLOCKS, LANES), F32),
        ],
        compiler_params=pltpu.CompilerParams(
            dimension_semantics=("arbitrary", "arbitrary"), vmem_limit_bytes=48 * 1024 * 1024),
        name="lru_mixer",
    )(x, row(g), w_in.astype(BF16), conv_w.astype(F32), row(conv_b), wgate, row(b_a), row(b_x),
      row(log_lambda), w_out.astype(BF16))


def _fox_proj_kernel(x_ref, g_ref, w_ref, wf_ref, bf_ref, qg_ref, kg_ref, hm_ref,
                     q_ref, k_ref, v_ref, gate_ref, f_ref, carry_scr):
    t = pl.program_id(1)
    ts, d = x_ref.shape

    @pl.when(t == 0)
    def _():
        carry_scr[...] = jnp.zeros_like(carry_scr)

    hn = _rms_rows(x_ref[...], g_ref[...]).astype(BF16)
    proj = _dot(hn, w_ref[...])
    hm = hm_ref[...]

    def head_norm(z, gain):
        ms = _dot((z * z).astype(BF16), hm)
        return z * lax.rsqrt(ms + RMS_EPS) * gain

    q_ref[...] = head_norm(proj[:, :d], qg_ref[...]).astype(BF16)
    k_ref[...] = head_norm(proj[:, d:2 * d], kg_ref[...]).astype(BF16)
    v_ref[...] = proj[:, 2 * d:3 * d].astype(BF16)
    gate_ref[...] = _sigmoid(proj[:, 3 * d:]).astype(BF16)

    f = _dot(hn, wf_ref[...]) + bf_ref[...]
    lf = -_softplus(-f)
    r_i = lax.broadcasted_iota(jnp.int32, (ts, ts), 0)
    c_i = lax.broadcasted_iota(jnp.int32, (ts, ts), 1)
    ltri = jnp.where(c_i <= r_i, 1.0, 0.0).astype(BF16)
    hi, mid, lo = _split3(lf)
    fcum = (_dot(ltri, hi) + _dot(ltri, mid)) + _dot(ltri, lo) + carry_scr[...]
    carry_scr[...] = fcum[ts - 1:ts, :]
    f_ref[...] = fcum * LOG2E


def _fox_proj(x, g, w_in, b_f, q_gain, k_gain, *, ts=256):
    bsz, s, d = x.shape
    hd = d // FOX_HEADS
    w_main = w_in[:, :4 * d].astype(BF16)
    w_f = jnp.pad(w_in[:, 4 * d:], ((0, 0), (0, LANES - FOX_HEADS))).astype(BF16)
    bf_row = jnp.pad(b_f.astype(F32), (0, LANES - FOX_HEADS)).reshape(1, LANES)
    scale = hd ** -0.5
    qg_row = (jnp.tile(q_gain.astype(F32), FOX_HEADS) * (scale * LOG2E)).reshape(1, d)
    kg_row = jnp.tile(k_gain.astype(F32), FOX_HEADS).reshape(1, d)
    head_id = jnp.arange(d) // hd
    hm = jnp.where(head_id[:, None] == head_id[None, :], 1.0 / hd, 0.0).astype(BF16)
    const2 = lambda shape: pl.BlockSpec(shape, lambda b, t: (0, 0))
    tok = pl.BlockSpec((None, ts, d), lambda b, t: (b, t, 0))
    act = jax.ShapeDtypeStruct((bsz, s, d), BF16)
    return pl.pallas_call(
        _fox_proj_kernel,
        grid=(bsz, s // ts),
        in_specs=[tok, const2((1, d)), const2((d, 4 * d)), const2((d, LANES)), const2((1, LANES)),
                  const2((1, d)), const2((1, d)), const2((d, d))],
        out_specs=[tok, tok, tok, tok, pl.BlockSpec((None, ts, LANES), lambda b, t: (b, t, 0))],
        out_shape=[act, act, act, act, jax.ShapeDtypeStruct((bsz, s, LANES), F32)],
        scratch_shapes=[pltpu.VMEM((1, LANES), F32)],
        compiler_params=pltpu.CompilerParams(
            dimension_semantics=("arbitrary", "arbitrary"), vmem_limit_bytes=48 * 1024 * 1024),
        name="fox_proj",
    )(x, g.reshape(1, d).astype(F32), w_main, w_f, bf_row, qg_row, kg_row, hm)


def _aug_lanes(hd):
    base = SUBLANES * hd
    return base, base + 3


def _attn_placement(pairs):
    pq = np.zeros((pairs, 3, LANES, LANES), np.float32)
    pk = np.zeros((pairs, 2, 3, LANES, LANES), np.float32)
    for p in range(pairs):
        for hd in range(2):
            q_lo, k_lo = _aug_lanes(hd)
            for i in range(3):
                pq[p, i, 2 * p + hd, q_lo + i] = 1.0
                pk[p, hd, i, 2 * p + hd, k_lo + i] = 1.0
    return jnp.asarray(pq, BF16), jnp.asarray(pk, BF16)


def _place(x, p_ref, idx):
    hi, mid, lo = _split3(x)
    return (_dot(hi, p_ref[idx + (0,)]) + _dot(mid, p_ref[idx + (1,)])) + _dot(lo, p_ref[idx + (2,)])


def _fox_attn_fast_kernel(js_ref, c_ref, q_ref, k_ref, v_ref, f_ref, pq_ref, pk_ref, o_ref,
                          ka_scr, kb_scr, va_scr, vb_scr, acc_scr, *, chunk, tq):
    tk = tq
    s_len = k_ref.shape[0]
    nq = s_len // tq
    js_pair = ((pl.program_id(0) * pl.num_programs(1) + pl.program_id(1)) * 2) * nq
    half = LANES // 2
    k_scrs = (ka_scr, kb_scr)
    v_scrs = (va_scr, vb_scr)
    sum_lane = (half, half - 1)

    def build(c, carry):
        rows = pl.ds(pl.multiple_of(c * chunk, chunk), chunk)
        kk = k_ref[rows, :]
        vv = v_ref[rows, :]
        lane = lax.broadcasted_iota(jnp.int32, (chunk, LANES), 1)
        nf = -f_ref[rows, :]
        for hd in range(2):
            ones_lo, _ = _aug_lanes(hd)
            aug = _place(nf, pk_ref, (hd,))
            aug = aug + jnp.where((lane >= ones_lo) & (lane < ones_lo + 3), 1.0, 0.0)
            keep = (lane < half) if hd == 0 else (lane >= half)
            k_scrs[hd][rows, 0:LANES] = jnp.where(keep, kk, jnp.zeros_like(kk))
            k_scrs[hd][rows, LANES:2 * LANES] = aug.astype(BF16)
            v_scrs[hd][rows, :] = jnp.where(lane == sum_lane[hd], jnp.ones_like(vv), vv)
        return carry

    lax.fori_loop(0, s_len // chunk, build, 0)
    lax.fori_loop(0, nq, functools.partial(
        _fox_attn_q_block, js_ref=js_ref, js_pair=js_pair, c_ref=c_ref, q_ref=q_ref, f_ref=f_ref, pq_ref=pq_ref,
        o_ref=o_ref, k_scrs=k_scrs, v_scrs=v_scrs, acc_scr=acc_scr, sum_lane=sum_lane, tq=tq, nq=nq), 0)


def _fox_attn_q_block(qi, carry, *, js_ref, js_pair, c_ref, q_ref, f_ref, pq_ref, o_ref, k_scrs, v_scrs, acc_scr,
                      sum_lane, tq, nq):
    tk = tq
    half = LANES // 2
    js_base = js_pair + qi
    q_rows = pl.ds(pl.multiple_of(qi * tq, tq), tq)
    lane_q = lax.broadcasted_iota(jnp.int32, (tq, LANES), 1)
    fq = f_ref[q_rows, :] - c_ref[...]
    augq = _place(fq, pq_ref, ())
    q_ones = ((lane_q >= 3) & (lane_q < 6)) | ((lane_q >= SUBLANES + 3) & (lane_q < SUBLANES + 6))
    augq = augq + jnp.where(q_ones, 1.0, 0.0)
    q_aug = jnp.concatenate([q_ref[q_rows, :], augq.astype(BF16)], axis=1)
    acc_scr[...] = jnp.zeros_like(acc_scr)

    def probs(j, hd, masked=False):
        rows = pl.ds(pl.multiple_of(j * tk, tk), tk)
        s = lax.dot_general(q_aug, k_scrs[hd][rows, :], (((1,), (1,)), ((), ())),
                            preferred_element_type=F32)
        p = jnp.exp2(s)
        if masked:
            r_i = lax.broadcasted_iota(jnp.int32, (tq, tk), 0)
            c_i = lax.broadcasted_iota(jnp.int32, (tq, tk), 1)
            p = jnp.where(c_i <= r_i, p, 0.0)
        return p.astype(BF16), v_scrs[hd][rows, :]

    def attend(j, hd, count):
        total = None
        for u in range(count):
            pu, vu = probs(j + u, hd)
            term = _dot(pu, vu)
            total = term if total is None else total + term
        acc_scr[hd] += total

    for hd in range(2):
        j0 = js_ref[js_base + hd * nq]
        n = qi - j0
        j = j0
        arm = 1
        while arm < ATTN_BLOCKS_PER_STEP:
            bit = n & arm

            @pl.when(bit != 0)
            def _(j=j, arm=arm):
                attend(j, hd, arm)

            j = j + bit
            arm *= 2

        def body(t, carry, j=j):
            attend(j + ATTN_BLOCKS_PER_STEP * t, hd, ATTN_BLOCKS_PER_STEP)
            return carry

        lax.fori_loop(0, lax.shift_right_logical(n, ATTN_BLOCKS_PER_STEP.bit_length() - 1), body, 0)

    for hd in range(2):
        pd, vd = probs(qi, hd, masked=True)
        acc_scr[hd] += _dot(pd, vd)

    acc_a = acc_scr[0]
    acc_b = acc_scr[1]
    oa = acc_a / acc_a[:, sum_lane[0]:sum_lane[0] + 1]
    ob = acc_b / acc_b[:, sum_lane[1]:sum_lane[1] + 1]
    o_ref[q_rows, :] = jnp.where(lane_q < half, oa, ob).astype(o_ref.dtype)
    return carry


def _first_live_block(f_tok, tq):
    bsz, s, _ = f_tok.shape
    nq = s // tq
    f_heads = f_tok[:, :, :FOX_HEADS]
    f_first = f_heads[:, 0::tq, :]
    f_last = f_heads[:, tq - 1::tq, :]
    gap = f_first[:, :, None, :] - f_last[:, None, :, :]
    blk = jnp.arange(nq)
    dead = (gap < -ATTN_DEAD_LOG2) & (blk[None, None, :, None] < blk[None, :, None, None])
    first = jnp.sum(dead, axis=2).astype(jnp.int32)
    return jnp.transpose(first, (0, 2, 1)).reshape(-1)


def _fox_attn_fast(q, k, v, f_tok, shift, *, tq=512, chunk=1024):
    bsz, s, d = q.shape
    pairs = d // LANES
    pq, pk = _attn_placement(pairs)
    c_row = jnp.full((1, LANES), shift * LOG2E, F32)
    first_live = _first_live_block(f_tok, tq)
    return pl.pallas_call(
        functools.partial(_fox_attn_fast_kernel, chunk=chunk, tq=tq),
        grid_spec=pltpu.PrefetchScalarGridSpec(
            num_scalar_prefetch=1,
            grid=(bsz, pairs),
            in_specs=[
                pl.BlockSpec((1, LANES), lambda b, p, js: (0, 0)),
                pl.BlockSpec((None, s, LANES), lambda b, p, js: (b, 0, p)),
                pl.BlockSpec((None, s, LANES), lambda b, p, js: (b, 0, p)),
                pl.BlockSpec((None, s, LANES), lambda b, p, js: (b, 0, p)),
                pl.BlockSpec((None, s, LANES), lambda b, p, js: (b, 0, 0)),
                pl.BlockSpec((None, 3, LANES, LANES), lambda b, p, js: (p, 0, 0, 0)),
                pl.BlockSpec((None, 2, 3, LANES, LANES), lambda b, p, js: (p, 0, 0, 0, 0)),
            ],
            out_specs=pl.BlockSpec((None, s, LANES), lambda b, p, js: (b, 0, p)),
            scratch_shapes=[
                pltpu.VMEM((s, 2 * LANES), BF16),
                pltpu.VMEM((s, 2 * LANES), BF16),
                pltpu.VMEM((s, LANES), BF16),
                pltpu.VMEM((s, LANES), BF16),
                pltpu.VMEM((2, tq, LANES), F32),
            ],
        ),
        out_shape=jax.ShapeDtypeStruct((bsz, s, d), BF16),
        compiler_params=pltpu.CompilerParams(
            dimension_semantics=("arbitrary", "arbitrary"), vmem_limit_bytes=48 * 1024 * 1024),
        name="fox_attn_fast",
    )(first_live, c_row, q, k, v, f_tok, pq, pk)


def _fox_attn_online_kernel(q_ref, k_ref, v_ref, ft_ref, o_ref, ka_scr, kb_scr, m_scr, l_scr, acc_scr):
    qi = pl.program_id(2)
    tq = q_ref.shape[0]
    tk = tq
    half = LANES // 2
    lane = lax.broadcasted_iota(jnp.int32, (1, LANES), 1)

    @pl.when(qi == 0)
    def _():
        kk = k_ref[...]
        lane_k = lax.broadcasted_iota(jnp.int32, kk.shape, 1)
        zero = jnp.zeros_like(kk)
        ka_scr[...] = jnp.where(lane_k < half, kk, zero)
        kb_scr[...] = jnp.where(lane_k >= half, kk, zero)

    m_scr[...] = jnp.full_like(m_scr, NEG_BIG)
    l_scr[...] = jnp.zeros_like(l_scr)
    acc_scr[...] = jnp.zeros_like(acc_scr)
    q = q_ref[...]

    def attend(j, masked):
        start = pl.multiple_of(j * tk, tk)
        v = v_ref[pl.ds(start, tk), :]
        for hd, k_scr in ((0, ka_scr), (1, kb_scr)):
            kb = k_scr[pl.ds(start, tk), :]
            s = lax.dot_general(q, kb, (((1,), (1,)), ((), ())), preferred_element_type=F32)
            s = s - ft_ref[hd, pl.ds(j, 1), :]
            if masked:
                r_i = lax.broadcasted_iota(jnp.int32, (tq, tk), 0)
                c_i = lax.broadcasted_iota(jnp.int32, (tq, tk), 1)
                s = jnp.where(c_i <= r_i, s, NEG_BIG)
            m_prev = m_scr[hd]
            m_new = jnp.maximum(m_prev, jnp.max(s, axis=1, keepdims=True))
            p = jnp.exp2(s - m_new[:, :1])
            alpha = jnp.exp2(m_prev - m_new)
            l_scr[hd] = alpha * l_scr[hd] + jnp.sum(p, axis=1, keepdims=True)
            acc_scr[hd] = alpha * acc_scr[hd] + _dot(p.astype(BF16), v)
            m_scr[hd] = m_new

    def body(j, carry):
        attend(j, False)
        return carry

    lax.fori_loop(0, qi, body, 0)
    attend(qi, True)

    oa = acc_scr[0] / l_scr[0]
    ob = acc_scr[1] / l_scr[1]
    o_ref[...] = jnp.where(lane < half, oa, ob).astype(o_ref.dtype)


def _fox_attn_online(q, k, v, f_tok, *, tq=512):
    bsz, s, d = q.shape
    pairs = d // LANES
    nq = s // tq
    ft5 = jnp.swapaxes(f_tok[:, :, :FOX_HEADS], 1, 2).reshape(bsz, pairs, 2, nq, tq)
    return pl.pallas_call(
        _fox_attn_online_kernel,
        grid=(bsz, pairs, nq),
        in_specs=[
            pl.BlockSpec((None, tq, LANES), lambda b, p, i: (b, i, p)),
            pl.BlockSpec((None, s, LANES), lambda b, p, i: (b, 0, p)),
            pl.BlockSpec((None, s, LANES), lambda b, p, i: (b, 0, p)),
            pl.BlockSpec((None, None, 2, nq, tq), lambda b, p, i: (b, p, 0, 0, 0)),
        ],
        out_specs=pl.BlockSpec((None, tq, LANES), lambda b, p, i: (b, i, p)),
        out_shape=jax.ShapeDtypeStruct((bsz, s, d), BF16),
        scratch_shapes=[
            pltpu.VMEM((s, LANES), BF16),
            pltpu.VMEM((s, LANES), BF16),
            pltpu.VMEM((2, tq, LANES), F32),
            pltpu.VMEM((2, tq, LANES), F32),
            pltpu.VMEM((2, tq, LANES), F32),
        ],
        compiler_params=pltpu.CompilerParams(
            dimension_semantics=("arbitrary", "arbitrary", "arbitrary"),
            vmem_limit_bytes=48 * 1024 * 1024),
        name="fox_attn_online",
    )(q, k, v, ft5)


def _fox_attn(q, k, v, f_tok, q_gain, k_gain):
    hd = q.shape[-1] // FOX_HEADS
    shift = (hd ** 0.5) * jnp.max(jnp.abs(q_gain)).astype(F32) * jnp.max(jnp.abs(k_gain)).astype(F32)
    return lax.cond(shift < ATTN_SHIFT_MAX,
                    lambda: _fox_attn_fast(q, k, v, f_tok, shift),
                    lambda: _fox_attn_online(q, k, v, f_tok))


FFN_CHUNK = 512


def _swiglu_rows(hn, wg_ref, wu_ref, wd_ref, acc_scr):
    dff = wg_ref.shape[-1]
    for c in range(dff // FFN_CHUNK):
        cols = slice(c * FFN_CHUNK, (c + 1) * FFN_CHUNK)
        gg = _dot(hn, wg_ref[:, cols])
        uu = _dot(hn, wu_ref[:, cols])
        act = (gg * _sigmoid(gg)) * uu
        acc_scr[...] += _dot(act.astype(BF16), wd_ref[cols, :])


def _ffn_kernel(x_ref, g_ref, wg_ref, wu_ref, wd_ref, o_ref, acc_scr):
    x = x_ref[...]
    acc_scr[...] = x
    _swiglu_rows(_rms_rows(x, g_ref[...]).astype(BF16), wg_ref, wu_ref, wd_ref, acc_scr)
    o_ref[...] = acc_scr[...]


def _resident(shape, index_map):
    return pl.BlockSpec(shape, index_map, pipeline_mode=pl.Buffered(1))


def _ffn_layer(x2, g, w_gate, w_up, w_down, layer, *, tm=512):
    t, d = x2.shape
    dff = w_gate.shape[-1]
    tok = pl.BlockSpec((tm, d), lambda i: (i, 0))
    return pl.pallas_call(
        _ffn_kernel,
        grid=(t // tm,),
        in_specs=[tok, pl.BlockSpec((1, d), lambda i: (0, 0)),
                  _resident((None, d, dff), lambda i: (layer, 0, 0)),
                  _resident((None, d, dff), lambda i: (layer, 0, 0)),
                  _resident((None, dff, d), lambda i: (layer, 0, 0))],
        out_specs=tok,
        out_shape=jax.ShapeDtypeStruct((t, d), F32),
        scratch_shapes=[pltpu.VMEM((tm, d), F32)],
        compiler_params=pltpu.CompilerParams(
            dimension_semantics=("arbitrary",), vmem_limit_bytes=48 * 1024 * 1024),
        name="dense_ffn",
    )(x2, g.reshape(1, d).astype(F32), w_gate, w_up, w_down)


def _top2(logits, lane):
    m1 = jnp.max(logits, axis=1, keepdims=True)
    i1 = jnp.min(jnp.where(logits == m1, lane, LANES), axis=1, keepdims=True)
    rest = jnp.where(lane == i1, NEG_BIG, logits)
    m2 = jnp.max(rest, axis=1, keepdims=True)
    i2 = jnp.min(jnp.where(rest == m2, lane, LANES), axis=1, keepdims=True)
    return m1, i1, m2, i2


def _router_logits(hn, rhi_ref, rlo_ref):
    h_hi = hn.astype(BF16)
    h_lo = (hn - h_hi.astype(F32)).astype(BF16)
    logits = (_dot(h_hi, rhi_ref[...]) + _dot(h_lo, rhi_ref[...])) + _dot(h_hi, rlo_ref[...])
    lane = lax.broadcasted_iota(jnp.int32, logits.shape, 1)
    return jnp.where(lane < N_EXPERTS, logits, NEG_BIG), lane


def _moe_route_kernel(x_ref, o_ref, gate_ref, wout_ref, g_ref, rhi_ref, rlo_ref,
                      x2_ref, idx_ref, w_ref, cnt_ref, carry_scr):
    @pl.when(pl.program_id(0) == 0)
    def _():
        carry_scr[...] = jnp.zeros_like(carry_scr)

    tm = x_ref.shape[0]
    og = (o_ref[...].astype(F32) * gate_ref[...].astype(F32)).astype(BF16)
    x2 = x_ref[...] + _dot(og, wout_ref[...])
    x2_ref[...] = x2
    hn = _rms_rows(x2, g_ref[...])
    logits, lane = _router_logits(hn, rhi_ref, rlo_ref)
    m1, i1, m2, i2 = _top2(logits, lane)
    e2 = jnp.exp(m2 - m1)
    w1 = 1.0 / (1.0 + e2)
    w2 = e2 / (1.0 + e2)
    oh1 = lane == i1
    oh2 = lane == i2
    member = jnp.where(oh1 | oh2, 1.0, 0.0)
    r_i = lax.broadcasted_iota(jnp.int32, (tm, tm), 0)
    c_i = lax.broadcasted_iota(jnp.int32, (tm, tm), 1)
    lstrict = jnp.where(c_i < r_i, 1.0, 0.0).astype(BF16)
    before = _dot(lstrict, member.astype(BF16)) + carry_scr[...]
    rank1 = jnp.sum(jnp.where(oh1, before, 0.0), axis=1, keepdims=True)
    rank2 = jnp.sum(jnp.where(oh2, before, 0.0), axis=1, keepdims=True)
    carry_scr[...] += jnp.sum(member, axis=0, keepdims=True)
    cnt_ref[...] = carry_scr[...]
    w_ref[...] = jnp.where(lane == 0, w1, jnp.where(lane == 1, w2, 0.0))
    cols = (i1.astype(F32), i2.astype(F32), rank1, rank2)
    packed = jnp.zeros((tm, LANES), F32)
    for c, col in enumerate(cols):
        packed = jnp.where(lane == c, col, packed)
    idx_ref[...] = packed.T[:SUBLANES, :].astype(jnp.int32)


def _moe_dispatch_kernel(meta_ref, d_ref, x_ref, xs_ref, zero_scr, sem, zsem, *, n_e):
    tm = x_ref.shape[0]

    for t in range(tm):
        for slot in range(2):
            row = d_ref[slot, t]
            pltpu.make_async_copy(x_ref.at[pl.ds(t, 1), :], xs_ref.at[pl.ds(row, 1), :],
                                  sem).start(priority=slot)
    for slot in range(2):
        pltpu.make_async_copy(x_ref, xs_ref.at[pl.ds(0, tm), :], sem).wait()

    @pl.when(pl.program_id(0) == pl.num_programs(0) - 1)
    def _():
        zero_scr[...] = jnp.zeros_like(zero_scr)
        zero_row = zero_scr.at[pl.ds(0, 1), :]
        for e in range(n_e):
            start = meta_ref[e]
            n_pad = meta_ref[n_e + e]

            def fill(r, carry):
                pltpu.make_async_copy(zero_row, xs_ref.at[pl.ds(start + r, 1), :], zsem).start()
                return carry

            def drain(r, carry):
                pltpu.make_async_copy(zero_row, xs_ref.at[pl.ds(0, 1), :], zsem).wait()
                return carry

            lax.fori_loop(0, n_pad, fill, 0)
            lax.fori_loop(0, n_pad, drain, 0)
        used = meta_ref[2 * n_e]
        tile = zero_scr.shape[0]

        def fill_tile(g, carry):
            row0 = pl.multiple_of((used + g) * tile, tile)
            cp = pltpu.make_async_copy(zero_scr, xs_ref.at[pl.ds(row0, tile), :], zsem)
            cp.start()
            cp.wait()
            return carry

        lax.fori_loop(0, meta_ref[2 * n_e + 1], fill_tile, 0)


def _moe_group_kernel(exp_ref, rows_ref, xs_ref, g_ref, wg_ref, wu_ref, wd_ref, ys_ref, acc_scr):
    live = rows_ref[pl.program_id(0)] > 0

    @pl.when(live)
    def _():
        acc_scr[...] = jnp.zeros_like(acc_scr)
        _swiglu_rows(_rms_rows(xs_ref[...], g_ref[...]).astype(BF16), wg_ref, wu_ref, wd_ref, acc_scr)
        ys_ref[...] = acc_scr[...]

    @pl.when(jnp.logical_not(live))
    def _():
        ys_ref[...] = jnp.zeros_like(ys_ref)


def _moe_combine_kernel(d_ref, x_ref, w_ref, ys_ref, o_ref, buf, sem):
    tm = x_ref.shape[0]

    for t in range(tm):
        for slot in range(2):
            row = d_ref[slot, t]
            pltpu.make_async_copy(ys_ref.at[pl.ds(row, 1), :], buf.at[slot, pl.ds(t, 1), :],
                                  sem).start(priority=slot)
    for slot in range(2):
        pltpu.make_async_copy(ys_ref.at[pl.ds(0, tm), :], buf.at[slot], sem).wait()
    w = w_ref[...]
    o_ref[...] = x_ref[...] + w[:, 0:1] * buf[0] + w[:, 1:2] * buf[1]


def _fox_out_moe_layer(x1, o2, gate2, w_out, g, w_gate, w_up, w_down, router, layer, *, tr=512, tm=512):
    t, d = x1.shape
    _, n_e, _, dff = w_gate.shape
    nt = t // tr
    n_tiles = (2 * t) // tm + n_e
    g_row = g.reshape(1, d).astype(F32)
    r_pad = jnp.pad(router.astype(F32), ((0, 0), (0, LANES - n_e)))
    r_hi = r_pad.astype(BF16)
    r_lo = (r_pad - r_hi.astype(F32)).astype(BF16)

    tok = pl.BlockSpec((tr, d), lambda i: (i, 0))
    x2, idx, wts, counts = pl.pallas_call(
        _moe_route_kernel,
        grid=(nt,),
        in_specs=[tok, tok, tok, _resident((d, d), lambda i: (0, 0)), pl.BlockSpec((1, d), lambda i: (0, 0)),
                  pl.BlockSpec((d, LANES), lambda i: (0, 0)), pl.BlockSpec((d, LANES), lambda i: (0, 0))],
        out_specs=[tok,
                   pl.BlockSpec((None, SUBLANES, tr), lambda i: (i, 0, 0)),
                   pl.BlockSpec((tr, LANES), lambda i: (i, 0)),
                   pl.BlockSpec((1, LANES), lambda i: (0, 0))],
        out_shape=[jax.ShapeDtypeStruct((t, d), F32),
                   jax.ShapeDtypeStruct((nt, SUBLANES, tr), jnp.int32),
                   jax.ShapeDtypeStruct((t, LANES), F32),
                   jax.ShapeDtypeStruct((1, LANES), F32)],
        scratch_shapes=[pltpu.VMEM((1, LANES), F32)],
        compiler_params=pltpu.CompilerParams(
            dimension_semantics=("arbitrary",), vmem_limit_bytes=48 * 1024 * 1024),
        name="fox_out_moe_route",
    )(x1, o2, gate2, w_out.astype(BF16), g_row, r_hi, r_lo)

    cnt = counts[0, :n_e].astype(jnp.int32)
    tiles_e = (cnt + tm - 1) // tm
    ends = jnp.cumsum(tiles_e)
    starts = ends - tiles_e
    used = ends[-1]
    dest = starts[idx[:, 0:2, :]] * tm + idx[:, 2:4, :]
    dest = jnp.concatenate([dest, jnp.zeros((nt, SUBLANES - 2, tr), jnp.int32)], axis=1)
    tile_ids = jnp.arange(n_tiles, dtype=jnp.int32)
    t_exp = jnp.minimum(jnp.sum(tile_ids[:, None] >= ends[None, :], axis=1), n_e - 1).astype(jnp.int32)
    t_rows = jnp.where(tile_ids < used,
                       jnp.clip(cnt[t_exp] - (tile_ids - starts[t_exp]) * tm, 0, tm), 0).astype(jnp.int32)
    meta = jnp.concatenate([starts * tm + cnt, tiles_e * tm - cnt,
                            jnp.stack([used, n_tiles - used])]).astype(jnp.int32)

    smem_rows = lambda: pl.BlockSpec((None, SUBLANES, tr), lambda i, *_: (i, 0, 0), memory_space=pltpu.SMEM)
    xs = pl.pallas_call(
        functools.partial(_moe_dispatch_kernel, n_e=n_e),
        grid_spec=pltpu.PrefetchScalarGridSpec(
            num_scalar_prefetch=1,
            grid=(nt,),
            in_specs=[smem_rows(), pl.BlockSpec((tr, d), lambda i, meta: (i, 0))],
            out_specs=pl.BlockSpec(memory_space=pl.ANY),
            scratch_shapes=[pltpu.VMEM((tm, d), F32), pltpu.SemaphoreType.DMA(()),
                            pltpu.SemaphoreType.DMA(())],
        ),
        out_shape=jax.ShapeDtypeStruct((n_tiles * tm, d), F32),
        compiler_params=pltpu.CompilerParams(
            dimension_semantics=("arbitrary",), vmem_limit_bytes=48 * 1024 * 1024),
        name="moe_dispatch",
    )(meta, dest, x2)

    ys = pl.pallas_call(
        _moe_group_kernel,
        grid_spec=pltpu.PrefetchScalarGridSpec(
            num_scalar_prefetch=2,
            grid=(n_tiles,),
            in_specs=[
                pl.BlockSpec((tm, d), lambda i, ex, rows: (i, 0)),
                pl.BlockSpec((1, d), lambda i, ex, rows: (0, 0)),
                _resident((None, None, d, dff), lambda i, ex, rows: (layer, ex[i], 0, 0)),
                _resident((None, None, d, dff), lambda i, ex, rows: (layer, ex[i], 0, 0)),
                _resident((None, None, dff, d), lambda i, ex, rows: (layer, ex[i], 0, 0)),
            ],
            out_specs=pl.BlockSpec((tm, d), lambda i, ex, rows: (i, 0)),
            scratch_shapes=[pltpu.VMEM((tm, d), F32)],
        ),
        out_shape=jax.ShapeDtypeStruct((n_tiles * tm, d), F32),
        compiler_params=pltpu.CompilerParams(
            dimension_semantics=("arbitrary",), vmem_limit_bytes=48 * 1024 * 1024),
        name="moe_group",
    )(t_exp, t_rows, xs, g_row, w_gate, w_up, w_down)

    return pl.pallas_call(
        _moe_combine_kernel,
        grid=(nt,),
        in_specs=[smem_rows(), pl.BlockSpec((tr, d), lambda i: (i, 0)),
                  pl.BlockSpec((tr, LANES), lambda i: (i, 0)), pl.BlockSpec(memory_space=pl.ANY)],
        out_specs=pl.BlockSpec((tr, d), lambda i: (i, 0)),
        out_shape=jax.ShapeDtypeStruct((t, d), F32),
        scratch_shapes=[pltpu.VMEM((2, tr, d), F32), pltpu.SemaphoreType.DMA(())],
        compiler_params=pltpu.CompilerParams(
            dimension_semantics=("arbitrary",), vmem_limit_bytes=48 * 1024 * 1024),
        name="moe_combine",
    )(dest, x2, wts, ys)


def kernel(x, norm_mix, norm_ffn, lru_w_in, lru_conv_w, lru_conv_b, lru_w_a, lru_b_a, lru_w_x, lru_b_x,
           lru_log_lambda, lru_w_out, fox_w_in, fox_b_f, fox_q_gain, fox_k_gain, fox_w_out, ffn_w_gate,
           ffn_w_up, ffn_w_down, moe_router, moe_w_gate, moe_w_up, moe_w_down):
    bsz, s, d = x.shape
    depth = norm_mix.shape[0]
    ffn_w = [w.astype(BF16) for w in (ffn_w_gate, ffn_w_up, ffn_w_down)]
    moe_w = [w.astype(BF16) for w in (moe_w_gate, moe_w_up, moe_w_down)]
    for i in range(depth):
        j = i // 2
        if i % 2 == 0:
            x = _lru_layer(x, norm_mix[i], lru_w_in[j], lru_conv_w[j], lru_conv_b[j], lru_w_a[j],
                           lru_b_a[j], lru_w_x[j], lru_b_x[j], lru_log_lambda[j], lru_w_out[j])
            x2 = _ffn_layer(x.reshape(bsz * s, d), norm_ffn[i], *ffn_w, j)
        else:
            q, k, v, gate, ft = _fox_proj(x, norm_mix[i], fox_w_in[j], fox_b_f[j], fox_q_gain[j],
                                          fox_k_gain[j])
            o = _fox_attn(q, k, v, ft, fox_q_gain[j], fox_k_gain[j])
            x2 = _fox_out_moe_layer(x.reshape(bsz * s, d), o.reshape(bsz * s, d), gate.reshape(bsz * s, d),
                                    fox_w_out[j], norm_ffn[i], *moe_w, moe_router[j], j)
        x = x2.reshape(bsz, s, d)
    return x
```
